```python
import math
import jax, jax.numpy as jnp
from jax import lax
import numpy as np

D_MODEL = 2048
BATCH = 8
SEQ = 2048
DEPTH = 1

CHUNK = 128
A_WIDTH = D_MODEL
A_GROUPS = 8
A_GROUP_DIM = A_WIDTH // A_GROUPS
B_HEADS = 8
B_HEAD_DIM = D_MODEL // B_HEADS // 2
B_QK = B_HEADS * 2 * B_HEAD_DIM
B_V = B_HEADS * 2 * B_HEAD_DIM
Q_BLOCK = 128
EPS = 1e-6
SUBLN_EPS = 1e-5
SPLIT_SIZES = (A_WIDTH, A_WIDTH, A_WIDTH, B_QK, B_QK, B_V, B_V, D_MODEL, D_MODEL)
IN_COLS = sum(SPLIT_SIZES)
SPLIT_POINTS = tuple(int(v) for v in np.cumsum(SPLIT_SIZES)[:-1])

kernel_name = "hybrid_gmlp_diffattn_gated_block"


def _rmsnorm(x, g, eps=EPS):
    xf = x.astype(jnp.float32)
    y = xf * lax.rsqrt(jnp.mean(xf * xf, axis=-1, keepdims=True) + eps)
    return (y * g.astype(jnp.float32)).astype(x.dtype)


def _layernorm(x, g, b, eps=EPS):
    xf = x.astype(jnp.float32)
    mu = jnp.mean(xf, axis=-1, keepdims=True)
    xc = xf - mu
    y = xc * lax.rsqrt(jnp.mean(xc * xc, axis=-1, keepdims=True) + eps)
    return (y * g.astype(jnp.float32) + b.astype(jnp.float32)).astype(x.dtype)


def _alibi_slopes(n_heads):
    return jnp.asarray([2.0 ** (-8.0 * (i + 1) / n_heads) for i in range(n_heads)], dtype=jnp.float32)


def _lambda_init(layer_idx):
    return 0.8 - 0.6 * math.exp(-0.3 * layer_idx)


def _gmlp_branch(u, v, z, ln_g, ln_b, w_s, b_s):
    bsz, seq, _ = u.shape
    n_chunks = seq // CHUNK
    vn = _layernorm(v, ln_g, ln_b)
    vc = vn.reshape(bsz, n_chunks, CHUNK, A_GROUPS, A_GROUP_DIM)
    causal = jnp.tril(jnp.ones((CHUNK, CHUNK), dtype=w_s.dtype))
    ws = w_s * causal[None]
    sv = jnp.einsum('gts,bnsgd->bntgd', ws, vc) + jnp.transpose(b_s)[None, None, :, :, None]
    y = u * sv.reshape(bsz, seq, A_WIDTH)
    return y * jax.nn.silu(z)


def _diff_attn_branch(q, k, v, z, lam, lam_init, subln_g):
    bsz, seq, _ = q.shape
    n_blocks = seq // Q_BLOCK
    q = q.reshape(bsz, seq, B_HEADS, 2, B_HEAD_DIM)
    k = k.reshape(bsz, seq, B_HEADS, 2, B_HEAD_DIM)
    vh = v.reshape(bsz, seq, B_HEADS, 2 * B_HEAD_DIM)
    qb = jnp.transpose(q.reshape(bsz, n_blocks, Q_BLOCK, B_HEADS, 2, B_HEAD_DIM), (1, 0, 2, 3, 4, 5))
    slopes = _alibi_slopes(B_HEADS)
    scale = 1.0 / math.sqrt(B_HEAD_DIM)
    key_pos = jnp.arange(seq)

    def block(args):
        qi, bi = args
        t_pos = bi * Q_BLOCK + jnp.arange(Q_BLOCK)
        dist = (t_pos[:, None] - key_pos[None, :]).astype(jnp.float32)
        bias = -slopes[:, None, None] * dist[None]
        sc = jnp.einsum('bthcd,bshcd->bhcts', qi, k).astype(jnp.float32) * scale
        sc = sc + bias[None, :, None]
        sc = jnp.where((dist >= 0)[None, None, None], sc, -jnp.inf)
        p = jax.nn.softmax(sc, axis=-1)
        w = p[:, :, 0] - lam * p[:, :, 1]
        return jnp.einsum('bhts,bshe->bthe', w.astype(vh.dtype), vh)

    o = lax.map(block, (qb, jnp.arange(n_blocks)))
    o = jnp.transpose(o, (1, 0, 2, 3, 4)).reshape(bsz, seq, B_HEADS, 2 * B_HEAD_DIM)
    o = _rmsnorm(o, subln_g, SUBLN_EPS) * (1.0 - lam_init)
    return o.reshape(bsz, seq, B_V) * jax.nn.silu(z)


def setup_inputs(seed: int = 0) -> dict:
    key = jax.random.key(seed)
    ks = jax.random.split(key, 20)
    f32 = jnp.float32
    L = DEPTH
    nrm = lambda k, shape, s: jax.random.normal(k, shape, f32) * s
    return {
        "x": jax.random.normal(ks[0], (BATCH, SEQ, D_MODEL), f32),
        "c": jax.random.normal(ks[1], (BATCH, D_MODEL), f32),
        "w_ada": nrm(ks[2], (L, D_MODEL, 3 * D_MODEL), 0.5 * D_MODEL ** -0.5),
        "b_ada": nrm(ks[3], (L, 3 * D_MODEL), 0.01),
        "norm_gain": 1.0 + nrm(ks[4], (L, D_MODEL), 0.02),
        "w_in": nrm(ks[5], (L, D_MODEL, IN_COLS), D_MODEL ** -0.5),
        "ln_v_gain": 1.0 + nrm(ks[6], (L, A_WIDTH), 0.02),
        "ln_v_bias": nrm(ks[7], (L, A_WIDTH), 0.01),
        "w_spatial": nrm(ks[8], (L, A_GROUPS, CHUNK, CHUNK), CHUNK ** -0.5),
        "b_spatial": 1.0 + nrm(ks[9], (L, A_GROUPS, CHUNK), 0.02),
        "lambda_q1": nrm(ks[10], (L, B_HEAD_DIM), 0.1),
        "lambda_k1": nrm(ks[11], (L, B_HEAD_DIM), 0.1),
        "lambda_q2": nrm(ks[12], (L, B_HEAD_DIM), 0.1),
        "lambda_k2": nrm(ks[13], (L, B_HEAD_DIM), 0.1),
        "subln_gain": 1.0 + nrm(ks[14], (L, 2 * B_HEAD_DIM), 0.02),
        "w_branch_a": nrm(ks[15], (L, A_WIDTH, D_MODEL), A_WIDTH ** -0.5),
        "w_branch_b": nrm(ks[16], (L, B_V, D_MODEL), B_V ** -0.5),
        "w_out": nrm(ks[17], (L, D_MODEL, D_MODEL), D_MODEL ** -0.5),
        "final_norm_gain": 1.0 + nrm(ks[18], (D_MODEL,), 0.02),
    }


def reference(x, c, w_ada, b_ada, norm_gain, w_in, ln_v_gain, ln_v_bias, w_spatial, b_spatial,
              lambda_q1, lambda_k1, lambda_q2, lambda_k2, subln_gain, w_branch_a, w_branch_b,
              w_out, final_norm_gain):
    for l in range(DEPTH):
        mod = jax.nn.silu(c) @ w_ada[l] + b_ada[l]
        shift, scale, gate = jnp.split(mod, 3, axis=-1)
        h = _rmsnorm(x, norm_gain[l]) * (1.0 + scale[:, None, :]) + shift[:, None, :]

        proj = jnp.einsum('bsd,de->bse', h, w_in[l])
        a_u, a_v, a_z, b_q, b_k, b_v, b_z, g_a, g_b = jnp.split(proj, SPLIT_POINTS, axis=-1)

        y_a = _gmlp_branch(jax.nn.gelu(a_u), jax.nn.gelu(a_v), a_z, ln_v_gain[l], ln_v_bias[l],
                           w_spatial[l], b_spatial[l])

        lam_init = _lambda_init(l)
        lam = (jnp.exp(jnp.sum(lambda_q1[l].astype(jnp.float32) * lambda_k1[l].astype(jnp.float32)))
               - jnp.exp(jnp.sum(lambda_q2[l].astype(jnp.float32) * lambda_k2[l].astype(jnp.float32)))
               + lam_init)
        y_b = _diff_attn_branch(b_q, b_k, b_v, b_z, lam, lam_init, subln_gain[l])

        m = (jax.nn.sigmoid(g_a) * (y_a @ w_branch_a[l])
             + jax.nn.sigmoid(g_b) * (y_b @ w_branch_b[l]))
        out = m @ w_out[l]
        x = x + gate[:, None, :] * out
    return _rmsnorm(x, final_norm_gain)
```

```python
import functools
import math

import jax
import jax.numpy as jnp
from jax import lax
from jax.experimental import pallas as pl
from jax.experimental.pallas import tpu as pltpu

F32 = jnp.float32
BF16 = jnp.bfloat16

N_HEADS = 8
N_GROUPS = 8
N_SLABS = 9
EPS = 1e-6
SUBLN_EPS = 1e-5
LAMBDA_INIT = 0.8 - 0.6 * math.exp(-0.3 * 0)

VMEM_LIMIT = 56 * 1024 * 1024


def _sigmoid(x):
    return 1.0 / (1.0 + jnp.exp(-x))


def _silu(x):
    return x * _sigmoid(x)


def _gelu_tanh(x):
    c = math.sqrt(2.0 / math.pi)
    return x * (0.5 * (1.0 + jnp.tanh(c * (x + 0.044715 * (x * x * x)))))


def _params(*sem):
    return pltpu.CompilerParams(dimension_semantics=sem, vmem_limit_bytes=VMEM_LIMIT)


def _ada_kernel(c_ref, w_ref, b_ref, o_ref):
    c = c_ref[...]
    o_ref[...] = jnp.dot(_silu(c), w_ref[...], preferred_element_type=F32) + b_ref[...]


def _ada(c, w_ada, b_ada, tn=1024):
    bsz, d = c.shape
    n = w_ada.shape[1]
    return pl.pallas_call(
        _ada_kernel,
        grid=(n // tn,),
        in_specs=[
            pl.BlockSpec((bsz, d), lambda j: (0, 0)),
            pl.BlockSpec((d, tn), lambda j: (0, j)),
            pl.BlockSpec((1, tn), lambda j: (0, j)),
        ],
        out_specs=pl.BlockSpec((bsz, tn), lambda j: (0, j)),
        out_shape=jax.ShapeDtypeStruct((bsz, n), F32),
        compiler_params=_params("parallel"),
        name="ada",
    )(c, w_ada, b_ada.reshape(1, n))


def _inproj_kernel(x_ref, g_ref, shift_ref, scale_ref, w_ref, cs_ref, o_ref, h_ref, *, rows):
    @pl.when(pl.program_id(1) == 0)
    def _():
        g = g_ref[...]
        sc = 1.0 + scale_ref[...]
        sh = shift_ref[...]

        def body(r, carry):
            sl = pl.ds(pl.multiple_of(r * rows, rows), rows)
            x = x_ref[sl, :]
            y = x * lax.rsqrt(jnp.mean(x * x, axis=-1, keepdims=True) + EPS)
            h_ref[sl, :] = ((y * g) * sc + sh).astype(BF16)
            return carry

        lax.fori_loop(0, x_ref.shape[0] // rows, body, 0)

    acc = jnp.dot(h_ref[...], w_ref[...], preferred_element_type=F32)
    o_ref[...] = (acc * cs_ref[...]).astype(o_ref.dtype)


def _inproj(x2, mod3, gain, w_bf, colscale, seq, tm=1024, tn=1024):
    m, d = x2.shape
    n = w_bf.shape[1]
    slab = n // N_SLABS
    per = slab // tn
    tiles_per_seq = seq // tm
    return pl.pallas_call(
        functools.partial(_inproj_kernel, rows=128),
        grid=(m // tm, n // tn),
        in_specs=[
            pl.BlockSpec((tm, d), lambda i, j: (i, 0)),
            pl.BlockSpec((1, d), lambda i, j: (0, 0)),
            pl.BlockSpec((None, 1, d), lambda i, j: (i // tiles_per_seq, 0, 0)),
            pl.BlockSpec((None, 1, d), lambda i, j: (i // tiles_per_seq, 0, 1)),
            pl.BlockSpec((d, tn), lambda i, j: (0, j)),
            pl.BlockSpec((1, tn), lambda i, j: (0, j)),
        ],
        out_specs=pl.BlockSpec((None, tm, tn), lambda i, j: (j // per, i, j % per)),
        out_shape=jax.ShapeDtypeStruct((N_SLABS, m, slab), BF16),
        scratch_shapes=[pltpu.VMEM((tm, d), BF16)],
        compiler_params=_params("parallel", "arbitrary"),
        name="inproj",
    )(x2, gain, mod3, mod3, w_bf, colscale)


def _gmlp_kernel(u_ref, v_ref, z_ref, lng_ref, lnb_ref, ws_ref, bs_ref, o_ref, *, chunk, gd):
    n_chunks = u_ref.shape[0] // chunk
    row = lax.broadcasted_iota(jnp.int32, (chunk, chunk), 0)
    col = lax.broadcasted_iota(jnp.int32, (chunk, chunk), 1)
    causal = (row >= col).astype(F32)
    lng = lng_ref[...]
    lnb = lnb_ref[...]
    for c in range(n_chunks):
        rows = pl.ds(c * chunk, chunk)
        gv = _gelu_tanh(v_ref[rows, :].astype(F32))
        mu = jnp.mean(gv, axis=-1, keepdims=True)
        xc = gv - mu
        var = jnp.mean(xc * xc, axis=-1, keepdims=True)
        vn = ((xc * lax.rsqrt(var + EPS)) * lng + lnb).astype(BF16)
        for g in range(N_GROUPS):
            cols = pl.ds(g * gd, gd)
            ws = (ws_ref[g] * causal).astype(BF16)
            sv = jnp.dot(ws, vn[:, g * gd:(g + 1) * gd], preferred_element_type=F32)
            sv = sv + bs_ref[:, cols]
            u = _gelu_tanh(u_ref[rows, cols].astype(F32))
            z = z_ref[rows, cols].astype(F32)
            o_ref[rows, cols] = ((u * sv) * _silu(z)).astype(o_ref.dtype)


def _gmlp(proj, ln_g, ln_b, w_s, bias_full, ta=256):
    _, m, d = proj.shape
    chunk = w_s.shape[-1]
    spec = lambda s: pl.BlockSpec((None, ta, d), lambda i, s=s: (s, i, 0))
    return pl.pallas_call(
        functools.partial(_gmlp_kernel, chunk=chunk, gd=d // N_GROUPS),
        grid=(m // ta,),
        in_specs=[
            spec(0), spec(1), spec(2),
            pl.BlockSpec((1, d), lambda i: (0, 0)),
            pl.BlockSpec((1, d), lambda i: (0, 0)),
            pl.BlockSpec((N_GROUPS, chunk, chunk), lambda i: (0, 0, 0)),
            pl.BlockSpec((chunk, d), lambda i: (0, 0)),
        ],
        out_specs=pl.BlockSpec((ta, d), lambda i: (i, 0)),
        out_shape=jax.ShapeDtypeStruct((m, d), BF16),
        compiler_params=_params("parallel"),
        name="gmlp",
    )(proj, proj, proj, ln_g, ln_b, w_s, bias_full)


def _attn_kernel(slopes_ref, lq1_ref, lk1_ref, lq2_ref, lk2_ref, q_ref, k_ref, v_ref, z_ref,
                 sg_ref, o_ref, bias_ref, m_ref, l_ref, acc_ref, *, blk, dh):
    h = pl.program_id(1)
    qi = pl.program_id(2)
    slope = slopes_ref[h]

    @pl.when(qi == 0)
    def _():
        r = lax.broadcasted_iota(jnp.int32, (blk, blk), 0)
        c = lax.broadcasted_iota(jnp.int32, (blk, blk), 1)
        bias_ref[...] = slope * (c - r).astype(F32)

    m_ref[...] = jnp.full(m_ref.shape, -jnp.inf, F32)
    l_ref[...] = jnp.zeros(l_ref.shape, F32)
    acc_ref[...] = jnp.zeros(acc_ref.shape, F32)

    def step(j, masked):
        ks = pl.ds(pl.multiple_of(j * blk, blk), blk)
        v = v_ref[ks, :]
        off = slope * ((j - qi) * blk).astype(F32)
        for c in range(2):
            q = q_ref[:, c * dh:(c + 1) * dh]
            k = k_ref[ks, c * dh:(c + 1) * dh]
            s = lax.dot_general(q, k, (((1,), (1,)), ((), ())), preferred_element_type=F32)
            s = s + bias_ref[...]
            if masked:
                r = lax.broadcasted_iota(jnp.int32, (blk, blk), 0)
                cc = lax.broadcasted_iota(jnp.int32, (blk, blk), 1)
                s = jnp.where(r >= cc, s, -jnp.inf)
            m_old = m_ref[c]
            m_new = jnp.maximum(m_old, jnp.max(s, axis=-1, keepdims=True) + off)
            p = jnp.exp(s - (m_new - off))
            alpha = jnp.exp(m_old - m_new)
            l_ref[c] = alpha * l_ref[c] + jnp.sum(p, axis=-1, keepdims=True)
            acc_ref[c] = alpha * acc_ref[c] + jnp.dot(p.astype(v.dtype), v,
                                                      preferred_element_type=F32)
            m_ref[c] = m_new

    def body(j, carry):
        step(j, False)
        return carry

    lax.fori_loop(0, qi, body, 0)
    step(qi, True)

    lam_init = jnp.float32(LAMBDA_INIT)
    lam = (jnp.exp(jnp.sum(lq1_ref[...] * lk1_ref[...], axis=-1, keepdims=True))
           - jnp.exp(jnp.sum(lq2_ref[...] * lk2_ref[...], axis=-1, keepdims=True))
           + lam_init)
    o = acc_ref[0] * (1.0 / l_ref[0]) - lam * (acc_ref[1] * (1.0 / l_ref[1]))
    o = o * lax.rsqrt(jnp.mean(o * o, axis=-1, keepdims=True) + SUBLN_EPS)
    o = (o * sg_ref[...]) * (1.0 - lam_init)
    z = z_ref[...].astype(F32)
    o_ref[...] = (o * _silu(z)).astype(o_ref.dtype)


def _attn(proj, slopes, lq1, lk1, lq2, lk2, subln_g, bsz, seq, blk=512):
    _, m, d = proj.shape
    dv = d // N_HEADS
    dh = dv // 2
    nq = seq // blk
    qspec = lambda s: pl.BlockSpec((None, blk, dv), lambda b, h, i, s=s: (s, b * nq + i, h))
    kvspec = lambda s: pl.BlockSpec((None, seq, dv), lambda b, h, i, s=s: (s, b, h))
    vec = pl.BlockSpec((1, dh), lambda b, h, i: (0, 0))
    return pl.pallas_call(
        functools.partial(_attn_kernel, blk=blk, dh=dh),
        grid=(bsz, N_HEADS, nq),
        in_specs=[
            pl.BlockSpec(memory_space=pltpu.SMEM),
            vec, vec, vec, vec,
            qspec(3), kvspec(4), kvspec(5), qspec(6),
            pl.BlockSpec((1, dv), lambda b, h, i: (0, 0)),
        ],
        out_specs=pl.BlockSpec((blk, dv), lambda b, h, i: (b * nq + i, h)),
        out_shape=jax.ShapeDtypeStruct((m, d), BF16),
        scratch_shapes=[
            pltpu.VMEM((blk, blk), F32),
            pltpu.VMEM((2, blk, 1), F32),
            pltpu.VMEM((2, blk, 1), F32),
            pltpu.VMEM((2, blk, dv), F32),
        ],
        compiler_params=_params("parallel", "arbitrary", "arbitrary"),
        name="attn",
    )(slopes, lq1, lk1, lq2, lk2, proj, proj, proj, proj, subln_g)


def _merge_kernel(ya_ref, yb_ref, ga_ref, gb_ref, wa_ref, wb_ref, o_ref):
    a = jnp.dot(ya_ref[...], wa_ref[...], preferred_element_type=F32)
    b = jnp.dot(yb_ref[...], wb_ref[...], preferred_element_type=F32)
    ga = _sigmoid(ga_ref[...].astype(F32))
    gb = _sigmoid(gb_ref[...].astype(F32))
    o_ref[...] = (ga * a + gb * b).astype(o_ref.dtype)


def _merge(ya, yb, proj, wa, wb, tm=1024, tn=512):
    m, d = ya.shape
    n = wa.shape[1]
    gspec = lambda s: pl.BlockSpec((None, tm, tn), lambda i, j, s=s: (s, i, j))
    return pl.pallas_call(
        _merge_kernel,
        grid=(m // tm, n // tn),
        in_specs=[
            pl.BlockSpec((tm, d), lambda i, j: (i, 0)),
            pl.BlockSpec((tm, d), lambda i, j: (i, 0)),
            gspec(7), gspec(8),
            pl.BlockSpec((d, tn), lambda i, j: (0, j)),
            pl.BlockSpec((d, tn), lambda i, j: (0, j)),
        ],
        out_specs=pl.BlockSpec((tm, tn), lambda i, j: (i, j)),
        out_shape=jax.ShapeDtypeStruct((m, n), BF16),
        compiler_params=_params("parallel", "arbitrary"),
        name="merge",
    )(ya, yb, proj, proj, wa, wb)


def _final_kernel(m_ref, w_ref, x_ref, gate_ref, fg_ref, o_ref):
    out = jnp.dot(m_ref[...], w_ref[...], preferred_element_type=F32)
    r = x_ref[...] + gate_ref[...] * out
    y = r * lax.rsqrt(jnp.mean(r * r, axis=-1, keepdims=True) + EPS)
    o_ref[...] = y * fg_ref[...]


def _final(mm, w_out, x2, mod3, fgain, seq, tm=512):
    m, d = x2.shape
    tiles_per_seq = seq // tm
    return pl.pallas_call(
        _final_kernel,
        grid=(m // tm,),
        in_specs=[
            pl.BlockSpec((tm, d), lambda i: (i, 0)),
            pl.BlockSpec((d, d), lambda i: (0, 0)),
            pl.BlockSpec((tm, d), lambda i: (i, 0)),
            pl.BlockSpec((None, 1, d), lambda i: (i // tiles_per_seq, 0, 2)),
            pl.BlockSpec((1, d), lambda i: (0, 0)),
        ],
        out_specs=pl.BlockSpec((tm, d), lambda i: (i, 0)),
        out_shape=jax.ShapeDtypeStruct((m, d), F32),
        compiler_params=_params("parallel"),
        name="final",
    )(mm, w_out, x2, mod3, fgain)


def kernel(x, c, w_ada, b_ada, norm_gain, w_in, ln_v_gain, ln_v_bias, w_spatial, b_spatial,
           lambda_q1, lambda_k1, lambda_q2, lambda_k2, subln_gain, w_branch_a, w_branch_b,
           w_out, final_norm_gain):
    bsz, seq, d = x.shape
    depth = w_ada.shape[0]
    assert depth == 1, "single-layer trunk"
    dh = lambda_q1.shape[-1]
    chunk = w_spatial.shape[-1]
    assert d == N_HEADS * 2 * dh and w_in.shape[-1] == N_SLABS * d
    l = 0

    x2 = x.reshape(bsz * seq, d)
    mod = _ada(c, w_ada[l], b_ada[l])
    mod3 = mod.reshape(bsz, 1, 3 * d)

    colscale = jnp.ones((N_SLABS, d), F32).at[3].set(1.0 / math.sqrt(dh)).reshape(1, N_SLABS * d)
    proj = _inproj(x2, mod3, norm_gain[l].reshape(1, d), w_in[l].astype(BF16), colscale, seq)

    bias_full = jnp.repeat(jnp.transpose(b_spatial[l]), d // N_GROUPS, axis=1)
    ya = _gmlp(proj, ln_v_gain[l].reshape(1, d), ln_v_bias[l].reshape(1, d), w_spatial[l], bias_full)

    slopes = jnp.asarray([2.0 ** (-8.0 * (i + 1) / N_HEADS) for i in range(N_HEADS)], F32)
    yb = _attn(proj, slopes, lambda_q1[l].reshape(1, dh), lambda_k1[l].reshape(1, dh),
               lambda_q2[l].reshape(1, dh), lambda_k2[l].reshape(1, dh),
               subln_gain[l].reshape(1, 2 * dh), bsz, seq)

    mm = _merge(ya, yb, proj, w_branch_a[l].astype(BF16), w_branch_b[l].astype(BF16))
    out = _final(mm, w_out[l].astype(BF16), x2, mod3, final_norm_gain.reshape(1, d), seq)
    return out.reshape(bsz, seq, d)
```

```python
import functools
import math

import jax
import jax.numpy as jnp
from jax import lax
from jax.experimental import pallas as pl
from jax.experimental.pallas import tpu as pltpu

F32 = jnp.float32
BF16 = jnp.bfloat16

N_HEADS = 8
N_GROUPS = 8
N_SLABS = 9
SLAB_U, SLAB_V, SLAB_Z, SLAB_Q, SLAB_K, SLAB_BV, SLAB_BZ, SLAB_GA, SLAB_GB = range(N_SLABS)
ST = 512
EPS = 1e-6
SUBLN_EPS = 1e-5
LAMBDA_INIT = 0.8 - 0.6 * math.exp(-0.3 * 0)
LOG2E = 1.0 / math.log(2.0)

VMEM_LIMIT = 56 * 1024 * 1024


def _sigmoid(x):
    return 1.0 / (1.0 + jnp.exp(-x))


def _silu(x):
    return x * _sigmoid(x)


def _gelu_tanh(x):
    c = math.sqrt(2.0 / math.pi)
    return x * (0.5 * (1.0 + jnp.tanh(c * (x + 0.044715 * (x * x * x)))))


def _params(*sem):
    return pltpu.CompilerParams(dimension_semantics=sem, vmem_limit_bytes=VMEM_LIMIT)


def _ada_kernel(c_ref, w_ref, b_ref, o_ref):
    c = c_ref[...]
    o_ref[...] = jnp.dot(_silu(c), w_ref[...], preferred_element_type=F32) + b_ref[...]


def _ada(c, w_ada, b_ada, tn=1024):
    bsz, d = c.shape
    n = w_ada.shape[1]
    return pl.pallas_call(
        _ada_kernel,
        grid=(n // tn,),
        in_specs=[
            pl.BlockSpec((bsz, d), lambda j: (0, 0)),
            pl.BlockSpec((d, tn), lambda j: (0, j)),
            pl.BlockSpec((1, tn), lambda j: (0, j)),
        ],
        out_specs=pl.BlockSpec((bsz, tn), lambda j: (0, j)),
        out_shape=jax.ShapeDtypeStruct((bsz, n), F32),
        compiler_params=_params("parallel"),
        name="ada",
    )(c, w_ada, b_ada.reshape(1, n))


def _inproj_kernel(x_ref, g_ref, shift_ref, scale_ref, w_ref, cs_ref, o_ref, h_ref, *,
                   rows, v_lo, v_hi):
    j = pl.program_id(1)

    @pl.when(j == 0)
    def _():
        g = g_ref[...]
        sc = 1.0 + scale_ref[...]
        sh = shift_ref[...]

        def body(r, carry):
            sl = pl.ds(pl.multiple_of(r * rows, rows), rows)
            x = x_ref[sl, :]
            y = x * lax.rsqrt(jnp.mean(x * x, axis=-1, keepdims=True) + EPS)
            h_ref[sl, :] = ((y * g) * sc + sh).astype(BF16)
            return carry

        lax.fori_loop(0, x_ref.shape[0] // rows, body, 0)

    n_tok, n_feat = o_ref.shape[0], o_ref.shape[1]
    is_v = jnp.logical_and(j >= v_lo, j < v_hi)

    @pl.when(jnp.logical_not(is_v))
    def _():
        acc = jnp.dot(h_ref[...], w_ref[...], preferred_element_type=F32) * cs_ref[...]
        for a in range(n_tok):
            for b in range(n_feat):
                o_ref[a, b] = acc[a * ST:(a + 1) * ST, b * ST:(b + 1) * ST].astype(o_ref.dtype)

    @pl.when(is_v)
    def _():
        acc_t = lax.dot_general(w_ref[...], h_ref[...], (((0,), (1,)), ((), ())),
                                preferred_element_type=F32)
        for a in range(n_tok):
            for b in range(n_feat):
                o_ref[a, b] = acc_t[b * ST:(b + 1) * ST, a * ST:(a + 1) * ST].astype(o_ref.dtype)


def _inproj(x2, mod3, gain, w_bf, colscale, seq, tm=1024, tn=1024):
    m, d = x2.shape
    n = w_bf.shape[1]
    slab = n // N_SLABS
    per = slab // tn
    tiles_per_seq = seq // tm
    kern = functools.partial(_inproj_kernel, rows=128, v_lo=SLAB_BV * per, v_hi=(SLAB_BV + 1) * per)
    return pl.pallas_call(
        kern,
        grid=(m // tm, n // tn),
        in_specs=[
            pl.BlockSpec((tm, d), lambda i, j: (i, 0)),
            pl.BlockSpec((1, d), lambda i, j: (0, 0)),
            pl.BlockSpec((None, 1, d), lambda i, j: (i // tiles_per_seq, 0, 0)),
            pl.BlockSpec((None, 1, d), lambda i, j: (i // tiles_per_seq, 0, 1)),
            pl.BlockSpec((d, tn), lambda i, j: (0, j)),
            pl.BlockSpec((1, tn), lambda i, j: (0, j)),
        ],
        out_specs=pl.BlockSpec((None, tm // ST, tn // ST, ST, ST),
                               lambda i, j: (j // per, i, j % per, 0, 0)),
        out_shape=jax.ShapeDtypeStruct((N_SLABS, m // ST, slab // ST, ST, ST), BF16),
        scratch_shapes=[pltpu.VMEM((tm, d), BF16)],
        compiler_params=_params("parallel", "arbitrary"),
        name="inproj",
    )(x2, gain, mod3, mod3, w_bf, colscale)


def _gmlp_kernel(u_ref, v_ref, z_ref, lng_ref, lnb_ref, ws_ref, bs_ref, o_ref, *, chunk, gd):
    n_ft, ta, _ = u_ref.shape
    d = n_ft * ST
    row = lax.broadcasted_iota(jnp.int32, (chunk, chunk), 0)
    col = lax.broadcasted_iota(jnp.int32, (chunk, chunk), 1)
    causal = (row >= col).astype(F32)
    for c in range(ta // chunk):
        rows = pl.ds(c * chunk, chunk)
        gv = [_gelu_tanh(v_ref[f, rows, :].astype(F32)) for f in range(n_ft)]
        mu = sum(jnp.sum(t, axis=-1, keepdims=True) for t in gv) * (1.0 / d)
        xc = [t - mu for t in gv]
        var = sum(jnp.sum(t * t, axis=-1, keepdims=True) for t in xc) * (1.0 / d)
        rstd = lax.rsqrt(var + EPS)
        vn = [((xc[f] * rstd) * lng_ref[:, f * ST:(f + 1) * ST] + lnb_ref[:, f * ST:(f + 1) * ST]
               ).astype(BF16) for f in range(n_ft)]
        for g in range(N_GROUPS):
            f, off = divmod(g * gd, ST)
            ws = (ws_ref[g] * causal).astype(BF16)
            sv = jnp.dot(ws, vn[f][:, off:off + gd], preferred_element_type=F32)
            sv = sv + bs_ref[:, g * gd:(g + 1) * gd]
            u = _gelu_tanh(u_ref[f, rows, off:off + gd].astype(F32))
            z = z_ref[f, rows, off:off + gd].astype(F32)
            o_ref[rows, g * gd:(g + 1) * gd] = ((u * sv) * _silu(z)).astype(o_ref.dtype)


def _gmlp(proj, ln_g, ln_b, w_s, bias_full, ta=256):
    _, n_tt, n_ft, _, _ = proj.shape
    m, d = n_tt * ST, n_ft * ST
    chunk = w_s.shape[-1]
    per = ST // ta
    spec = lambda s: pl.BlockSpec((None, None, n_ft, ta, ST),
                                  lambda i, s=s: (s, i // per, 0, i % per, 0))
    return pl.pallas_call(
        functools.partial(_gmlp_kernel, chunk=chunk, gd=d // N_GROUPS),
        grid=(m // ta,),
        in_specs=[
            spec(SLAB_U), spec(SLAB_V), spec(SLAB_Z),
            pl.BlockSpec((1, d), lambda i: (0, 0)),
            pl.BlockSpec((1, d), lambda i: (0, 0)),
            pl.BlockSpec((N_GROUPS, chunk, chunk), lambda i: (0, 0, 0)),
            pl.BlockSpec((chunk, d), lambda i: (0, 0)),
        ],
        out_specs=pl.BlockSpec((ta, d), lambda i: (i, 0)),
        out_shape=jax.ShapeDtypeStruct((m, d), BF16),
        compiler_params=_params("parallel"),
        name="gmlp",
    )(proj, proj, proj, ln_g, ln_b, w_s, bias_full)


def _attn_kernel(slopes_ref, lq1_ref, lk1_ref, lq2_ref, lk2_ref, q_ref, k_ref, vt_ref, z_ref,
                 sg_ref, o_ref, bias_ref, m_ref, l_ref, acc_ref, *, dh):
    blk = q_ref.shape[0]
    h = pl.program_id(1)
    qi = pl.program_id(2)
    slope = slopes_ref[h] * LOG2E

    @pl.when(qi == 0)
    def _():
        s_pos = lax.broadcasted_iota(jnp.int32, (blk, blk), 0)
        t_pos = lax.broadcasted_iota(jnp.int32, (blk, blk), 1)
        bias_ref[...] = slope * (s_pos - t_pos).astype(F32)

    m_ref[...] = jnp.full(m_ref.shape, -jnp.inf, F32)
    l_ref[...] = jnp.zeros(l_ref.shape, F32)
    acc_ref[...] = jnp.zeros(acc_ref.shape, F32)

    def step(j, masked):
        vt = vt_ref[j]
        off = slope * ((j - qi) * blk).astype(F32)
        for c in range(2):
            q = q_ref[:, c * dh:(c + 1) * dh]
            k = k_ref[j, :, c * dh:(c + 1) * dh]
            s = lax.dot_general(k, q, (((1,), (1,)), ((), ())), preferred_element_type=F32)
            s = s + bias_ref[...]
            if masked:
                s_pos = lax.broadcasted_iota(jnp.int32, (blk, blk), 0)
                t_pos = lax.broadcasted_iota(jnp.int32, (blk, blk), 1)
                s = jnp.where(s_pos <= t_pos, s, -jnp.inf)
            m_old = m_ref[c]
            m_new = jnp.maximum(m_old, jnp.max(s, axis=0, keepdims=True) + off)
            p = jnp.exp2(s - (m_new - off))
            alpha = jnp.exp2(m_old - m_new)
            l_ref[c] = alpha * l_ref[c] + jnp.sum(p, axis=0, keepdims=True)
            acc_ref[c] = alpha * acc_ref[c] + jnp.dot(vt, p.astype(vt.dtype),
                                                      preferred_element_type=F32)
            m_ref[c] = m_new

    def body(j, carry):
        step(j, False)
        return carry

    lax.fori_loop(0, qi, body, 0)
    step(qi, True)

    lam_init = jnp.float32(LAMBDA_INIT)
    lam = (jnp.exp(jnp.sum(lq1_ref[...] * lk1_ref[...], axis=-1, keepdims=True))
           - jnp.exp(jnp.sum(lq2_ref[...] * lk2_ref[...], axis=-1, keepdims=True))
           + lam_init)
    o = acc_ref[0] * (1.0 / l_ref[0]) - lam * (acc_ref[1] * (1.0 / l_ref[1]))
    o = o * lax.rsqrt(jnp.mean(o * o, axis=0, keepdims=True) + SUBLN_EPS)
    o = (o * sg_ref[...]) * (1.0 - lam_init)
    z = z_ref[...].astype(F32)
    o_ref[...] = (o.T * _silu(z)).astype(o_ref.dtype)


def _attn(proj, slopes, lq1, lk1, lq2, lk2, subln_g, bsz, seq):
    _, n_tt, n_ft, _, _ = proj.shape
    m, d = n_tt * ST, n_ft * ST
    dv = d // N_HEADS
    dh = dv // 2
    nq = seq // ST
    hp = ST // dv
    qspec = lambda s: pl.BlockSpec((None, None, None, ST, dv),
                                   lambda b, h, i, s=s: (s, b * nq + i, h // hp, 0, h % hp))
    vec = pl.BlockSpec((1, dh), lambda b, h, i: (0, 0))
    return pl.pallas_call(
        functools.partial(_attn_kernel, dh=dh),
        grid=(bsz, N_HEADS, nq),
        in_specs=[
            pl.BlockSpec(memory_space=pltpu.SMEM),
            vec, vec, vec, vec,
            qspec(SLAB_Q),
            pl.BlockSpec((None, nq, None, ST, dv), lambda b, h, i: (SLAB_K, b, h // hp, 0, h % hp)),
            pl.BlockSpec((None, nq, None, dv, ST), lambda b, h, i: (SLAB_BV, b, h // hp, h % hp, 0)),
            qspec(SLAB_BZ),
            pl.BlockSpec((dv, 1), lambda b, h, i: (0, 0)),
        ],
        out_specs=pl.BlockSpec((ST, dv), lambda b, h, i: (b * nq + i, h)),
        out_shape=jax.ShapeDtypeStruct((m, d), BF16),
        scratch_shapes=[
            pltpu.VMEM((ST, ST), F32),
            pltpu.VMEM((2, 1, ST), F32),
            pltpu.VMEM((2, 1, ST), F32),
            pltpu.VMEM((2, dv, ST), F32),
        ],
        compiler_params=_params("parallel", "arbitrary", "arbitrary"),
        name="attn",
    )(slopes, lq1, lk1, lq2, lk2, proj, proj, proj, proj, subln_g)


def _merge_kernel(ya_ref, yb_ref, ga_ref, gb_ref, wa_ref, wb_ref, o_ref):
    a = jnp.dot(ya_ref[...], wa_ref[...], preferred_element_type=F32)
    b = jnp.dot(yb_ref[...], wb_ref[...], preferred_element_type=F32)
    for t in range(ga_ref.shape[0]):
        rows = slice(t * ST, (t + 1) * ST)
        ga = _sigmoid(ga_ref[t].astype(F32))
        gb = _sigmoid(gb_ref[t].astype(F32))
        o_ref[rows, :] = (ga * a[rows, :] + gb * b[rows, :]).astype(o_ref.dtype)


def _merge(ya, yb, proj, wa, wb, tm=1024):
    m, d = ya.shape
    n = wa.shape[1]
    gspec = lambda s: pl.BlockSpec((None, tm // ST, None, ST, ST),
                                   lambda i, j, s=s: (s, i, j, 0, 0))
    return pl.pallas_call(
        _merge_kernel,
        grid=(m // tm, n // ST),
        in_specs=[
            pl.BlockSpec((tm, d), lambda i, j: (i, 0)),
            pl.BlockSpec((tm, d), lambda i, j: (i, 0)),
            gspec(SLAB_GA), gspec(SLAB_GB),
            pl.BlockSpec((d, ST), lambda i, j: (0, j)),
            pl.BlockSpec((d, ST), lambda i, j: (0, j)),
        ],
        out_specs=pl.BlockSpec((tm, ST), lambda i, j: (i, j)),
        out_shape=jax.ShapeDtypeStruct((m, n), BF16),
        compiler_params=_params("parallel", "arbitrary"),
        name="merge",
    )(ya, yb, proj, proj, wa, wb)


def _final_kernel(m_ref, w_ref, x_ref, gate_ref, fg_ref, o_ref):
    out = jnp.dot(m_ref[...], w_ref[...], preferred_element_type=F32)
    r = x_ref[...] + gate_ref[...] * out
    y = r * lax.rsqrt(jnp.mean(r * r, axis=-1, keepdims=True) + EPS)
    o_ref[...] = y * fg_ref[...]


def _final(mm, w_out, x2, mod3, fgain, seq, tm=512):
    m, d = x2.shape
    tiles_per_seq = seq // tm
    return pl.pallas_call(
        _final_kernel,
        grid=(m // tm,),
        in_specs=[
            pl.BlockSpec((tm, d), lambda i: (i, 0)),
            pl.BlockSpec((d, d), lambda i: (0, 0)),
            pl.BlockSpec((tm, d), lambda i: (i, 0)),
            pl.BlockSpec((None, 1, d), lambda i: (i // tiles_per_seq, 0, 2)),
            pl.BlockSpec((1, d), lambda i: (0, 0)),
        ],
        out_specs=pl.BlockSpec((tm, d), lambda i: (i, 0)),
        out_shape=jax.ShapeDtypeStruct((m, d), F32),
        compiler_params=_params("parallel"),
        name="final",
    )(mm, w_out, x2, mod3, fgain)


def kernel(x, c, w_ada, b_ada, norm_gain, w_in, ln_v_gain, ln_v_bias, w_spatial, b_spatial,
           lambda_q1, lambda_k1, lambda_q2, lambda_k2, subln_gain, w_branch_a, w_branch_b,
           w_out, final_norm_gain):
    bsz, seq, d = x.shape
    depth = w_ada.shape[0]
    assert depth == 1, "single-layer trunk"
    dh = lambda_q1.shape[-1]
    assert d == N_HEADS * 2 * dh and w_in.shape[-1] == N_SLABS * d
    assert seq % ST == 0 and d % ST == 0 and ST % (2 * dh) == 0
    l = 0

    x2 = x.reshape(bsz * seq, d)
    mod = _ada(c, w_ada[l], b_ada[l])
    mod3 = mod.reshape(bsz, 1, 3 * d)

    colscale = jnp.ones((N_SLABS, d), F32).at[SLAB_Q].set(LOG2E / math.sqrt(dh))
    proj = _inproj(x2, mod3, norm_gain[l].reshape(1, d), w_in[l].astype(BF16),
                   colscale.reshape(1, N_SLABS * d), seq)

    bias_full = jnp.repeat(jnp.transpose(b_spatial[l]), d // N_GROUPS, axis=1)
    ya = _gmlp(proj, ln_v_gain[l].reshape(1, d), ln_v_bias[l].reshape(1, d), w_spatial[l], bias_full)

    slopes = jnp.asarray([2.0 ** (-8.0 * (i + 1) / N_HEADS) for i in range(N_HEADS)], F32)
    yb = _attn(proj, slopes, lambda_q1[l].reshape(1, dh), lambda_k1[l].reshape(1, dh),
               lambda_q2[l].reshape(1, dh), lambda_k2[l].reshape(1, dh),
               subln_gain[l].reshape(2 * dh, 1), bsz, seq)

    mm = _merge(ya, yb, proj, w_branch_a[l].astype(BF16), w_branch_b[l].astype(BF16))
    out = _final(mm, w_out[l].astype(BF16), x2, mod3, final_norm_gain.reshape(1, d), seq)
    return out.reshape(bsz, seq, d)
```

```python
import functools
import math

import jax
import jax.numpy as jnp
from jax import lax
from jax.experimental import pallas as pl
from jax.experimental.pallas import tpu as pltpu

F32 = jnp.float32
BF16 = jnp.bfloat16

N_HEADS = 8
N_GROUPS = 8
N_SLABS = 9
SLAB_U, SLAB_V, SLAB_Z, SLAB_Q, SLAB_K, SLAB_BV, SLAB_BZ, SLAB_GA, SLAB_GB = range(N_SLABS)
ST = 512
EPS = 1e-6
SUBLN_EPS = 1e-5
LAMBDA_INIT = 0.8 - 0.6 * math.exp(-0.3 * 0)
LOG2E = 1.0 / math.log(2.0)

VMEM_LIMIT = 56 * 1024 * 1024


def _sigmoid(x):
    return 1.0 / (1.0 + jnp.exp(-x))


def _silu(x):
    return x * _sigmoid(x)


def _gelu_tanh(x):
    c = math.sqrt(2.0 / math.pi)
    return x * (0.5 * (1.0 + jnp.tanh(c * (x + 0.044715 * (x * x * x)))))


def _params(*sem):
    return pltpu.CompilerParams(dimension_semantics=sem, vmem_limit_bytes=VMEM_LIMIT)


def _ada_kernel(c_ref, w_ref, b_ref, o_ref):
    c = c_ref[...]
    o_ref[...] = jnp.dot(_silu(c), w_ref[...], preferred_element_type=F32) + b_ref[...]


def _ada(c, w_ada, b_ada, tn=1024):
    bsz, d = c.shape
    n = w_ada.shape[1]
    return pl.pallas_call(
        _ada_kernel,
        grid=(n // tn,),
        in_specs=[
            pl.BlockSpec((bsz, d), lambda j: (0, 0)),
            pl.BlockSpec((d, tn), lambda j: (0, j)),
            pl.BlockSpec((1, tn), lambda j: (0, j)),
        ],
        out_specs=pl.BlockSpec((bsz, tn), lambda j: (0, j)),
        out_shape=jax.ShapeDtypeStruct((bsz, n), F32),
        compiler_params=_params("parallel"),
        name="ada",
    )(c, w_ada, b_ada.reshape(1, n))


def _inproj_kernel(x_ref, g_ref, shift_ref, scale_ref, w_ref, cs_ref, o_ref, h_ref, *,
                   rows, v_lo, v_hi):
    j = pl.program_id(1)

    @pl.when(j == 0)
    def _():
        g = g_ref[...]
        sc = 1.0 + scale_ref[...]
        sh = shift_ref[...]

        def body(r, carry):
            sl = pl.ds(pl.multiple_of(r * rows, rows), rows)
            x = x_ref[sl, :]
            y = x * lax.rsqrt(jnp.mean(x * x, axis=-1, keepdims=True) + EPS)
            h_ref[sl, :] = ((y * g) * sc + sh).astype(BF16)
            return carry

        lax.fori_loop(0, x_ref.shape[0] // rows, body, 0)

    n_tok, n_feat = o_ref.shape[0], o_ref.shape[1]
    is_v = jnp.logical_and(j >= v_lo, j < v_hi)

    @pl.when(jnp.logical_not(is_v))
    def _():
        acc = jnp.dot(h_ref[...], w_ref[...], preferred_element_type=F32) * cs_ref[...]
        for a in range(n_tok):
            for b in range(n_feat):
                o_ref[a, b] = acc[a * ST:(a + 1) * ST, b * ST:(b + 1) * ST].astype(o_ref.dtype)

    @pl.when(is_v)
    def _():
        acc_t = lax.dot_general(w_ref[...], h_ref[...], (((0,), (1,)), ((), ())),
                                preferred_element_type=F32)
        for a in range(n_tok):
            for b in range(n_feat):
                o_ref[a, b] = acc_t[b * ST:(b + 1) * ST, a * ST:(a + 1) * ST].astype(o_ref.dtype)


def _inproj(x2, mod3, gain, w_bf, colscale, seq, tm=1024, tn=1024):
    m, d = x2.shape
    n = w_bf.shape[1]
    slab = n // N_SLABS
    per = slab // tn
    tiles_per_seq = seq // tm
    kern = functools.partial(_inproj_kernel, rows=128, v_lo=SLAB_BV * per, v_hi=(SLAB_BV + 1) * per)
    return pl.pallas_call(
        kern,
        grid=(m // tm, n // tn),
        in_specs=[
            pl.BlockSpec((tm, d), lambda i, j: (i, 0)),
            pl.BlockSpec((1, d), lambda i, j: (0, 0)),
            pl.BlockSpec((None, 1, d), lambda i, j: (i // tiles_per_seq, 0, 0)),
            pl.BlockSpec((None, 1, d), lambda i, j: (i // tiles_per_seq, 0, 1)),
            pl.BlockSpec((d, tn), lambda i, j: (0, j)),
            pl.BlockSpec((1, tn), lambda i, j: (0, j)),
        ],
        out_specs=pl.BlockSpec((None, tm // ST, tn // ST, ST, ST),
                               lambda i, j: (j // per, i, j % per, 0, 0)),
        out_shape=jax.ShapeDtypeStruct((N_SLABS, m // ST, slab // ST, ST, ST), BF16),
        scratch_shapes=[pltpu.VMEM((tm, d), BF16)],
        compiler_params=_params("parallel", "arbitrary"),
        name="inproj",
    )(x2, gain, mod3, mod3, w_bf, colscale)


def _gmlp_kernel(u_ref, v_ref, z_ref, lng_ref, lnb_ref, ws_ref, bs_ref, o_ref, *, chunk, gd):
    n_ft, ta, _ = u_ref.shape
    d = n_ft * ST
    row = lax.broadcasted_iota(jnp.int32, (chunk, chunk), 0)
    col = lax.broadcasted_iota(jnp.int32, (chunk, chunk), 1)
    causal = (row >= col).astype(F32)
    for c in range(ta // chunk):
        rows = pl.ds(c * chunk, chunk)
        gv = [_gelu_tanh(v_ref[f, rows, :].astype(F32)) for f in range(n_ft)]
        mu = sum(jnp.sum(t, axis=-1, keepdims=True) for t in gv) * (1.0 / d)
        xc = [t - mu for t in gv]
        var = sum(jnp.sum(t * t, axis=-1, keepdims=True) for t in xc) * (1.0 / d)
        rstd = lax.rsqrt(var + EPS)
        vn = [((xc[f] * rstd) * lng_ref[:, f * ST:(f + 1) * ST] + lnb_ref[:, f * ST:(f + 1) * ST]
               ).astype(BF16) for f in range(n_ft)]
        for g in range(N_GROUPS):
            f, off = divmod(g * gd, ST)
            ws = (ws_ref[g] * causal).astype(BF16)
            sv = jnp.dot(ws, vn[f][:, off:off + gd], preferred_element_type=F32)
            sv = sv + bs_ref[:, g * gd:(g + 1) * gd]
            u = _gelu_tanh(u_ref[f, rows, off:off + gd].astype(F32))
            z = z_ref[f, rows, off:off + gd].astype(F32)
            o_ref[rows, g * gd:(g + 1) * gd] = ((u * sv) * _silu(z)).astype(o_ref.dtype)


def _gmlp(proj, ln_g, ln_b, w_s, bias_full, ta=256):
    _, n_tt, n_ft, _, _ = proj.shape
    m, d = n_tt * ST, n_ft * ST
    chunk = w_s.shape[-1]
    per = ST // ta
    spec = lambda s: pl.BlockSpec((None, None, n_ft, ta, ST),
                                  lambda i, s=s: (s, i // per, 0, i % per, 0))
    return pl.pallas_call(
        functools.partial(_gmlp_kernel, chunk=chunk, gd=d // N_GROUPS),
        grid=(m // ta,),
        in_specs=[
            spec(SLAB_U), spec(SLAB_V), spec(SLAB_Z),
            pl.BlockSpec((1, d), lambda i: (0, 0)),
            pl.BlockSpec((1, d), lambda i: (0, 0)),
            pl.BlockSpec((N_GROUPS, chunk, chunk), lambda i: (0, 0, 0)),
            pl.BlockSpec((chunk, d), lambda i: (0, 0)),
        ],
        out_specs=pl.BlockSpec((ta, d), lambda i: (i, 0)),
        out_shape=jax.ShapeDtypeStruct((m, d), BF16),
        compiler_params=_params("parallel"),
        name="gmlp",
    )(proj, proj, proj, ln_g, ln_b, w_s, bias_full)


def _attn_kernel(slopes_ref, lq1_ref, lk1_ref, lq2_ref, lk2_ref, q_ref, k_ref, vt_ref, z_ref,
                 sg_ref, o_ref, bias_ref, s_ref, smax_ref, m_ref, l_ref, acc_ref, *, dh, nq):
    blk = q_ref.shape[0]
    h = pl.program_id(1)
    qi = pl.program_id(2)
    slope = slopes_ref[h] * LOG2E

    @pl.when(qi == 0)
    def _():
        s_pos = lax.broadcasted_iota(jnp.int32, (blk, blk), 0)
        t_pos = lax.broadcasted_iota(jnp.int32, (blk, blk), 1)
        bias = slope * (s_pos - t_pos).astype(F32)
        bias_ref[0] = bias
        bias_ref[1] = jnp.where(s_pos <= t_pos, bias, -jnp.inf)

    m_ref[...] = jnp.full(m_ref.shape, -jnp.inf, F32)
    l_ref[...] = jnp.zeros(l_ref.shape, F32)
    acc_ref[...] = jnp.zeros(acc_ref.shape, F32)

    def scores(j, n):
        slot = j % 2
        bias = bias_ref[int(j == n)]
        for c in range(2):
            q = q_ref[:, c * dh:(c + 1) * dh]
            k = k_ref[j, :, c * dh:(c + 1) * dh]
            s = lax.dot_general(k, q, (((1,), (1,)), ((), ())), preferred_element_type=F32)
            s = s + bias
            s_ref[slot, c] = s
            smax_ref[slot, c] = jnp.max(s, axis=0, keepdims=True)

    def consume(j, n):
        slot = j % 2
        vt = vt_ref[j]
        off = slope * float((j - n) * blk)
        for c in range(2):
            m_old = m_ref[c]
            m_new = jnp.maximum(m_old, smax_ref[slot, c] + off)
            p = jnp.exp2(s_ref[slot, c] - (m_new - off))
            alpha = jnp.exp2(m_old - m_new)
            l_ref[c] = alpha * l_ref[c] + jnp.sum(p, axis=0, keepdims=True)
            acc_ref[c] = alpha * acc_ref[c] + jnp.dot(vt, p.astype(vt.dtype),
                                                      preferred_element_type=F32)
            m_ref[c] = m_new

    for n in range(nq):
        @pl.when(qi == n)
        def _(n=n):
            scores(0, n)
            for j in range(n):
                scores(j + 1, n)
                consume(j, n)
            consume(n, n)

    lam_init = jnp.float32(LAMBDA_INIT)
    lam = (jnp.exp(jnp.sum(lq1_ref[...] * lk1_ref[...], axis=-1, keepdims=True))
           - jnp.exp(jnp.sum(lq2_ref[...] * lk2_ref[...], axis=-1, keepdims=True))
           + lam_init)
    o = acc_ref[0] * (1.0 / l_ref[0]) - lam * (acc_ref[1] * (1.0 / l_ref[1]))
    o = o * lax.rsqrt(jnp.mean(o * o, axis=0, keepdims=True) + SUBLN_EPS)
    o = (o * sg_ref[...]) * (1.0 - lam_init)
    z = z_ref[...].astype(F32)
    o_ref[...] = (o.T * _silu(z)).astype(o_ref.dtype)


def _attn(proj, slopes, lq1, lk1, lq2, lk2, subln_g, bsz, seq):
    _, n_tt, n_ft, _, _ = proj.shape
    m, d = n_tt * ST, n_ft * ST
    dv = d // N_HEADS
    dh = dv // 2
    nq = seq // ST
    hp = ST // dv
    qspec = lambda s: pl.BlockSpec((None, None, None, ST, dv),
                                   lambda b, h, i, s=s: (s, b * nq + i, h // hp, 0, h % hp))
    vec = pl.BlockSpec((1, dh), lambda b, h, i: (0, 0))
    return pl.pallas_call(
        functools.partial(_attn_kernel, dh=dh, nq=nq),
        grid=(bsz, N_HEADS, nq),
        in_specs=[
            pl.BlockSpec(memory_space=pltpu.SMEM),
            vec, vec, vec, vec,
            qspec(SLAB_Q),
            pl.BlockSpec((None, nq, None, ST, dv), lambda b, h, i: (SLAB_K, b, h // hp, 0, h % hp)),
            pl.BlockSpec((None, nq, None, dv, ST), lambda b, h, i: (SLAB_BV, b, h // hp, h % hp, 0)),
            qspec(SLAB_BZ),
            pl.BlockSpec((dv, 1), lambda b, h, i: (0, 0)),
        ],
        out_specs=pl.BlockSpec((ST, dv), lambda b, h, i: (b * nq + i, h)),
        out_shape=jax.ShapeDtypeStruct((m, d), BF16),
        scratch_shapes=[
            pltpu.VMEM((2, ST, ST), F32),
            pltpu.VMEM((2, 2, ST, ST), F32),
            pltpu.VMEM((2, 2, 1, ST), F32),
            pltpu.VMEM((2, 1, ST), F32),
            pltpu.VMEM((2, 1, ST), F32),
            pltpu.VMEM((2, dv, ST), F32),
        ],
        compiler_params=_params("parallel", "arbitrary", "arbitrary"),
        name="attn",
    )(slopes, lq1, lk1, lq2, lk2, proj, proj, proj, proj, subln_g)


def _merge_kernel(ya_ref, yb_ref, ga_ref, gb_ref, wa_ref, wb_ref, o_ref):
    a = jnp.dot(ya_ref[...], wa_ref[...], preferred_element_type=F32)
    b = jnp.dot(yb_ref[...], wb_ref[...], preferred_element_type=F32)
    for t in range(ga_ref.shape[0]):
        rows = slice(t * ST, (t + 1) * ST)
        ga = _sigmoid(ga_ref[t].astype(F32))
        gb = _sigmoid(gb_ref[t].astype(F32))
        o_ref[rows, :] = (ga * a[rows, :] + gb * b[rows, :]).astype(o_ref.dtype)


def _merge(ya, yb, proj, wa, wb, tm=1024):
    m, d = ya.shape
    n = wa.shape[1]
    gspec = lambda s: pl.BlockSpec((None, tm // ST, None, ST, ST),
                                   lambda i, j, s=s: (s, i, j, 0, 0))
    return pl.pallas_call(
        _merge_kernel,
        grid=(m // tm, n // ST),
        in_specs=[
            pl.BlockSpec((tm, d), lambda i, j: (i, 0)),
            pl.BlockSpec((tm, d), lambda i, j: (i, 0)),
            gspec(SLAB_GA), gspec(SLAB_GB),
            pl.BlockSpec((d, ST), lambda i, j: (0, j)),
            pl.BlockSpec((d, ST), lambda i, j: (0, j)),
        ],
        out_specs=pl.BlockSpec((tm, ST), lambda i, j: (i, j)),
        out_shape=jax.ShapeDtypeStruct((m, n), BF16),
        compiler_params=_params("parallel", "arbitrary"),
        name="merge",
    )(ya, yb, proj, proj, wa, wb)


def _final_kernel(m_ref, w_ref, x_ref, gate_ref, fg_ref, o_ref):
    out = jnp.dot(m_ref[...], w_ref[...], preferred_element_type=F32)
    r = x_ref[...] + gate_ref[...] * out
    y = r * lax.rsqrt(jnp.mean(r * r, axis=-1, keepdims=True) + EPS)
    o_ref[...] = y * fg_ref[...]


def _final(mm, w_out, x2, mod3, fgain, seq, tm=512):
    m, d = x2.shape
    tiles_per_seq = seq // tm
    return pl.pallas_call(
        _final_kernel,
        grid=(m // tm,),
        in_specs=[
            pl.BlockSpec((tm, d), lambda i: (i, 0)),
            pl.BlockSpec((d, d), lambda i: (0, 0)),
            pl.BlockSpec((tm, d), lambda i: (i, 0)),
            pl.BlockSpec((None, 1, d), lambda i: (i // tiles_per_seq, 0, 2)),
            pl.BlockSpec((1, d), lambda i: (0, 0)),
        ],
        out_specs=pl.BlockSpec((tm, d), lambda i: (i, 0)),
        out_shape=jax.ShapeDtypeStruct((m, d), F32),
        compiler_params=_params("parallel"),
        name="final",
    )(mm, w_out, x2, mod3, fgain)


def kernel(x, c, w_ada, b_ada, norm_gain, w_in, ln_v_gain, ln_v_bias, w_spatial, b_spatial,
           lambda_q1, lambda_k1, lambda_q2, lambda_k2, subln_gain, w_branch_a, w_branch_b,
           w_out, final_norm_gain):
    bsz, seq, d = x.shape
    depth = w_ada.shape[0]
    assert depth == 1, "single-layer trunk"
    dh = lambda_q1.shape[-1]
    assert d == N_HEADS * 2 * dh and w_in.shape[-1] == N_SLABS * d
    assert seq % ST == 0 and d % ST == 0 and ST % (2 * dh) == 0
    l = 0

    x2 = x.reshape(bsz * seq, d)
    mod = _ada(c, w_ada[l], b_ada[l])
    mod3 = mod.reshape(bsz, 1, 3 * d)

    colscale = jnp.ones((N_SLABS, d), F32).at[SLAB_Q].set(LOG2E / math.sqrt(dh))
    proj = _inproj(x2, mod3, norm_gain[l].reshape(1, d), w_in[l].astype(BF16),
                   colscale.reshape(1, N_SLABS * d), seq)

    bias_full = jnp.repeat(jnp.transpose(b_spatial[l]), d // N_GROUPS, axis=1)
    ya = _gmlp(proj, ln_v_gain[l].reshape(1, d), ln_v_bias[l].reshape(1, d), w_spatial[l], bias_full)

    slopes = jnp.asarray([2.0 ** (-8.0 * (i + 1) / N_HEADS) for i in range(N_HEADS)], F32)
    yb = _attn(proj, slopes, lambda_q1[l].reshape(1, dh), lambda_k1[l].reshape(1, dh),
               lambda_q2[l].reshape(1, dh), lambda_k2[l].reshape(1, dh),
               subln_gain[l].reshape(2 * dh, 1), bsz, seq)

    mm = _merge(ya, yb, proj, w_branch_a[l].astype(BF16), w_branch_b[l].astype(BF16))
    out = _final(mm, w_out[l].astype(BF16), x2, mod3, final_norm_gain.reshape(1, d), seq)
    return out.reshape(bsz, seq, d)
```

```python
import functools
import math

import jax
import jax.numpy as jnp
from jax import lax
from jax.experimental import pallas as pl
from jax.experimental.pallas import tpu as pltpu

F32 = jnp.float32
BF16 = jnp.bfloat16

N_HEADS = 8
N_GROUPS = 8
N_SLABS = 9
SLAB_U, SLAB_V, SLAB_Z, SLAB_Q, SLAB_K, SLAB_BV, SLAB_BZ, SLAB_GA, SLAB_GB = range(N_SLABS)
ST = 512
EPS = 1e-6
SUBLN_EPS = 1e-5
LAMBDA_INIT = 0.8 - 0.6 * math.exp(-0.3 * 0)
LOG2E = 1.0 / math.log(2.0)

VMEM_LIMIT = 56 * 1024 * 1024


def _sigmoid(x):
    return 1.0 / (1.0 + jnp.exp(-x))


def _silu(x):
    return x * _sigmoid(x)


def _gelu_tanh(x):
    c = math.sqrt(2.0 / math.pi)
    return x * (0.5 * (1.0 + jnp.tanh(c * (x + 0.044715 * (x * x * x)))))


def _params(*sem):
    return pltpu.CompilerParams(dimension_semantics=sem, vmem_limit_bytes=VMEM_LIMIT)


def _ada_kernel(c_ref, w_ref, b_ref, o_ref):
    c = c_ref[...]
    o_ref[...] = jnp.dot(_silu(c), w_ref[...], preferred_element_type=F32) + b_ref[...]


def _ada(c, w_ada, b_ada, tn=1024):
    bsz, d = c.shape
    n = w_ada.shape[1]
    return pl.pallas_call(
        _ada_kernel,
        grid=(n // tn,),
        in_specs=[
            pl.BlockSpec((bsz, d), lambda j: (0, 0)),
            pl.BlockSpec((d, tn), lambda j: (0, j)),
            pl.BlockSpec((1, tn), lambda j: (0, j)),
        ],
        out_specs=pl.BlockSpec((bsz, tn), lambda j: (0, j)),
        out_shape=jax.ShapeDtypeStruct((bsz, n), F32),
        compiler_params=_params("parallel"),
        name="ada",
    )(c, w_ada, b_ada.reshape(1, n))


def _inproj_kernel(x_ref, g_ref, shift_ref, scale_ref, w_ref, cs_ref, o_ref, h_ref, *,
                   rows, v_lo, v_hi):
    j = pl.program_id(1)

    @pl.when(j == 0)
    def _():
        g = g_ref[...]
        sc = 1.0 + scale_ref[...]
        sh = shift_ref[...]

        def body(r, carry):
            sl = pl.ds(pl.multiple_of(r * rows, rows), rows)
            x = x_ref[sl, :]
            y = x * lax.rsqrt(jnp.mean(x * x, axis=-1, keepdims=True) + EPS)
            h_ref[sl, :] = ((y * g) * sc + sh).astype(BF16)
            return carry

        lax.fori_loop(0, x_ref.shape[0] // rows, body, 0)

    n_tok, n_feat = o_ref.shape[0], o_ref.shape[1]
    is_v = jnp.logical_and(j >= v_lo, j < v_hi)

    @pl.when(jnp.logical_not(is_v))
    def _():
        acc = jnp.dot(h_ref[...], w_ref[...], preferred_element_type=F32) * cs_ref[...]
        for a in range(n_tok):
            for b in range(n_feat):
                o_ref[a, b] = acc[a * ST:(a + 1) * ST, b * ST:(b + 1) * ST].astype(o_ref.dtype)

    @pl.when(is_v)
    def _():
        acc_t = lax.dot_general(w_ref[...], h_ref[...], (((0,), (1,)), ((), ())),
                                preferred_element_type=F32)
        for a in range(n_tok):
            for b in range(n_feat):
                o_ref[a, b] = acc_t[b * ST:(b + 1) * ST, a * ST:(a + 1) * ST].astype(o_ref.dtype)


def _inproj(x2, mod3, gain, w_bf, colscale, seq, tm=1024, tn=1024):
    m, d = x2.shape
    n = w_bf.shape[1]
    slab = n // N_SLABS
    per = slab // tn
    tiles_per_seq = seq // tm
    kern = functools.partial(_inproj_kernel, rows=128, v_lo=SLAB_BV * per, v_hi=(SLAB_BV + 1) * per)
    return pl.pallas_call(
        kern,
        grid=(m // tm, n // tn),
        in_specs=[
            pl.BlockSpec((tm, d), lambda i, j: (i, 0)),
            pl.BlockSpec((1, d), lambda i, j: (0, 0)),
            pl.BlockSpec((None, 1, d), lambda i, j: (i // tiles_per_seq, 0, 0)),
            pl.BlockSpec((None, 1, d), lambda i, j: (i // tiles_per_seq, 0, 1)),
            pl.BlockSpec((d, tn), lambda i, j: (0, j)),
            pl.BlockSpec((1, tn), lambda i, j: (0, j)),
        ],
        out_specs=pl.BlockSpec((None, tm // ST, tn // ST, ST, ST),
                               lambda i, j: (j // per, i, j % per, 0, 0)),
        out_shape=jax.ShapeDtypeStruct((N_SLABS, m // ST, slab // ST, ST, ST), BF16),
        scratch_shapes=[pltpu.VMEM((tm, d), BF16)],
        compiler_params=_params("parallel", "arbitrary"),
        name="inproj",
    )(x2, gain, mod3, mod3, w_bf, colscale)


def _gmlp_tile(u_ref, v_ref, z_ref, lng_ref, lnb_ref, ws_ref, bs_ref, o_ref, chunk, gd):
    n_ft, ta, _ = u_ref.shape
    d = n_ft * ST
    row = lax.broadcasted_iota(jnp.int32, (chunk, chunk), 0)
    col = lax.broadcasted_iota(jnp.int32, (chunk, chunk), 1)
    causal = (row >= col).astype(F32)
    for c in range(ta // chunk):
        rows = pl.ds(c * chunk, chunk)
        gv = [_gelu_tanh(v_ref[f, rows, :].astype(F32)) for f in range(n_ft)]
        mu = sum(jnp.sum(t, axis=-1, keepdims=True) for t in gv) * (1.0 / d)
        xc = [t - mu for t in gv]
        var = sum(jnp.sum(t * t, axis=-1, keepdims=True) for t in xc) * (1.0 / d)
        rstd = lax.rsqrt(var + EPS)
        vn = [((xc[f] * rstd) * lng_ref[:, f * ST:(f + 1) * ST] + lnb_ref[:, f * ST:(f + 1) * ST]
               ).astype(BF16) for f in range(n_ft)]
        for g in range(N_GROUPS):
            f, off = divmod(g * gd, ST)
            ws = (ws_ref[g] * causal).astype(BF16)
            sv = jnp.dot(ws, vn[f][:, off:off + gd], preferred_element_type=F32)
            sv = sv + bs_ref[:, g * gd:(g + 1) * gd]
            u = _gelu_tanh(u_ref[f, rows, off:off + gd].astype(F32))
            z = z_ref[f, rows, off:off + gd].astype(F32)
            o_ref[rows, g * gd:(g + 1) * gd] = ((u * sv) * _silu(z)).astype(o_ref.dtype)


def _attn_kernel(slopes_ref, lq1_ref, lk1_ref, lq2_ref, lk2_ref, q_ref, k_ref, vt_ref, z_ref,
                 sg_ref, o_ref, bias_ref, s_ref, smax_ref, m_ref, l_ref, acc_ref, *, dh, nq):
    blk = q_ref.shape[0]
    h = pl.program_id(1)
    qi = pl.program_id(2)
    slope = slopes_ref[h] * LOG2E

    @pl.when(qi == 0)
    def _():
        s_pos = lax.broadcasted_iota(jnp.int32, (blk, blk), 0)
        t_pos = lax.broadcasted_iota(jnp.int32, (blk, blk), 1)
        bias = slope * (s_pos - t_pos).astype(F32)
        bias_ref[0] = bias
        bias_ref[1] = jnp.where(s_pos <= t_pos, bias, -jnp.inf)

    m_ref[...] = jnp.full(m_ref.shape, -jnp.inf, F32)
    l_ref[...] = jnp.zeros(l_ref.shape, F32)
    acc_ref[...] = jnp.zeros(acc_ref.shape, F32)

    def scores(j, n):
        slot = j % 2
        bias = bias_ref[int(j == n)]
        for c in range(2):
            q = q_ref[:, c * dh:(c + 1) * dh]
            k = k_ref[j, :, c * dh:(c + 1) * dh]
            s = lax.dot_general(k, q, (((1,), (1,)), ((), ())), preferred_element_type=F32)
            s = s + bias
            s_ref[slot, c] = s
            smax_ref[slot, c] = jnp.max(s, axis=0, keepdims=True)

    def consume(j, n):
        slot = j % 2
        vt = vt_ref[j]
        off = slope * float((j - n) * blk)
        for c in range(2):
            m_old = m_ref[c]
            m_new = jnp.maximum(m_old, smax_ref[slot, c] + off)
            p = jnp.exp2(s_ref[slot, c] - (m_new - off))
            alpha = jnp.exp2(m_old - m_new)
            l_ref[c] = alpha * l_ref[c] + jnp.sum(p, axis=0, keepdims=True)
            acc_ref[c] = alpha * acc_ref[c] + jnp.dot(vt, p.astype(vt.dtype),
                                                      preferred_element_type=F32)
            m_ref[c] = m_new

    for n in range(nq):
        @pl.when(qi == n)
        def _(n=n):
            scores(0, n)
            for j in range(n):
                scores(j + 1, n)
                consume(j, n)
            consume(n, n)

    lam_init = jnp.float32(LAMBDA_INIT)
    lam = (jnp.exp(jnp.sum(lq1_ref[...] * lk1_ref[...], axis=-1, keepdims=True))
           - jnp.exp(jnp.sum(lq2_ref[...] * lk2_ref[...], axis=-1, keepdims=True))
           + lam_init)
    o = acc_ref[0] * (1.0 / l_ref[0]) - lam * (acc_ref[1] * (1.0 / l_ref[1]))
    o = o * lax.rsqrt(jnp.mean(o * o, axis=0, keepdims=True) + SUBLN_EPS)
    o = (o * sg_ref[...]) * (1.0 - lam_init)
    z = z_ref[...].astype(F32)
    o_ref[...] = (o.T * _silu(z)).astype(o_ref.dtype)


def _attn(proj, slopes, lq1, lk1, lq2, lk2, subln_g, bsz, seq):
    _, n_tt, n_ft, _, _ = proj.shape
    m, d = n_tt * ST, n_ft * ST
    dv = d // N_HEADS
    dh = dv // 2
    nq = seq // ST
    hp = ST // dv
    qspec = lambda s: pl.BlockSpec((None, None, None, ST, dv),
                                   lambda b, h, i, s=s: (s, b * nq + i, h // hp, 0, h % hp))
    vec = pl.BlockSpec((1, dh), lambda b, h, i: (0, 0))
    return pl.pallas_call(
        functools.partial(_attn_kernel, dh=dh, nq=nq),
        grid=(bsz, N_HEADS, nq),
        in_specs=[
            pl.BlockSpec(memory_space=pltpu.SMEM),
            vec, vec, vec, vec,
            qspec(SLAB_Q),
            pl.BlockSpec((None, nq, None, ST, dv), lambda b, h, i: (SLAB_K, b, h // hp, 0, h % hp)),
            pl.BlockSpec((None, nq, None, dv, ST), lambda b, h, i: (SLAB_BV, b, h // hp, h % hp, 0)),
            qspec(SLAB_BZ),
            pl.BlockSpec((dv, 1), lambda b, h, i: (0, 0)),
        ],
        out_specs=pl.BlockSpec((ST, dv), lambda b, h, i: (b * nq + i, h)),
        out_shape=jax.ShapeDtypeStruct((m, d), BF16),
        scratch_shapes=[
            pltpu.VMEM((2, ST, ST), F32),
            pltpu.VMEM((2, 2, ST, ST), F32),
            pltpu.VMEM((2, 2, 1, ST), F32),
            pltpu.VMEM((2, 1, ST), F32),
            pltpu.VMEM((2, 1, ST), F32),
            pltpu.VMEM((2, dv, ST), F32),
        ],
        compiler_params=_params("parallel", "arbitrary", "arbitrary"),
        name="attn",
    )(slopes, lq1, lk1, lq2, lk2, proj, proj, proj, proj, subln_g)


def _tail_kernel(u_ref, v_ref, z_ref, ga_ref, gb_ref, yb_ref, x_ref, gate_ref, wa_ref, wb_ref,
                 wo_ref, lng_ref, lnb_ref, ws_ref, bs_ref, fg_ref, o_ref, ya_ref, m_ref, *,
                 chunk, gd):
    b = jnp.dot(yb_ref[...], wb_ref[...], preferred_element_type=F32)
    _gmlp_tile(u_ref, v_ref, z_ref, lng_ref, lnb_ref, ws_ref, bs_ref, ya_ref, chunk, gd)
    a = jnp.dot(ya_ref[...], wa_ref[...], preferred_element_type=F32)
    for f in range(ga_ref.shape[0]):
        cols = slice(f * ST, (f + 1) * ST)
        ga = _sigmoid(ga_ref[f].astype(F32))
        gb = _sigmoid(gb_ref[f].astype(F32))
        m_ref[:, cols] = (ga * a[:, cols] + gb * b[:, cols]).astype(m_ref.dtype)
    out = jnp.dot(m_ref[...], wo_ref[...], preferred_element_type=F32)
    r = x_ref[...] + gate_ref[...] * out
    y = r * lax.rsqrt(jnp.mean(r * r, axis=-1, keepdims=True) + EPS)
    o_ref[...] = y * fg_ref[...]


def _tail(proj, yb, x2, mod3, wa, wb, wo, ln_g, ln_b, w_s, bias_full, fgain, seq, tm=256):
    _, n_tt, n_ft, _, _ = proj.shape
    m, d = x2.shape
    chunk = w_s.shape[-1]
    per = ST // tm
    tiles_per_seq = seq // tm
    pspec = lambda s: pl.BlockSpec((None, None, n_ft, tm, ST),
                                   lambda i, s=s: (s, i // per, 0, i % per, 0))
    rows = pl.BlockSpec((tm, d), lambda i: (i, 0))
    vec = pl.BlockSpec((1, d), lambda i: (0, 0))
    weight = pl.BlockSpec((d, d), lambda i: (0, 0), pipeline_mode=pl.Buffered(1))
    return pl.pallas_call(
        functools.partial(_tail_kernel, chunk=chunk, gd=d // N_GROUPS),
        grid=(m // tm,),
        in_specs=[
            pspec(SLAB_U), pspec(SLAB_V), pspec(SLAB_Z), pspec(SLAB_GA), pspec(SLAB_GB),
            rows, rows,
            pl.BlockSpec((None, 1, d), lambda i: (i // tiles_per_seq, 0, 2)),
            weight, weight, weight,
            vec, vec,
            pl.BlockSpec((N_GROUPS, chunk, chunk), lambda i: (0, 0, 0)),
            pl.BlockSpec((chunk, d), lambda i: (0, 0)),
            vec,
        ],
        out_specs=rows,
        out_shape=jax.ShapeDtypeStruct((m, d), F32),
        scratch_shapes=[pltpu.VMEM((tm, d), BF16), pltpu.VMEM((tm, d), BF16)],
        compiler_params=_params("parallel"),
        name="tail",
    )(proj, proj, proj, proj, proj, yb, x2, mod3, wa, wb, wo, ln_g, ln_b, w_s, bias_full, fgain)


def kernel(x, c, w_ada, b_ada, norm_gain, w_in, ln_v_gain, ln_v_bias, w_spatial, b_spatial,
           lambda_q1, lambda_k1, lambda_q2, lambda_k2, subln_gain, w_branch_a, w_branch_b,
           w_out, final_norm_gain):
    bsz, seq, d = x.shape
    depth = w_ada.shape[0]
    assert depth == 1, "single-layer trunk"
    dh = lambda_q1.shape[-1]
    assert d == N_HEADS * 2 * dh and w_in.shape[-1] == N_SLABS * d
    assert seq % ST == 0 and d % ST == 0 and ST % (2 * dh) == 0
    l = 0

    x2 = x.reshape(bsz * seq, d)
    mod = _ada(c, w_ada[l], b_ada[l])
    mod3 = mod.reshape(bsz, 1, 3 * d)

    colscale = jnp.ones((N_SLABS, d), F32).at[SLAB_Q].set(LOG2E / math.sqrt(dh))
    proj = _inproj(x2, mod3, norm_gain[l].reshape(1, d), w_in[l].astype(BF16),
                   colscale.reshape(1, N_SLABS * d), seq)

    bias_full = jnp.repeat(jnp.transpose(b_spatial[l]), d // N_GROUPS, axis=1)

    slopes = jnp.asarray([2.0 ** (-8.0 * (i + 1) / N_HEADS) for i in range(N_HEADS)], F32)
    yb = _attn(proj, slopes, lambda_q1[l].reshape(1, dh), lambda_k1[l].reshape(1, dh),
               lambda_q2[l].reshape(1, dh), lambda_k2[l].reshape(1, dh),
               subln_gain[l].reshape(2 * dh, 1), bsz, seq)

    out = _tail(proj, yb, x2, mod3, w_branch_a[l].astype(BF16), w_branch_b[l].astype(BF16),
                w_out[l].astype(BF16), ln_v_gain[l].reshape(1, d), ln_v_bias[l].reshape(1, d),
                w_spatial[l], bias_full, final_norm_gain.reshape(1, d), seq)
    return out.reshape(bsz, seq, d)
```

```python
import functools
import math

import jax
import jax.numpy as jnp
from jax import lax
from jax.experimental import pallas as pl
from jax.experimental.pallas import tpu as pltpu

F32 = jnp.float32
BF16 = jnp.bfloat16

N_HEADS = 8
N_GROUPS = 8
N_SLABS = 9
SLAB_U, SLAB_V, SLAB_Z, SLAB_Q, SLAB_K, SLAB_BV, SLAB_BZ, SLAB_GA, SLAB_GB = range(N_SLABS)
ST = 512
EPS = 1e-6
SUBLN_EPS = 1e-5
LAMBDA_INIT = 0.8 - 0.6 * math.exp(-0.3 * 0)
LOG2E = 1.0 / math.log(2.0)

VMEM_LIMIT = 56 * 1024 * 1024


def _sigmoid(x):
    return 1.0 / (1.0 + jnp.exp(-x))


def _silu(x):
    return x * _sigmoid(x)


def _gelu_tanh(x):
    c = math.sqrt(2.0 / math.pi)
    return x * (0.5 * (1.0 + jnp.tanh(c * (x + 0.044715 * (x * x * x)))))


def _params(*sem):
    return pltpu.CompilerParams(dimension_semantics=sem, vmem_limit_bytes=VMEM_LIMIT)


def _ada_kernel(c_ref, w_ref, b_ref, o_ref):
    c = c_ref[...]
    o_ref[...] = jnp.dot(_silu(c), w_ref[...], preferred_element_type=F32) + b_ref[...]


def _ada(c, w_ada, b_ada, tn=1024):
    bsz, d = c.shape
    n = w_ada.shape[1]
    return pl.pallas_call(
        _ada_kernel,
        grid=(n // tn,),
        in_specs=[
            pl.BlockSpec((bsz, d), lambda j: (0, 0)),
            pl.BlockSpec((d, tn), lambda j: (0, j)),
            pl.BlockSpec((1, tn), lambda j: (0, j)),
        ],
        out_specs=pl.BlockSpec((bsz, tn), lambda j: (0, j)),
        out_shape=jax.ShapeDtypeStruct((bsz, n), F32),
        compiler_params=_params("parallel"),
        name="ada",
    )(c, w_ada, b_ada.reshape(1, n))


def _inproj_kernel(x_ref, g_ref, shift_ref, scale_ref, w_ref, o_ref, h_ref, *,
                   rows, per, q_scale):
    j = pl.program_id(1)

    @pl.when(j == 0)
    def _():
        g = g_ref[...]
        sc = 1.0 + scale_ref[...]
        sh = shift_ref[...]

        def body(r, carry):
            sl = pl.ds(pl.multiple_of(r * rows, rows), rows)
            x = x_ref[sl, :]
            y = x * lax.rsqrt(jnp.mean(x * x, axis=-1, keepdims=True) + EPS)
            h_ref[sl, :] = ((y * g) * sc + sh).astype(BF16)
            return carry

        lax.fori_loop(0, x_ref.shape[0] // rows, body, 0)

    n_tok, n_feat = o_ref.shape[0], o_ref.shape[1]
    slab = j // per
    epilogues = (
        ((SLAB_U, SLAB_V), _gelu_tanh),
        ((SLAB_Z,), _silu),
        ((SLAB_GA, SLAB_GB), _sigmoid),
        ((SLAB_Q,), lambda acc: acc * q_scale),
        ((SLAB_K,), lambda acc: acc),
    )
    for slabs, act in epilogues:
        @pl.when(functools.reduce(jnp.logical_or, [slab == s for s in slabs]))
        def _(act=act):
            acc = act(jnp.dot(h_ref[...], w_ref[...], preferred_element_type=F32))
            for a in range(n_tok):
                for b in range(n_feat):
                    o_ref[a, b] = acc[a * ST:(a + 1) * ST, b * ST:(b + 1) * ST].astype(o_ref.dtype)

    for s_t, act in ((SLAB_BV, lambda acc: acc), (SLAB_BZ, _silu)):
        @pl.when(slab == s_t)
        def _(act=act):
            acc_t = act(lax.dot_general(w_ref[...], h_ref[...], (((0,), (1,)), ((), ())),
                                        preferred_element_type=F32))
            for a in range(n_tok):
                for b in range(n_feat):
                    o_ref[a, b] = acc_t[b * ST:(b + 1) * ST, a * ST:(a + 1) * ST].astype(o_ref.dtype)


def _inproj(x2, mod3, gain, w_bf, q_scale, seq, tm=1024, tn=1024):
    m, d = x2.shape
    n = w_bf.shape[1]
    slab = n // N_SLABS
    per = slab // tn
    tiles_per_seq = seq // tm
    return pl.pallas_call(
        functools.partial(_inproj_kernel, rows=128, per=per, q_scale=q_scale),
        grid=(m // tm, n // tn),
        in_specs=[
            pl.BlockSpec((tm, d), lambda i, j: (i, 0)),
            pl.BlockSpec((1, d), lambda i, j: (0, 0)),
            pl.BlockSpec((None, 1, d), lambda i, j: (i // tiles_per_seq, 0, 0)),
            pl.BlockSpec((None, 1, d), lambda i, j: (i // tiles_per_seq, 0, 1)),
            pl.BlockSpec((d, tn), lambda i, j: (0, j)),
        ],
        out_specs=pl.BlockSpec((None, tm // ST, tn // ST, ST, ST),
                               lambda i, j: (j // per, i, j % per, 0, 0)),
        out_shape=jax.ShapeDtypeStruct((N_SLABS, m // ST, slab // ST, ST, ST), BF16),
        scratch_shapes=[pltpu.VMEM((tm, d), BF16)],
        compiler_params=_params("parallel", "arbitrary"),
        name="inproj",
    )(x2, gain, mod3, mod3, w_bf)


def _gmlp_tile(u_ref, v_ref, z_ref, lng_ref, lnb_ref, ws_ref, bs_ref, o_ref, chunk, gd):
    n_ft, ta, _ = u_ref.shape
    d = n_ft * ST
    row = lax.broadcasted_iota(jnp.int32, (chunk, chunk), 0)
    col = lax.broadcasted_iota(jnp.int32, (chunk, chunk), 1)
    causal = (row >= col).astype(F32)
    for c in range(ta // chunk):
        rows = pl.ds(c * chunk, chunk)
        gv = [v_ref[f, rows, :].astype(F32) for f in range(n_ft)]
        mu = sum(jnp.sum(t, axis=-1, keepdims=True) for t in gv) * (1.0 / d)
        xc = [t - mu for t in gv]
        var = sum(jnp.sum(t * t, axis=-1, keepdims=True) for t in xc) * (1.0 / d)
        rstd = lax.rsqrt(var + EPS)
        vn = [((xc[f] * rstd) * lng_ref[:, f * ST:(f + 1) * ST] + lnb_ref[:, f * ST:(f + 1) * ST]
               ).astype(BF16) for f in range(n_ft)]
        for g in range(N_GROUPS):
            f, off = divmod(g * gd, ST)
            ws = (ws_ref[g] * causal).astype(BF16)
            sv = jnp.dot(ws, vn[f][:, off:off + gd], preferred_element_type=F32)
            sv = sv + bs_ref[:, g * gd:(g + 1) * gd]
            u = u_ref[f, rows, off:off + gd].astype(F32)
            z = z_ref[f, rows, off:off + gd].astype(F32)
            o_ref[rows, g * gd:(g + 1) * gd] = ((u * sv) * z).astype(o_ref.dtype)


def _attn_kernel(slopes_ref, lq1_ref, lk1_ref, lq2_ref, lk2_ref, q_ref, k_ref, vt_ref, zt_ref,
                 sg_ref, o_ref, bias_ref, vta_ref, s_ref, smax_ref, acc_ref, *, dh):
    nq, blk, dv = q_ref.shape
    slope = slopes_ref[pl.program_id(1)] * LOG2E

    s_pos = lax.broadcasted_iota(jnp.int32, (blk, blk), 0)
    t_pos = lax.broadcasted_iota(jnp.int32, (blk, blk), 1)
    bias = slope * (s_pos - t_pos).astype(F32)
    bias_ref[0] = bias
    bias_ref[1] = jnp.where(s_pos <= t_pos, bias, -jnp.inf)

    for j in range(nq):
        vta_ref[j, :dv, :] = vt_ref[j]
        vta_ref[j, dv:, :] = jnp.ones((vta_ref.shape[1] - dv, blk), vta_ref.dtype)

    lam_init = jnp.float32(LAMBDA_INIT)
    lam = (jnp.exp(jnp.sum(lq1_ref[...] * lk1_ref[...], axis=-1, keepdims=True))
           - jnp.exp(jnp.sum(lq2_ref[...] * lk2_ref[...], axis=-1, keepdims=True))
           + lam_init)
    gain = sg_ref[...] * (1.0 - lam_init)

    def scores(j, n):
        slot = j % 2
        bias = bias_ref[int(j == n)]
        for c in range(2):
            q = q_ref[n, :, c * dh:(c + 1) * dh]
            k = k_ref[j, :, c * dh:(c + 1) * dh]
            s = lax.dot_general(k, q, (((1,), (1,)), ((), ())), preferred_element_type=F32)
            s = s + bias
            s_ref[slot, c] = s
            smax_ref[slot, c] = jnp.max(s, axis=0, keepdims=True)

    def consume(j, n, m):
        slot = j % 2
        vta = vta_ref[j]
        off = slope * float((j - n) * blk)
        for c in range(2):
            m_new = smax_ref[slot, c] + off
            if m[c] is not None:
                m_new = jnp.maximum(m[c], m_new)
            p = jnp.exp2(s_ref[slot, c] - (m_new - off)).astype(vta.dtype)
            pv = jnp.dot(vta, p, preferred_element_type=F32)
            if m[c] is None:
                acc_ref[c] = pv
            else:
                acc_ref[c] = jnp.exp2(m[c] - m_new) * acc_ref[c] + pv
            m[c] = m_new

    for n in range(nq):
        m = [None, None]
        scores(0, n)
        for j in range(n):
            scores(j + 1, n)
            consume(j, n, m)
        consume(n, n, m)

        a0, a1 = acc_ref[0], acc_ref[1]
        r0 = 1.0 / a0[dv:dv + 1]
        r1 = lam * (1.0 / a1[dv:dv + 1])
        o = a0[:dv] * r0 - a1[:dv] * r1
        o = o * lax.rsqrt(jnp.mean(o * o, axis=0, keepdims=True) + SUBLN_EPS)
        o = (o * gain) * zt_ref[n].astype(F32)
        o_ref[:, n * blk:(n + 1) * blk] = o.astype(o_ref.dtype)


def _attn(proj, slopes, lq1, lk1, lq2, lk2, subln_g, bsz, seq):
    _, n_tt, n_ft, _, _ = proj.shape
    m, d = n_tt * ST, n_ft * ST
    dv = d // N_HEADS
    dh = dv // 2
    nq = seq // ST
    hp = ST // dv
    ones_rows = 16
    tok_major = lambda s: pl.BlockSpec((None, nq, None, ST, dv),
                                       lambda b, h, s=s: (s, b, h // hp, 0, h % hp))
    feat_major = lambda s: pl.BlockSpec((None, nq, None, dv, ST),
                                        lambda b, h, s=s: (s, b, h // hp, h % hp, 0))
    vec = pl.BlockSpec((1, dh), lambda b, h: (0, 0))
    return pl.pallas_call(
        functools.partial(_attn_kernel, dh=dh),
        grid=(bsz, N_HEADS),
        in_specs=[
            pl.BlockSpec(memory_space=pltpu.SMEM),
            vec, vec, vec, vec,
            tok_major(SLAB_Q), tok_major(SLAB_K), feat_major(SLAB_BV), feat_major(SLAB_BZ),
            pl.BlockSpec((dv, 1), lambda b, h: (0, 0)),
        ],
        out_specs=pl.BlockSpec((dv, seq), lambda b, h: (h, b)),
        out_shape=jax.ShapeDtypeStruct((d, m), BF16),
        scratch_shapes=[
            pltpu.VMEM((2, ST, ST), F32),
            pltpu.VMEM((nq, dv + ones_rows, ST), BF16),
            pltpu.VMEM((2, 2, ST, ST), F32),
            pltpu.VMEM((2, 2, 1, ST), F32),
            pltpu.VMEM((2, dv + ones_rows, ST), F32),
        ],
        compiler_params=_params("parallel", "arbitrary"),
        name="attn",
    )(slopes, lq1, lk1, lq2, lk2, proj, proj, proj, proj, subln_g)


def _tail_kernel(u_ref, v_ref, z_ref, ga_ref, gb_ref, ybt_ref, x_ref, gate_ref, wa_ref, wb_ref,
                 wo_ref, lng_ref, lnb_ref, ws_ref, bs_ref, fg_ref, o_ref, ya_ref, m_ref, *,
                 chunk, gd):
    b = lax.dot_general(ybt_ref[...], wb_ref[...], (((0,), (0,)), ((), ())),
                        preferred_element_type=F32)
    _gmlp_tile(u_ref, v_ref, z_ref, lng_ref, lnb_ref, ws_ref, bs_ref, ya_ref, chunk, gd)
    a = jnp.dot(ya_ref[...], wa_ref[...], preferred_element_type=F32)
    for f in range(ga_ref.shape[0]):
        cols = slice(f * ST, (f + 1) * ST)
        ga = ga_ref[f].astype(F32)
        gb = gb_ref[f].astype(F32)
        m_ref[:, cols] = (ga * a[:, cols] + gb * b[:, cols]).astype(m_ref.dtype)
    out = jnp.dot(m_ref[...], wo_ref[...], preferred_element_type=F32)
    r = x_ref[...] + gate_ref[...] * out
    y = r * lax.rsqrt(jnp.mean(r * r, axis=-1, keepdims=True) + EPS)
    o_ref[...] = y * fg_ref[...]


def _tail(proj, ybt, x2, mod3, wa, wb, wo, ln_g, ln_b, w_s, bias_full, fgain, seq, tm=256):
    _, n_tt, n_ft, _, _ = proj.shape
    m, d = x2.shape
    chunk = w_s.shape[-1]
    per = ST // tm
    tiles_per_seq = seq // tm
    pspec = lambda s: pl.BlockSpec((None, None, n_ft, tm, ST),
                                   lambda i, s=s: (s, i // per, 0, i % per, 0))
    rows = pl.BlockSpec((tm, d), lambda i: (i, 0))
    vec = pl.BlockSpec((1, d), lambda i: (0, 0))
    weight = pl.BlockSpec((d, d), lambda i: (0, 0), pipeline_mode=pl.Buffered(1))
    return pl.pallas_call(
        functools.partial(_tail_kernel, chunk=chunk, gd=d // N_GROUPS),
        grid=(m // tm,),
        in_specs=[
            pspec(SLAB_U), pspec(SLAB_V), pspec(SLAB_Z), pspec(SLAB_GA), pspec(SLAB_GB),
            pl.BlockSpec((d, tm), lambda i: (0, i)), rows,
            pl.BlockSpec((None, 1, d), lambda i: (i // tiles_per_seq, 0, 2)),
            weight, weight, weight,
            vec, vec,
            pl.BlockSpec((N_GROUPS, chunk, chunk), lambda i: (0, 0, 0)),
            pl.BlockSpec((chunk, d), lambda i: (0, 0)),
            vec,
        ],
        out_specs=rows,
        out_shape=jax.ShapeDtypeStruct((m, d), F32),
        scratch_shapes=[pltpu.VMEM((tm, d), BF16), pltpu.VMEM((tm, d), BF16)],
        compiler_params=_params("parallel"),
        name="tail",
    )(proj, proj, proj, proj, proj, ybt, x2, mod3, wa, wb, wo, ln_g, ln_b, w_s, bias_full, fgain)


def kernel(x, c, w_ada, b_ada, norm_gain, w_in, ln_v_gain, ln_v_bias, w_spatial, b_spatial,
           lambda_q1, lambda_k1, lambda_q2, lambda_k2, subln_gain, w_branch_a, w_branch_b,
           w_out, final_norm_gain):
    bsz, seq, d = x.shape
    depth = w_ada.shape[0]
    assert depth == 1, "single-layer trunk"
    dh = lambda_q1.shape[-1]
    assert d == N_HEADS * 2 * dh and w_in.shape[-1] == N_SLABS * d
    assert seq % ST == 0 and d % ST == 0 and ST % (2 * dh) == 0
    l = 0

    x2 = x.reshape(bsz * seq, d)
    mod = _ada(c, w_ada[l], b_ada[l])
    mod3 = mod.reshape(bsz, 1, 3 * d)

    proj = _inproj(x2, mod3, norm_gain[l].reshape(1, d), w_in[l].astype(BF16),
                   LOG2E / math.sqrt(dh), seq)

    bias_full = jnp.repeat(jnp.transpose(b_spatial[l]), d // N_GROUPS, axis=1)

    slopes = jnp.asarray([2.0 ** (-8.0 * (i + 1) / N_HEADS) for i in range(N_HEADS)], F32)
    ybt = _attn(proj, slopes, lambda_q1[l].reshape(1, dh), lambda_k1[l].reshape(1, dh),
                lambda_q2[l].reshape(1, dh), lambda_k2[l].reshape(1, dh),
                subln_gain[l].reshape(2 * dh, 1), bsz, seq)

    out = _tail(proj, ybt, x2, mod3, w_branch_a[l].astype(BF16), w_branch_b[l].astype(BF16),
                w_out[l].astype(BF16), ln_v_gain[l].reshape(1, d), ln_v_bias[l].reshape(1, d),
                w_spatial[l], bias_full, final_norm_gain.reshape(1, d), seq)
    return out.reshape(bsz, seq, d)
```

```python
import functools
import math

import jax
import jax.numpy as jnp
from jax import lax
from jax.experimental import pallas as pl
from jax.experimental.pallas import tpu as pltpu

F32 = jnp.float32
BF16 = jnp.bfloat16

N_HEADS = 8
N_GROUPS = 8
N_SLABS = 9
SLAB_U, SLAB_V, SLAB_Z, SLAB_Q, SLAB_K, SLAB_BV, SLAB_BZ, SLAB_GA, SLAB_GB = range(N_SLABS)
ST = 512
EPS = 1e-6
SUBLN_EPS = 1e-5
LAMBDA_INIT = 0.8 - 0.6 * math.exp(-0.3 * 0)
LOG2E = 1.0 / math.log(2.0)

VMEM_LIMIT = 56 * 1024 * 1024


def _sigmoid(x):
    return 1.0 / (1.0 + jnp.exp(-x))


def _silu(x):
    return x * _sigmoid(x)


def _gelu_tanh(x):
    c = math.sqrt(2.0 / math.pi)
    return x * (0.5 * (1.0 + jnp.tanh(c * (x + 0.044715 * (x * x * x)))))


def _params(*sem):
    return pltpu.CompilerParams(dimension_semantics=sem, vmem_limit_bytes=VMEM_LIMIT)


def _ada_kernel(c_ref, w_ref, b_ref, o_ref):
    c = c_ref[...]
    o_ref[...] = jnp.dot(_silu(c), w_ref[...], preferred_element_type=F32) + b_ref[...]


def _ada(c, w_ada, b_ada, tn=1024):
    bsz, d = c.shape
    n = w_ada.shape[1]
    return pl.pallas_call(
        _ada_kernel,
        grid=(n // tn,),
        in_specs=[
            pl.BlockSpec((bsz, d), lambda j: (0, 0)),
            pl.BlockSpec((d, tn), lambda j: (0, j)),
            pl.BlockSpec((1, tn), lambda j: (0, j)),
        ],
        out_specs=pl.BlockSpec((bsz, tn), lambda j: (0, j)),
        out_shape=jax.ShapeDtypeStruct((bsz, n), F32),
        compiler_params=_params("parallel"),
        name="ada",
    )(c, w_ada, b_ada.reshape(1, n))


def _inproj_kernel(x_ref, g_ref, shift_ref, scale_ref, w_ref, o_ref, h_ref, *,
                   rows, per, q_scale):
    j = pl.program_id(1)

    @pl.when(j == 0)
    def _():
        g = g_ref[...]
        sc = 1.0 + scale_ref[...]
        sh = shift_ref[...]

        def body(r, carry):
            sl = pl.ds(pl.multiple_of(r * rows, rows), rows)
            x = x_ref[sl, :]
            y = x * lax.rsqrt(jnp.mean(x * x, axis=-1, keepdims=True) + EPS)
            h_ref[sl, :] = ((y * g) * sc + sh).astype(BF16)
            return carry

        lax.fori_loop(0, x_ref.shape[0] // rows, body, 0)

    n_tok, n_feat = o_ref.shape[0], o_ref.shape[1]
    slab = j // per
    epilogues = (
        ((SLAB_U, SLAB_V), _gelu_tanh),
        ((SLAB_Z,), _silu),
        ((SLAB_GA, SLAB_GB), _sigmoid),
        ((SLAB_Q,), lambda acc: acc * q_scale),
        ((SLAB_K,), lambda acc: acc),
    )
    for slabs, act in epilogues:
        @pl.when(functools.reduce(jnp.logical_or, [slab == s for s in slabs]))
        def _(act=act):
            acc = act(jnp.dot(h_ref[...], w_ref[...].astype(BF16), preferred_element_type=F32))
            for a in range(n_tok):
                for b in range(n_feat):
                    o_ref[a, b] = acc[a * ST:(a + 1) * ST, b * ST:(b + 1) * ST].astype(o_ref.dtype)

    for s_t, act in ((SLAB_BV, lambda acc: acc), (SLAB_BZ, _silu)):
        @pl.when(slab == s_t)
        def _(act=act):
            acc_t = act(lax.dot_general(w_ref[...].astype(BF16), h_ref[...], (((0,), (1,)), ((), ())),
                                        preferred_element_type=F32))
            for a in range(n_tok):
                for b in range(n_feat):
                    o_ref[a, b] = acc_t[b * ST:(b + 1) * ST, a * ST:(a + 1) * ST].astype(o_ref.dtype)


def _inproj(x2, mod3, gain, w_bf, q_scale, seq, tm=1024, tn=1024):
    m, d = x2.shape
    n = w_bf.shape[1]
    slab = n // N_SLABS
    per = slab // tn
    tiles_per_seq = seq // tm
    return pl.pallas_call(
        functools.partial(_inproj_kernel, rows=128, per=per, q_scale=q_scale),
        grid=(m // tm, n // tn),
        in_specs=[
            pl.BlockSpec((tm, d), lambda i, j: (i, 0)),
            pl.BlockSpec((1, d), lambda i, j: (0, 0)),
            pl.BlockSpec((None, 1, d), lambda i, j: (i // tiles_per_seq, 0, 0)),
            pl.BlockSpec((None, 1, d), lambda i, j: (i // tiles_per_seq, 0, 1)),
            pl.BlockSpec((d, tn), lambda i, j: (0, j)),
        ],
        out_specs=pl.BlockSpec((None, tm // ST, tn // ST, ST, ST),
                               lambda i, j: (j // per, i, j % per, 0, 0)),
        out_shape=jax.ShapeDtypeStruct((N_SLABS, m // ST, slab // ST, ST, ST), BF16),
        scratch_shapes=[pltpu.VMEM((tm, d), BF16)],
        compiler_params=_params("parallel", "arbitrary"),
        name="inproj",
    )(x2, gain, mod3, mod3, w_bf)


def _gmlp_tile(u_ref, v_ref, z_ref, lng_ref, lnb_ref, ws_ref, bs_ref, o_ref, chunk, gd):
    n_ft, ta, _ = u_ref.shape
    d = n_ft * ST
    row = lax.broadcasted_iota(jnp.int32, (chunk, chunk), 0)
    col = lax.broadcasted_iota(jnp.int32, (chunk, chunk), 1)
    causal = (row >= col).astype(F32)
    for c in range(ta // chunk):
        rows = pl.ds(c * chunk, chunk)
        gv = [v_ref[f, rows, :].astype(F32) for f in range(n_ft)]
        mu = sum(jnp.sum(t, axis=-1, keepdims=True) for t in gv) * (1.0 / d)
        xc = [t - mu for t in gv]
        var = sum(jnp.sum(t * t, axis=-1, keepdims=True) for t in xc) * (1.0 / d)
        rstd = lax.rsqrt(var + EPS)
        vn = [((xc[f] * rstd) * lng_ref[:, f * ST:(f + 1) * ST] + lnb_ref[:, f * ST:(f + 1) * ST]
               ).astype(BF16) for f in range(n_ft)]
        for g in range(N_GROUPS):
            f, off = divmod(g * gd, ST)
            ws = (ws_ref[g] * causal).astype(BF16)
            sv = jnp.dot(ws, vn[f][:, off:off + gd], preferred_element_type=F32)
            sv = sv + bs_ref[:, g * gd:(g + 1) * gd]
            u = u_ref[f, rows, off:off + gd].astype(F32)
            z = z_ref[f, rows, off:off + gd].astype(F32)
            o_ref[rows, g * gd:(g + 1) * gd] = ((u * sv) * z).astype(o_ref.dtype)


def _attn_kernel(slopes_ref, lq1_ref, lk1_ref, lq2_ref, lk2_ref, q_ref, k_ref, vt_ref, zt_ref,
                 sg_ref, o_ref, bias_ref, vta_ref, s_ref, smax_ref, acc_ref, *, dh):
    nq, blk, dv = q_ref.shape
    slope = slopes_ref[pl.program_id(1)] * LOG2E

    s_pos = lax.broadcasted_iota(jnp.int32, (blk, blk), 0)
    t_pos = lax.broadcasted_iota(jnp.int32, (blk, blk), 1)
    bias = slope * (s_pos - t_pos).astype(F32)
    bias_ref[0] = bias
    bias_ref[1] = jnp.where(s_pos <= t_pos, bias, -jnp.inf)

    for j in range(nq):
        vta_ref[j, :dv, :] = vt_ref[j]
        vta_ref[j, dv:, :] = jnp.ones((vta_ref.shape[1] - dv, blk), vta_ref.dtype)

    lam_init = jnp.float32(LAMBDA_INIT)
    lam = (jnp.exp(jnp.sum(lq1_ref[...] * lk1_ref[...], axis=-1, keepdims=True))
           - jnp.exp(jnp.sum(lq2_ref[...] * lk2_ref[...], axis=-1, keepdims=True))
           + lam_init)
    gain = sg_ref[...] * (1.0 - lam_init)

    def scores(j, n):
        slot = j % 2
        bias = bias_ref[int(j == n)]
        for c in range(2):
            q = q_ref[n, :, c * dh:(c + 1) * dh]
            k = k_ref[j, :, c * dh:(c + 1) * dh]
            s = lax.dot_general(k, q, (((1,), (1,)), ((), ())), preferred_element_type=F32)
            s = s + bias
            s_ref[slot, c] = s
            smax_ref[slot, c] = jnp.max(s, axis=0, keepdims=True)

    def consume(j, n, m):
        slot = j % 2
        vta = vta_ref[j]
        off = slope * float((j - n) * blk)
        for c in range(2):
            m_new = smax_ref[slot, c] + off
            if m[c] is not None:
                m_new = jnp.maximum(m[c], m_new)
            p = jnp.exp2(s_ref[slot, c] - (m_new - off)).astype(vta.dtype)
            pv = jnp.dot(vta, p, preferred_element_type=F32)
            if m[c] is None:
                acc_ref[c] = pv
            else:
                acc_ref[c] = jnp.exp2(m[c] - m_new) * acc_ref[c] + pv
            m[c] = m_new

    for n in range(nq):
        m = [None, None]
        scores(0, n)
        for j in range(n):
            scores(j + 1, n)
            consume(j, n, m)
        consume(n, n, m)

        a0, a1 = acc_ref[0], acc_ref[1]
        r0 = 1.0 / a0[dv:dv + 1]
        r1 = lam * (1.0 / a1[dv:dv + 1])
        o = a0[:dv] * r0 - a1[:dv] * r1
        o = o * lax.rsqrt(jnp.mean(o * o, axis=0, keepdims=True) + SUBLN_EPS)
        o = (o * gain) * zt_ref[n].astype(F32)
        o_ref[:, n * blk:(n + 1) * blk] = o.astype(o_ref.dtype)


def _attn(proj, slopes, lq1, lk1, lq2, lk2, subln_g, bsz, seq):
    _, n_tt, n_ft, _, _ = proj.shape
    m, d = n_tt * ST, n_ft * ST
    dv = d // N_HEADS
    dh = dv // 2
    nq = seq // ST
    hp = ST // dv
    ones_rows = 16
    tok_major = lambda s: pl.BlockSpec((None, nq, None, ST, dv),
                                       lambda b, h, s=s: (s, b, h // hp, 0, h % hp))
    feat_major = lambda s: pl.BlockSpec((None, nq, None, dv, ST),
                                        lambda b, h, s=s: (s, b, h // hp, h % hp, 0))
    vec = pl.BlockSpec((1, dh), lambda b, h: (0, 0))
    return pl.pallas_call(
        functools.partial(_attn_kernel, dh=dh),
        grid=(bsz, N_HEADS),
        in_specs=[
            pl.BlockSpec(memory_space=pltpu.SMEM),
            vec, vec, vec, vec,
            tok_major(SLAB_Q), tok_major(SLAB_K), feat_major(SLAB_BV), feat_major(SLAB_BZ),
            pl.BlockSpec((dv, 1), lambda b, h: (0, 0)),
        ],
        out_specs=pl.BlockSpec((dv, seq), lambda b, h: (h, b)),
        out_shape=jax.ShapeDtypeStruct((d, m), BF16),
        scratch_shapes=[
            pltpu.VMEM((2, ST, ST), F32),
            pltpu.VMEM((nq, dv + ones_rows, ST), BF16),
            pltpu.VMEM((2, 2, ST, ST), F32),
            pltpu.VMEM((2, 2, 1, ST), F32),
            pltpu.VMEM((2, dv + ones_rows, ST), F32),
        ],
        compiler_params=_params("parallel", "arbitrary"),
        name="attn",
    )(slopes, lq1, lk1, lq2, lk2, proj, proj, proj, proj, subln_g)


def _tail_kernel(u_ref, v_ref, z_ref, ga_ref, gb_ref, ybt_ref, x_ref, gate_ref, wa_ref, wb_ref,
                 wo_ref, lng_ref, lnb_ref, ws_ref, bs_ref, fg_ref, o_ref, ya_ref, m_ref, *,
                 chunk, gd):
    b = lax.dot_general(ybt_ref[...], wb_ref[...], (((0,), (0,)), ((), ())),
                        preferred_element_type=F32)
    _gmlp_tile(u_ref, v_ref, z_ref, lng_ref, lnb_ref, ws_ref, bs_ref, ya_ref, chunk, gd)
    a = jnp.dot(ya_ref[...], wa_ref[...], preferred_element_type=F32)
    for f in range(ga_ref.shape[0]):
        cols = slice(f * ST, (f + 1) * ST)
        ga = ga_ref[f].astype(F32)
        gb = gb_ref[f].astype(F32)
        m_ref[:, cols] = (ga * a[:, cols] + gb * b[:, cols]).astype(m_ref.dtype)
    out = jnp.dot(m_ref[...], wo_ref[...], preferred_element_type=F32)
    r = x_ref[...] + gate_ref[...] * out
    y = r * lax.rsqrt(jnp.mean(r * r, axis=-1, keepdims=True) + EPS)
    o_ref[...] = y * fg_ref[...]


def _tail(proj, ybt, x2, mod3, wa, wb, wo, ln_g, ln_b, w_s, bias_full, fgain, seq, tm=256):
    _, n_tt, n_ft, _, _ = proj.shape
    m, d = x2.shape
    chunk = w_s.shape[-1]
    per = ST // tm
    tiles_per_seq = seq // tm
    pspec = lambda s: pl.BlockSpec((None, None, n_ft, tm, ST),
                                   lambda i, s=s: (s, i // per, 0, i % per, 0))
    rows = pl.BlockSpec((tm, d), lambda i: (i, 0))
    vec = pl.BlockSpec((1, d), lambda i: (0, 0))
    weight = pl.BlockSpec((d, d), lambda i: (0, 0), pipeline_mode=pl.Buffered(1))
    return pl.pallas_call(
        functools.partial(_tail_kernel, chunk=chunk, gd=d // N_GROUPS),
        grid=(m // tm,),
        in_specs=[
            pspec(SLAB_U), pspec(SLAB_V), pspec(SLAB_Z), pspec(SLAB_GA), pspec(SLAB_GB),
            pl.BlockSpec((d, tm), lambda i: (0, i)), rows,
            pl.BlockSpec((None, 1, d), lambda i: (i // tiles_per_seq, 0, 2)),
            weight, weight, weight,
            vec, vec,
            pl.BlockSpec((N_GROUPS, chunk, chunk), lambda i: (0, 0, 0)),
            pl.BlockSpec((chunk, d), lambda i: (0, 0)),
            vec,
        ],
        out_specs=rows,
        out_shape=jax.ShapeDtypeStruct((m, d), F32),
        scratch_shapes=[pltpu.VMEM((tm, d), BF16), pltpu.VMEM((tm, d), BF16)],
        compiler_params=_params("parallel"),
        name="tail",
    )(proj, proj, proj, proj, proj, ybt, x2, mod3, wa, wb, wo, ln_g, ln_b, w_s, bias_full, fgain)


def kernel(x, c, w_ada, b_ada, norm_gain, w_in, ln_v_gain, ln_v_bias, w_spatial, b_spatial,
           lambda_q1, lambda_k1, lambda_q2, lambda_k2, subln_gain, w_branch_a, w_branch_b,
           w_out, final_norm_gain):
    bsz, seq, d = x.shape
    depth = w_ada.shape[0]
    assert depth == 1, "single-layer trunk"
    dh = lambda_q1.shape[-1]
    assert d == N_HEADS * 2 * dh and w_in.shape[-1] == N_SLABS * d
    assert seq % ST == 0 and d % ST == 0 and ST % (2 * dh) == 0
    l = 0

    x2 = x.reshape(bsz * seq, d)
    mod = _ada(c, w_ada[l], b_ada[l])
    mod3 = mod.reshape(bsz, 1, 3 * d)

    proj = _inproj(x2, mod3, norm_gain[l].reshape(1, d), w_in[l],
                   LOG2E / math.sqrt(dh), seq)

    bias_full = jnp.repeat(jnp.transpose(b_spatial[l]), d // N_GROUPS, axis=1)

    slopes = jnp.asarray([2.0 ** (-8.0 * (i + 1) / N_HEADS) for i in range(N_HEADS)], F32)
    ybt = _attn(proj, slopes, lambda_q1[l].reshape(1, dh), lambda_k1[l].reshape(1, dh),
                lambda_q2[l].reshape(1, dh), lambda_k2[l].reshape(1, dh),
                subln_gain[l].reshape(2 * dh, 1), bsz, seq)

    out = _tail(proj, ybt, x2, mod3, w_branch_a[l].astype(BF16), w_branch_b[l].astype(BF16),
                w_out[l].astype(BF16), ln_v_gain[l].reshape(1, d), ln_v_bias[l].reshape(1, d),
                w_spatial[l], bias_full, final_norm_gain.reshape(1, d), seq)
    return out.reshape(bsz, seq, d)
```

```python
import functools
import math

import jax
import jax.numpy as jnp
from jax import lax
from jax.experimental import pallas as pl
from jax.experimental.pallas import tpu as pltpu

F32 = jnp.float32
BF16 = jnp.bfloat16

N_HEADS = 8
N_GROUPS = 8
N_SLABS = 9
SLAB_U, SLAB_V, SLAB_Z, SLAB_Q, SLAB_K, SLAB_BV, SLAB_BZ, SLAB_GA, SLAB_GB = range(N_SLABS)
ST = 512
EPS = 1e-6
SUBLN_EPS = 1e-5
LAMBDA_INIT = 0.8 - 0.6 * math.exp(-0.3 * 0)
LOG2E = 1.0 / math.log(2.0)

VMEM_LIMIT = 56 * 1024 * 1024


def _sigmoid(x):
    return 0.5 * jnp.tanh(0.5 * x) + 0.5


def _silu(x):
    hx = 0.5 * x
    return hx * jnp.tanh(hx) + hx


def _gelu_tanh(x):
    c = math.sqrt(2.0 / math.pi)
    hx = 0.5 * x
    return hx * jnp.tanh(x * ((0.044715 * c) * (x * x) + c)) + hx


def _params(*sem):
    return pltpu.CompilerParams(dimension_semantics=sem, vmem_limit_bytes=VMEM_LIMIT)


def _ada_kernel(c_ref, w_ref, b_ref, o_ref):
    c = c_ref[...]
    o_ref[...] = jnp.dot(_silu(c), w_ref[...], preferred_element_type=F32) + b_ref[...]


def _ada(c, w_ada, b_ada, tn=1024):
    bsz, d = c.shape
    n = w_ada.shape[1]
    return pl.pallas_call(
        _ada_kernel,
        grid=(n // tn,),
        in_specs=[
            pl.BlockSpec((bsz, d), lambda j: (0, 0)),
            pl.BlockSpec((d, tn), lambda j: (0, j)),
            pl.BlockSpec((1, tn), lambda j: (0, j)),
        ],
        out_specs=pl.BlockSpec((bsz, tn), lambda j: (0, j)),
        out_shape=jax.ShapeDtypeStruct((bsz, n), F32),
        compiler_params=_params("parallel"),
        name="ada",
    )(c, w_ada, b_ada.reshape(1, n))


def _inproj_kernel(x_ref, g_ref, shift_ref, scale_ref, w_ref, o_ref, h_ref, *,
                   rows, per, q_scale):
    j = pl.program_id(1)

    @pl.when(j == 0)
    def _():
        gsc = g_ref[...] * (1.0 + scale_ref[...])
        sh = shift_ref[...]

        def body(r, carry):
            sl = pl.ds(pl.multiple_of(r * rows, rows), rows)
            x = x_ref[sl, :]
            rstd = lax.rsqrt(jnp.mean(x * x, axis=-1, keepdims=True) + EPS)
            h_ref[sl, :] = ((x * rstd) * gsc + sh).astype(BF16)
            return carry

        lax.fori_loop(0, x_ref.shape[0] // rows, body, 0, unroll=4)

    n_tok, n_feat = o_ref.shape[0], o_ref.shape[1]
    slab = j // per
    epilogues = (
        ((SLAB_U, SLAB_V), _gelu_tanh),
        ((SLAB_Z,), _silu),
        ((SLAB_GA, SLAB_GB), _sigmoid),
        ((SLAB_Q,), lambda acc: acc * q_scale),
        ((SLAB_K,), lambda acc: acc),
    )
    for slabs, act in epilogues:
        @pl.when(functools.reduce(jnp.logical_or, [slab == s for s in slabs]))
        def _(act=act):
            acc = act(jnp.dot(h_ref[...], w_ref[...].astype(BF16), preferred_element_type=F32))
            for a in range(n_tok):
                for b in range(n_feat):
                    o_ref[a, b] = acc[a * ST:(a + 1) * ST, b * ST:(b + 1) * ST].astype(o_ref.dtype)

    for s_t, act in ((SLAB_BV, lambda acc: acc), (SLAB_BZ, _silu)):
        @pl.when(slab == s_t)
        def _(act=act):
            acc_t = act(lax.dot_general(w_ref[...].astype(BF16), h_ref[...], (((0,), (1,)), ((), ())),
                                        preferred_element_type=F32))
            for a in range(n_tok):
                for b in range(n_feat):
                    o_ref[a, b] = acc_t[b * ST:(b + 1) * ST, a * ST:(a + 1) * ST].astype(o_ref.dtype)


def _inproj(x2, mod3, gain, w_bf, q_scale, seq, tm=1024, tn=1024):
    m, d = x2.shape
    n = w_bf.shape[1]
    slab = n // N_SLABS
    per = slab // tn
    tiles_per_seq = seq // tm
    return pl.pallas_call(
        functools.partial(_inproj_kernel, rows=32, per=per, q_scale=q_scale),
        grid=(m // tm, n // tn),
        in_specs=[
            pl.BlockSpec((tm, d), lambda i, j: (i, 0)),
            pl.BlockSpec((1, d), lambda i, j: (0, 0)),
            pl.BlockSpec((None, 1, d), lambda i, j: (i // tiles_per_seq, 0, 0)),
            pl.BlockSpec((None, 1, d), lambda i, j: (i // tiles_per_seq, 0, 1)),
            pl.BlockSpec((d, tn), lambda i, j: (0, j)),
        ],
        out_specs=pl.BlockSpec((None, tm // ST, tn // ST, ST, ST),
                               lambda i, j: (j // per, i, j % per, 0, 0)),
        out_shape=jax.ShapeDtypeStruct((N_SLABS, m // ST, slab // ST, ST, ST), BF16),
        scratch_shapes=[pltpu.VMEM((tm, d), BF16)],
        compiler_params=_params("parallel", "arbitrary"),
        name="inproj",
    )(x2, gain, mod3, mod3, w_bf)


def _gmlp_tile(u_ref, v_ref, z_ref, lng_ref, lnb_ref, ws_ref, bs_ref, o_ref, chunk, gd):
    n_ft, ta, _ = u_ref.shape
    d = n_ft * ST
    row = lax.broadcasted_iota(jnp.int32, (chunk, chunk), 0)
    col = lax.broadcasted_iota(jnp.int32, (chunk, chunk), 1)
    causal = (row >= col).astype(F32)
    for c in range(ta // chunk):
        rows = pl.ds(c * chunk, chunk)
        gv = [v_ref[f, rows, :].astype(F32) for f in range(n_ft)]
        mu = sum(jnp.sum(t, axis=-1, keepdims=True) for t in gv) * (1.0 / d)
        xc = [t - mu for t in gv]
        var = sum(jnp.sum(t * t, axis=-1, keepdims=True) for t in xc) * (1.0 / d)
        rstd = lax.rsqrt(var + EPS)
        vn = [((xc[f] * rstd) * lng_ref[:, f * ST:(f + 1) * ST] + lnb_ref[:, f * ST:(f + 1) * ST]
               ).astype(BF16) for f in range(n_ft)]
        for g in range(N_GROUPS):
            f, off = divmod(g * gd, ST)
            ws = (ws_ref[g] * causal).astype(BF16)
            sv = jnp.dot(ws, vn[f][:, off:off + gd], preferred_element_type=F32)
            sv = sv + bs_ref[:, g * gd:(g + 1) * gd]
            u = u_ref[f, rows, off:off + gd].astype(F32)
            z = z_ref[f, rows, off:off + gd].astype(F32)
            o_ref[rows, g * gd:(g + 1) * gd] = ((u * sv) * z).astype(o_ref.dtype)


def _attn_kernel(slopes_ref, lq1_ref, lk1_ref, lq2_ref, lk2_ref, q_ref, k_ref, vt_ref, zt_ref,
                 sg_ref, o_ref, bias_ref, vta_ref, s_ref, smax_ref, acc_ref, *, dh):
    nq, blk, dv = q_ref.shape
    slope = slopes_ref[pl.program_id(1)] * LOG2E

    s_pos = lax.broadcasted_iota(jnp.int32, (blk, blk), 0)
    t_pos = lax.broadcasted_iota(jnp.int32, (blk, blk), 1)
    bias = slope * (s_pos - t_pos).astype(F32)
    bias_ref[0] = bias
    bias_ref[1] = jnp.where(s_pos <= t_pos, bias, -jnp.inf)

    for j in range(nq):
        vta_ref[j, :dv, :] = vt_ref[j]
        vta_ref[j, dv:, :] = jnp.ones((vta_ref.shape[1] - dv, blk), vta_ref.dtype)

    lam_init = jnp.float32(LAMBDA_INIT)
    lam = (jnp.exp(jnp.sum(lq1_ref[...] * lk1_ref[...], axis=-1, keepdims=True))
           - jnp.exp(jnp.sum(lq2_ref[...] * lk2_ref[...], axis=-1, keepdims=True))
           + lam_init)
    gain = sg_ref[...] * (1.0 - lam_init)

    def scores(j, n):
        slot = j % 2
        bias = bias_ref[int(j == n)]
        for c in range(2):
            q = q_ref[n, :, c * dh:(c + 1) * dh]
            k = k_ref[j, :, c * dh:(c + 1) * dh]
            s = lax.dot_general(k, q, (((1,), (1,)), ((), ())), preferred_element_type=F32)
            s = s + bias
            s_ref[slot, c] = s
            smax_ref[slot, c] = jnp.max(s, axis=0, keepdims=True)

    def consume(j, n, m):
        slot = j % 2
        vta = vta_ref[j]
        off = slope * float((j - n) * blk)
        for c in range(2):
            m_new = smax_ref[slot, c] + off
            if m[c] is not None:
                m_new = jnp.maximum(m[c], m_new)
            p = jnp.exp2(s_ref[slot, c] - (m_new - off)).astype(vta.dtype)
            pv = jnp.dot(vta, p, preferred_element_type=F32)
            if m[c] is None:
                acc_ref[c] = pv
            else:
                acc_ref[c] = jnp.exp2(m[c] - m_new) * acc_ref[c] + pv
            m[c] = m_new

    for n in range(nq):
        m = [None, None]
        scores(0, n)
        for j in range(n):
            scores(j + 1, n)
            consume(j, n, m)
        consume(n, n, m)

        a0, a1 = acc_ref[0], acc_ref[1]
        r0 = 1.0 / a0[dv:dv + 1]
        r1 = lam * (1.0 / a1[dv:dv + 1])
        o = a0[:dv] * r0 - a1[:dv] * r1
        o = o * lax.rsqrt(jnp.mean(o * o, axis=0, keepdims=True) + SUBLN_EPS)
        o = (o * gain) * zt_ref[n].astype(F32)
        o_ref[:, n * blk:(n + 1) * blk] = o.astype(o_ref.dtype)


def _attn(proj, slopes, lq1, lk1, lq2, lk2, subln_g, bsz, seq):
    _, n_tt, n_ft, _, _ = proj.shape
    m, d = n_tt * ST, n_ft * ST
    dv = d // N_HEADS
    dh = dv // 2
    nq = seq // ST
    hp = ST // dv
    ones_rows = 16
    tok_major = lambda s: pl.BlockSpec((None, nq, None, ST, dv),
                                       lambda b, h, s=s: (s, b, h // hp, 0, h % hp))
    feat_major = lambda s: pl.BlockSpec((None, nq, None, dv, ST),
                                        lambda b, h, s=s: (s, b, h // hp, h % hp, 0))
    vec = pl.BlockSpec((1, dh), lambda b, h: (0, 0))
    return pl.pallas_call(
        functools.partial(_attn_kernel, dh=dh),
        grid=(bsz, N_HEADS),
        in_specs=[
            pl.BlockSpec(memory_space=pltpu.SMEM),
            vec, vec, vec, vec,
            tok_major(SLAB_Q), tok_major(SLAB_K), feat_major(SLAB_BV), feat_major(SLAB_BZ),
            pl.BlockSpec((dv, 1), lambda b, h: (0, 0)),
        ],
        out_specs=pl.BlockSpec((dv, seq), lambda b, h: (h, b)),
        out_shape=jax.ShapeDtypeStruct((d, m), BF16),
        scratch_shapes=[
            pltpu.VMEM((2, ST, ST), F32),
            pltpu.VMEM((nq, dv + ones_rows, ST), BF16),
            pltpu.VMEM((2, 2, ST, ST), F32),
            pltpu.VMEM((2, 2, 1, ST), F32),
            pltpu.VMEM((2, dv + ones_rows, ST), F32),
        ],
        compiler_params=_params("parallel", "arbitrary"),
        name="attn",
    )(slopes, lq1, lk1, lq2, lk2, proj, proj, proj, proj, subln_g)


def _tail_kernel(u_ref, v_ref, z_ref, ga_ref, gb_ref, ybt_ref, x_ref, gate_ref, wa_ref, wb_ref,
                 wo_ref, lng_ref, lnb_ref, ws_ref, bs_ref, fg_ref, o_ref, ya_ref, m_ref, *,
                 chunk, gd):
    b = lax.dot_general(ybt_ref[...], wb_ref[...], (((0,), (0,)), ((), ())),
                        preferred_element_type=F32)
    _gmlp_tile(u_ref, v_ref, z_ref, lng_ref, lnb_ref, ws_ref, bs_ref, ya_ref, chunk, gd)
    a = jnp.dot(ya_ref[...], wa_ref[...], preferred_element_type=F32)
    for f in range(ga_ref.shape[0]):
        cols = slice(f * ST, (f + 1) * ST)
        ga = ga_ref[f].astype(F32)
        gb = gb_ref[f].astype(F32)
        m_ref[:, cols] = (ga * a[:, cols] + gb * b[:, cols]).astype(m_ref.dtype)
    out = jnp.dot(m_ref[...], wo_ref[...], preferred_element_type=F32)
    r = x_ref[...] + gate_ref[...] * out
    y = r * lax.rsqrt(jnp.mean(r * r, axis=-1, keepdims=True) + EPS)
    o_ref[...] = y * fg_ref[...]


def _tail(proj, ybt, x2, mod3, wa, wb, wo, ln_g, ln_b, w_s, bias_full, fgain, seq, tm=256):
    _, n_tt, n_ft, _, _ = proj.shape
    m, d = x2.shape
    chunk = w_s.shape[-1]
    per = ST // tm
    tiles_per_seq = seq // tm
    pspec = lambda s: pl.BlockSpec((None, None, n_ft, tm, ST),
                                   lambda i, s=s: (s, i // per, 0, i % per, 0))
    rows = pl.BlockSpec((tm, d), lambda i: (i, 0))
    vec = pl.BlockSpec((1, d), lambda i: (0, 0))
    weight = pl.BlockSpec((d, d), lambda i: (0, 0), pipeline_mode=pl.Buffered(1))
    return pl.pallas_call(
        functools.partial(_tail_kernel, chunk=chunk, gd=d // N_GROUPS),
        grid=(m // tm,),
        in_specs=[
            pspec(SLAB_U), pspec(SLAB_V), pspec(SLAB_Z), pspec(SLAB_GA), pspec(SLAB_GB),
            pl.BlockSpec((d, tm), lambda i: (0, i)), rows,
            pl.BlockSpec((None, 1, d), lambda i: (i // tiles_per_seq, 0, 2)),
            weight, weight, weight,
            vec, vec,
            pl.BlockSpec((N_GROUPS, chunk, chunk), lambda i: (0, 0, 0)),
            pl.BlockSpec((chunk, d), lambda i: (0, 0)),
            vec,
        ],
        out_specs=rows,
        out_shape=jax.ShapeDtypeStruct((m, d), F32),
        scratch_shapes=[pltpu.VMEM((tm, d), BF16), pltpu.VMEM((tm, d), BF16)],
        compiler_params=_params("parallel"),
        name="tail",
    )(proj, proj, proj, proj, proj, ybt, x2, mod3, wa, wb, wo, ln_g, ln_b, w_s, bias_full, fgain)


def kernel(x, c, w_ada, b_ada, norm_gain, w_in, ln_v_gain, ln_v_bias, w_spatial, b_spatial,
           lambda_q1, lambda_k1, lambda_q2, lambda_k2, subln_gain, w_branch_a, w_branch_b,
           w_out, final_norm_gain):
    bsz, seq, d = x.shape
    depth = w_ada.shape[0]
    assert depth == 1, "single-layer trunk"
    dh = lambda_q1.shape[-1]
    assert d == N_HEADS * 2 * dh and w_in.shape[-1] == N_SLABS * d
    assert seq % ST == 0 and d % ST == 0 and ST % (2 * dh) == 0
    l = 0

    x2 = x.reshape(bsz * seq, d)
    mod = _ada(c, w_ada[l], b_ada[l])
    mod3 = mod.reshape(bsz, 1, 3 * d)

    proj = _inproj(x2, mod3, norm_gain[l].reshape(1, d), w_in[l],
                   LOG2E / math.sqrt(dh), seq)

    bias_full = jnp.repeat(jnp.transpose(b_spatial[l]), d // N_GROUPS, axis=1)

    slopes = jnp.asarray([2.0 ** (-8.0 * (i + 1) / N_HEADS) for i in range(N_HEADS)], F32)
    ybt = _attn(proj, slopes, lambda_q1[l].reshape(1, dh), lambda_k1[l].reshape(1, dh),
                lambda_q2[l].reshape(1, dh), lambda_k2[l].reshape(1, dh),
                subln_gain[l].reshape(2 * dh, 1), bsz, seq)

    out = _tail(proj, ybt, x2, mod3, w_branch_a[l].astype(BF16), w_branch_b[l].astype(BF16),
                w_out[l].astype(BF16), ln_v_gain[l].reshape(1, d), ln_v_bias[l].reshape(1, d),
                w_spatial[l], bias_full, final_norm_gain.reshape(1, d), seq)
    return out.reshape(bsz, seq, d)
```

```python
import functools
import math

import jax
import jax.numpy as jnp
from jax import lax
from jax.experimental import pallas as pl
from jax.experimental.pallas import tpu as pltpu

F32 = jnp.float32
BF16 = jnp.bfloat16

N_HEADS = 8
N_GROUPS = 8
N_SLABS = 9
SLAB_U, SLAB_V, SLAB_Z, SLAB_Q, SLAB_K, SLAB_BV, SLAB_BZ, SLAB_GA, SLAB_GB = range(N_SLABS)
ST = 512
EPS = 1e-6
SUBLN_EPS = 1e-5
LAMBDA_INIT = 0.8 - 0.6 * math.exp(-0.3 * 0)
LOG2E = 1.0 / math.log(2.0)

VMEM_LIMIT = 56 * 1024 * 1024


def _sigmoid(x):
    return 0.5 * jnp.tanh(0.5 * x) + 0.5


def _silu(x):
    hx = 0.5 * x
    return hx * jnp.tanh(hx) + hx


def _gelu_tanh(x):
    c = math.sqrt(2.0 / math.pi)
    hx = 0.5 * x
    return hx * jnp.tanh(x * ((0.044715 * c) * (x * x) + c)) + hx


def _params(*sem):
    return pltpu.CompilerParams(dimension_semantics=sem, vmem_limit_bytes=VMEM_LIMIT)


def _ada_kernel(c_ref, w_ref, b_ref, o_ref):
    c = c_ref[...]
    o_ref[...] = jnp.dot(_silu(c), w_ref[...], preferred_element_type=F32) + b_ref[...]


def _ada(c, w_ada, b_ada, tn=1024):
    bsz, d = c.shape
    n = w_ada.shape[1]
    return pl.pallas_call(
        _ada_kernel,
        grid=(n // tn,),
        in_specs=[
            pl.BlockSpec((bsz, d), lambda j: (0, 0)),
            pl.BlockSpec((d, tn), lambda j: (0, j)),
            pl.BlockSpec((1, tn), lambda j: (0, j)),
        ],
        out_specs=pl.BlockSpec((bsz, tn), lambda j: (0, j)),
        out_shape=jax.ShapeDtypeStruct((bsz, n), F32),
        compiler_params=_params("parallel"),
        name="ada",
    )(c, w_ada, b_ada.reshape(1, n))


def _inproj_kernel(x_ref, g_ref, shift_ref, scale_ref, w_hbm, o_hbm, h_ref, wbuf, obuf, wsem, osem,
                   *, rows, per, q_scale, tn):
    i = pl.program_id(0)
    nj = w_hbm.shape[1] // tn
    n_tok, n_feat = obuf.shape[1], obuf.shape[2]

    def w_copy(j):
        return pltpu.make_async_copy(w_hbm.at[:, pl.ds(j * tn, tn)], wbuf.at[j % 2], wsem.at[j % 2])

    def o_copy(j):
        slab, jj = divmod(j, per)
        dst = o_hbm.at[slab, pl.ds(i * n_tok, n_tok), pl.ds(jj * n_feat, n_feat)]
        return pltpu.make_async_copy(obuf.at[j % 2], dst, osem.at[j % 2])

    w_copy(0).start()

    gsc = g_ref[...] * (1.0 + scale_ref[...])
    sh = shift_ref[...]

    def body(r, carry):
        sl = pl.ds(pl.multiple_of(r * rows, rows), rows)
        x = x_ref[sl, :]
        rstd = lax.rsqrt(jnp.mean(x * x, axis=-1, keepdims=True) + EPS)
        h_ref[sl, :] = ((x * rstd) * gsc + sh).astype(BF16)
        return carry

    lax.fori_loop(0, x_ref.shape[0] // rows, body, 0, unroll=4)

    w_copy(0).wait()
    if nj > 1:
        w_copy(1).start()

    act = {SLAB_U: _gelu_tanh, SLAB_V: _gelu_tanh, SLAB_Z: _silu, SLAB_BZ: _silu,
           SLAB_GA: _sigmoid, SLAB_GB: _sigmoid, SLAB_Q: lambda acc: acc * q_scale,
           SLAB_K: lambda acc: acc, SLAB_BV: lambda acc: acc}
    for j in range(nj):
        slot, slab = j % 2, j // per
        transposed = slab in (SLAB_BV, SLAB_BZ)
        w = wbuf[slot].astype(BF16)
        if transposed:
            acc = lax.dot_general(w, h_ref[...], (((0,), (1,)), ((), ())),
                                  preferred_element_type=F32)
        else:
            acc = jnp.dot(h_ref[...], w, preferred_element_type=F32)
        acc = act[slab](acc)
        if j + 1 < nj:
            w_copy(j + 1).wait()
        if j + 2 < nj:
            w_copy(j + 2).start()
        if j >= 1:
            o_copy(j - 1).start()
        if j >= 2:
            o_copy(j - 2).wait()
        for a in range(n_tok):
            for b in range(n_feat):
                blk = (acc[b * ST:(b + 1) * ST, a * ST:(a + 1) * ST] if transposed
                       else acc[a * ST:(a + 1) * ST, b * ST:(b + 1) * ST])
                obuf[slot, a, b] = blk.astype(obuf.dtype)
    o_copy(nj - 1).start()
    if nj >= 2:
        o_copy(nj - 2).wait()
    o_copy(nj - 1).wait()


def _inproj(x2, mod3, gain, w_in, q_scale, seq, tm=1024, tn=1024):
    m, d = x2.shape
    n = w_in.shape[1]
    slab = n // N_SLABS
    per = slab // tn
    tiles_per_seq = seq // tm
    return pl.pallas_call(
        functools.partial(_inproj_kernel, rows=32, per=per, q_scale=q_scale, tn=tn),
        grid=(m // tm,),
        in_specs=[
            pl.BlockSpec((tm, d), lambda i: (i, 0)),
            pl.BlockSpec((1, d), lambda i: (0, 0)),
            pl.BlockSpec((None, 1, d), lambda i: (i // tiles_per_seq, 0, 0)),
            pl.BlockSpec((None, 1, d), lambda i: (i // tiles_per_seq, 0, 1)),
            pl.BlockSpec(memory_space=pl.ANY),
        ],
        out_specs=pl.BlockSpec(memory_space=pl.ANY),
        out_shape=jax.ShapeDtypeStruct((N_SLABS, m // ST, slab // ST, ST, ST), BF16),
        scratch_shapes=[
            pltpu.VMEM((tm, d), BF16),
            pltpu.VMEM((2, d, tn), F32),
            pltpu.VMEM((2, tm // ST, tn // ST, ST, ST), BF16),
            pltpu.SemaphoreType.DMA((2,)),
            pltpu.SemaphoreType.DMA((2,)),
        ],
        compiler_params=_params("arbitrary"),
        name="inproj",
    )(x2, gain, mod3, mod3, w_in)


def _gmlp_tile(u_ref, v_ref, z_ref, lng_ref, lnb_ref, ws_ref, bs_ref, o_ref, chunk, gd):
    n_ft, ta, _ = u_ref.shape
    d = n_ft * ST
    row = lax.broadcasted_iota(jnp.int32, (chunk, chunk), 0)
    col = lax.broadcasted_iota(jnp.int32, (chunk, chunk), 1)
    causal = (row >= col).astype(F32)
    for c in range(ta // chunk):
        rows = pl.ds(c * chunk, chunk)
        gv = [v_ref[f, rows, :].astype(F32) for f in range(n_ft)]
        mu = sum(jnp.sum(t, axis=-1, keepdims=True) for t in gv) * (1.0 / d)
        xc = [t - mu for t in gv]
        var = sum(jnp.sum(t * t, axis=-1, keepdims=True) for t in xc) * (1.0 / d)
        rstd = lax.rsqrt(var + EPS)
        vn = [((xc[f] * rstd) * lng_ref[:, f * ST:(f + 1) * ST] + lnb_ref[:, f * ST:(f + 1) * ST]
               ).astype(BF16) for f in range(n_ft)]
        for g in range(N_GROUPS):
            f, off = divmod(g * gd, ST)
            ws = (ws_ref[g] * causal).astype(BF16)
            sv = jnp.dot(ws, vn[f][:, off:off + gd], preferred_element_type=F32)
            sv = sv + bs_ref[:, g * gd:(g + 1) * gd]
            u = u_ref[f, rows, off:off + gd].astype(F32)
            z = z_ref[f, rows, off:off + gd].astype(F32)
            o_ref[rows, g * gd:(g + 1) * gd] = ((u * sv) * z).astype(o_ref.dtype)


def _attn_kernel(slopes_ref, lq1_ref, lk1_ref, lq2_ref, lk2_ref, q_ref, k_ref, vt_ref, zt_ref,
                 sg_ref, o_ref, bias_ref, vta_ref, s_ref, smax_ref, acc_ref, *, dh):
    nq, blk, dv = q_ref.shape
    slope = slopes_ref[pl.program_id(1)] * LOG2E

    s_pos = lax.broadcasted_iota(jnp.int32, (blk, blk), 0)
    t_pos = lax.broadcasted_iota(jnp.int32, (blk, blk), 1)
    bias = slope * (s_pos - t_pos).astype(F32)
    bias_ref[0] = bias
    bias_ref[1] = jnp.where(s_pos <= t_pos, bias, -jnp.inf)

    for j in range(nq):
        vta_ref[j, :dv, :] = vt_ref[j]
        vta_ref[j, dv:, :] = jnp.ones((vta_ref.shape[1] - dv, blk), vta_ref.dtype)

    lam_init = jnp.float32(LAMBDA_INIT)
    lam = (jnp.exp(jnp.sum(lq1_ref[...] * lk1_ref[...], axis=-1, keepdims=True))
           - jnp.exp(jnp.sum(lq2_ref[...] * lk2_ref[...], axis=-1, keepdims=True))
           + lam_init)
    gain = sg_ref[...] * (1.0 - lam_init)

    def scores(j, n):
        slot = j % 2
        bias = bias_ref[int(j == n)]
        for c in range(2):
            q = q_ref[n, :, c * dh:(c + 1) * dh]
            k = k_ref[j, :, c * dh:(c + 1) * dh]
            s = lax.dot_general(k, q, (((1,), (1,)), ((), ())), preferred_element_type=F32)
            s = s + bias
            s_ref[slot, c] = s
            smax_ref[slot, c] = jnp.max(s, axis=0, keepdims=True)

    def consume(j, n, m):
        slot = j % 2
        vta = vta_ref[j]
        off = slope * float((j - n) * blk)
        for c in range(2):
            m_new = smax_ref[slot, c] + off
            if m[c] is not None:
                m_new = jnp.maximum(m[c], m_new)
            p = jnp.exp2(s_ref[slot, c] - (m_new - off)).astype(vta.dtype)
            pv = jnp.dot(vta, p, preferred_element_type=F32)
            if m[c] is None:
                acc_ref[c] = pv
            else:
                acc_ref[c] = jnp.exp2(m[c] - m_new) * acc_ref[c] + pv
            m[c] = m_new

    for n in range(nq):
        m = [None, None]
        scores(0, n)
        for j in range(n):
            scores(j + 1, n)
            consume(j, n, m)
        consume(n, n, m)

        a0, a1 = acc_ref[0], acc_ref[1]
        r0 = 1.0 / a0[dv:dv + 1]
        r1 = lam * (1.0 / a1[dv:dv + 1])
        o = a0[:dv] * r0 - a1[:dv] * r1
        o = o * lax.rsqrt(jnp.mean(o * o, axis=0, keepdims=True) + SUBLN_EPS)
        o = (o * gain) * zt_ref[n].astype(F32)
        o_ref[:, n * blk:(n + 1) * blk] = o.astype(o_ref.dtype)


def _attn(proj, slopes, lq1, lk1, lq2, lk2, subln_g, bsz, seq):
    _, n_tt, n_ft, _, _ = proj.shape
    m, d = n_tt * ST, n_ft * ST
    dv = d // N_HEADS
    dh = dv // 2
    nq = seq // ST
    hp = ST // dv
    ones_rows = 16
    tok_major = lambda s: pl.BlockSpec((None, nq, None, ST, dv),
                                       lambda b, h, s=s: (s, b, h // hp, 0, h % hp))
    feat_major = lambda s: pl.BlockSpec((None, nq, None, dv, ST),
                                        lambda b, h, s=s: (s, b, h // hp, h % hp, 0))
    vec = pl.BlockSpec((1, dh), lambda b, h: (0, 0))
    return pl.pallas_call(
        functools.partial(_attn_kernel, dh=dh),
        grid=(bsz, N_HEADS),
        in_specs=[
            pl.BlockSpec(memory_space=pltpu.SMEM),
            vec, vec, vec, vec,
            tok_major(SLAB_Q), tok_major(SLAB_K), feat_major(SLAB_BV), feat_major(SLAB_BZ),
            pl.BlockSpec((dv, 1), lambda b, h: (0, 0)),
        ],
        out_specs=pl.BlockSpec((dv, seq), lambda b, h: (h, b)),
        out_shape=jax.ShapeDtypeStruct((d, m), BF16),
        scratch_shapes=[
            pltpu.VMEM((2, ST, ST), F32),
            pltpu.VMEM((nq, dv + ones_rows, ST), BF16),
            pltpu.VMEM((2, 2, ST, ST), F32),
            pltpu.VMEM((2, 2, 1, ST), F32),
            pltpu.VMEM((2, dv + ones_rows, ST), F32),
        ],
        compiler_params=_params("parallel", "arbitrary"),
        name="attn",
    )(slopes, lq1, lk1, lq2, lk2, proj, proj, proj, proj, subln_g)


def _tail_kernel(u_ref, v_ref, z_ref, ga_ref, gb_ref, ybt_ref, x_ref, gate_ref, wa_ref, wb_ref,
                 wo_ref, lng_ref, lnb_ref, ws_ref, bs_ref, fg_ref, o_ref, ya_ref, m_ref, *,
                 chunk, gd):
    b = lax.dot_general(ybt_ref[...], wb_ref[...], (((0,), (0,)), ((), ())),
                        preferred_element_type=F32)
    _gmlp_tile(u_ref, v_ref, z_ref, lng_ref, lnb_ref, ws_ref, bs_ref, ya_ref, chunk, gd)
    a = jnp.dot(ya_ref[...], wa_ref[...], preferred_element_type=F32)
    for f in range(ga_ref.shape[0]):
        cols = slice(f * ST, (f + 1) * ST)
        ga = ga_ref[f].astype(F32)
        gb = gb_ref[f].astype(F32)
        m_ref[:, cols] = (ga * a[:, cols] + gb * b[:, cols]).astype(m_ref.dtype)
    out = jnp.dot(m_ref[...], wo_ref[...], preferred_element_type=F32)
    r = x_ref[...] + gate_ref[...] * out
    y = r * lax.rsqrt(jnp.mean(r * r, axis=-1, keepdims=True) + EPS)
    o_ref[...] = y * fg_ref[...]


def _tail(proj, ybt, x2, mod3, wa, wb, wo, ln_g, ln_b, w_s, bias_full, fgain, seq, tm=256):
    _, n_tt, n_ft, _, _ = proj.shape
    m, d = x2.shape
    chunk = w_s.shape[-1]
    per = ST // tm
    tiles_per_seq = seq // tm
    pspec = lambda s: pl.BlockSpec((None, None, n_ft, tm, ST),
                                   lambda i, s=s: (s, i // per, 0, i % per, 0))
    rows = pl.BlockSpec((tm, d), lambda i: (i, 0))
    vec = pl.BlockSpec((1, d), lambda i: (0, 0))
    weight = pl.BlockSpec((d, d), lambda i: (0, 0), pipeline_mode=pl.Buffered(1))
    return pl.pallas_call(
        functools.partial(_tail_kernel, chunk=chunk, gd=d // N_GROUPS),
        grid=(m // tm,),
        in_specs=[
            pspec(SLAB_U), pspec(SLAB_V), pspec(SLAB_Z), pspec(SLAB_GA), pspec(SLAB_GB),
            pl.BlockSpec((d, tm), lambda i: (0, i)), rows,
            pl.BlockSpec((None, 1, d), lambda i: (i // tiles_per_seq, 0, 2)),
            weight, weight, weight,
            vec, vec,
            pl.BlockSpec((N_GROUPS, chunk, chunk), lambda i: (0, 0, 0)),
            pl.BlockSpec((chunk, d), lambda i: (0, 0)),
            vec,
        ],
        out_specs=rows,
        out_shape=jax.ShapeDtypeStruct((m, d), F32),
        scratch_shapes=[pltpu.VMEM((tm, d), BF16), pltpu.VMEM((tm, d), BF16)],
        compiler_params=_params("parallel"),
        name="tail",
    )(proj, proj, proj, proj, proj, ybt, x2, mod3, wa, wb, wo, ln_g, ln_b, w_s, bias_full, fgain)


def kernel(x, c, w_ada, b_ada, norm_gain, w_in, ln_v_gain, ln_v_bias, w_spatial, b_spatial,
           lambda_q1, lambda_k1, lambda_q2, lambda_k2, subln_gain, w_branch_a, w_branch_b,
           w_out, final_norm_gain):
    bsz, seq, d = x.shape
    depth = w_ada.shape[0]
    assert depth == 1, "single-layer trunk"
    dh = lambda_q1.shape[-1]
    assert d == N_HEADS * 2 * dh and w_in.shape[-1] == N_SLABS * d
    assert seq % ST == 0 and d % ST == 0 and ST % (2 * dh) == 0
    l = 0

    x2 = x.reshape(bsz * seq, d)
    mod = _ada(c, w_ada[l], b_ada[l])
    mod3 = mod.reshape(bsz, 1, 3 * d)

    proj = _inproj(x2, mod3, norm_gain[l].reshape(1, d), w_in[l],
                   LOG2E / math.sqrt(dh), seq)

    bias_full = jnp.repeat(jnp.transpose(b_spatial[l]), d // N_GROUPS, axis=1)

    slopes = jnp.asarray([2.0 ** (-8.0 * (i + 1) / N_HEADS) for i in range(N_HEADS)], F32)
    ybt = _attn(proj, slopes, lambda_q1[l].reshape(1, dh), lambda_k1[l].reshape(1, dh),
                lambda_q2[l].reshape(1, dh), lambda_k2[l].reshape(1, dh),
                subln_gain[l].reshape(2 * dh, 1), bsz, seq)

    out = _tail(proj, ybt, x2, mod3, w_branch_a[l].astype(BF16), w_branch_b[l].astype(BF16),
                w_out[l].astype(BF16), ln_v_gain[l].reshape(1, d), ln_v_bias[l].reshape(1, d),
                w_spatial[l], bias_full, final_norm_gain.reshape(1, d), seq)
    return out.reshape(bsz, seq, d)
```

```python
import functools
import math

import jax
import jax.numpy as jnp
from jax import lax
from jax.experimental import pallas as pl
from jax.experimental.pallas import tpu as pltpu

F32 = jnp.float32
BF16 = jnp.bfloat16

N_HEADS = 8
N_GROUPS = 8
N_SLABS = 9
SLAB_U, SLAB_V, SLAB_Z, SLAB_Q, SLAB_K, SLAB_BV, SLAB_BZ, SLAB_GA, SLAB_GB = range(N_SLABS)
ST = 512
EPS = 1e-6
SUBLN_EPS = 1e-5
LAMBDA_INIT = 0.8 - 0.6 * math.exp(-0.3 * 0)
LOG2E = 1.0 / math.log(2.0)

VMEM_LIMIT = 56 * 1024 * 1024


def _sigmoid(x):
    return 0.5 * jnp.tanh(0.5 * x) + 0.5


def _silu(x):
    hx = 0.5 * x
    return hx * jnp.tanh(hx) + hx


def _gelu_tanh(x):
    c = math.sqrt(2.0 / math.pi)
    hx = 0.5 * x
    return hx * jnp.tanh(x * ((0.044715 * c) * (x * x) + c)) + hx


def _params(*sem):
    return pltpu.CompilerParams(dimension_semantics=sem, vmem_limit_bytes=VMEM_LIMIT)


def _ada_kernel(c_ref, w_ref, b_ref, o_ref):
    c = c_ref[...]
    o_ref[...] = jnp.dot(_silu(c), w_ref[...], preferred_element_type=F32) + b_ref[...]


def _ada(c, w_ada, b_ada, tn=1024):
    bsz, d = c.shape
    n = w_ada.shape[1]
    return pl.pallas_call(
        _ada_kernel,
        grid=(n // tn,),
        in_specs=[
            pl.BlockSpec((bsz, d), lambda j: (0, 0)),
            pl.BlockSpec((d, tn), lambda j: (0, j)),
            pl.BlockSpec((1, tn), lambda j: (0, j)),
        ],
        out_specs=pl.BlockSpec((bsz, tn), lambda j: (0, j)),
        out_shape=jax.ShapeDtypeStruct((bsz, n), F32),
        compiler_params=_params("parallel"),
        name="ada",
    )(c, w_ada, b_ada.reshape(1, n))


def _inproj_kernel(x_ref, g_ref, shift_ref, scale_ref, w_ref, o_ref, h_ref, *,
                   rows, per, q_scale):
    j = pl.program_id(1)

    @pl.when(j == 0)
    def _():
        gsc = g_ref[...] * (1.0 + scale_ref[...])
        sh = shift_ref[...]

        def body(r, carry):
            sl = pl.ds(pl.multiple_of(r * rows, rows), rows)
            x = x_ref[sl, :]
            rstd = lax.rsqrt(jnp.mean(x * x, axis=-1, keepdims=True) + EPS)
            h_ref[sl, :] = ((x * rstd) * gsc + sh).astype(BF16)
            return carry

        lax.fori_loop(0, x_ref.shape[0] // rows, body, 0, unroll=4)

    n_tok, n_feat = o_ref.shape[0], o_ref.shape[1]
    slab = j // per
    epilogues = (
        ((SLAB_U, SLAB_V), _gelu_tanh),
        ((SLAB_Z,), _silu),
        ((SLAB_GA, SLAB_GB), _sigmoid),
        ((SLAB_Q,), lambda acc: acc * q_scale),
        ((SLAB_K,), lambda acc: acc),
    )
    for slabs, act in epilogues:
        @pl.when(functools.reduce(jnp.logical_or, [slab == s for s in slabs]))
        def _(act=act):
            acc = act(jnp.dot(h_ref[...], w_ref[...].astype(BF16), preferred_element_type=F32))
            for a in range(n_tok):
                for b in range(n_feat):
                    o_ref[a, b] = acc[a * ST:(a + 1) * ST, b * ST:(b + 1) * ST].astype(o_ref.dtype)

    for s_t, act in ((SLAB_BV, lambda acc: acc), (SLAB_BZ, _silu)):
        @pl.when(slab == s_t)
        def _(act=act):
            acc_t = act(lax.dot_general(w_ref[...].astype(BF16), h_ref[...], (((0,), (1,)), ((), ())),
                                        preferred_element_type=F32))
            for a in range(n_tok):
                for b in range(n_feat):
                    o_ref[a, b] = acc_t[b * ST:(b + 1) * ST, a * ST:(a + 1) * ST].astype(o_ref.dtype)


def _inproj(x2, mod3, gain, w_bf, q_scale, seq, tm=1024, tn=1024):
    m, d = x2.shape
    n = w_bf.shape[1]
    slab = n // N_SLABS
    per = slab // tn
    tiles_per_seq = seq // tm
    return pl.pallas_call(
        functools.partial(_inproj_kernel, rows=32, per=per, q_scale=q_scale),
        grid=(m // tm, n // tn),
        in_specs=[
            pl.BlockSpec((tm, d), lambda i, j: (i, 0)),
            pl.BlockSpec((1, d), lambda i, j: (0, 0)),
            pl.BlockSpec((None, 1, d), lambda i, j: (i // tiles_per_seq, 0, 0)),
            pl.BlockSpec((None, 1, d), lambda i, j: (i // tiles_per_seq, 0, 1)),
            pl.BlockSpec((d, tn), lambda i, j: (0, j)),
        ],
        out_specs=pl.BlockSpec((None, tm // ST, tn // ST, ST, ST),
                               lambda i, j: (j // per, i, j % per, 0, 0)),
        out_shape=jax.ShapeDtypeStruct((N_SLABS, m // ST, slab // ST, ST, ST), BF16),
        scratch_shapes=[pltpu.VMEM((tm, d), BF16)],
        compiler_params=_params("parallel", "arbitrary"),
        name="inproj",
    )(x2, gain, mod3, mod3, w_bf)


def _gmlp_pieces(u_ref, v_ref, z_ref, lng_ref, lnb_ref, ws_ref, bs_ref, o_ref, chunk, gd):
    n_ft, ta, _ = u_ref.shape
    d = n_ft * ST
    state = {}

    def layernorm(c):
        rows = pl.ds(c * chunk, chunk)
        gv = [v_ref[f, rows, :].astype(F32) for f in range(n_ft)]
        mu = sum(jnp.sum(t, axis=-1, keepdims=True) for t in gv) * (1.0 / d)
        xc = [t - mu for t in gv]
        var = sum(jnp.sum(t * t, axis=-1, keepdims=True) for t in xc) * (1.0 / d)
        rstd = lax.rsqrt(var + EPS)
        state[c] = [((xc[f] * rstd) * lng_ref[:, f * ST:(f + 1) * ST]
                     + lnb_ref[:, f * ST:(f + 1) * ST]).astype(BF16) for f in range(n_ft)]

    def group(c, g):
        rows = pl.ds(c * chunk, chunk)
        row = lax.broadcasted_iota(jnp.int32, (chunk, chunk), 0)
        col = lax.broadcasted_iota(jnp.int32, (chunk, chunk), 1)
        f, off = divmod(g * gd, ST)
        ws = (ws_ref[g] * (row >= col).astype(F32)).astype(BF16)
        sv = jnp.dot(ws, state[c][f][:, off:off + gd], preferred_element_type=F32)
        sv = sv + bs_ref[:, g * gd:(g + 1) * gd]
        u = u_ref[f, rows, off:off + gd].astype(F32)
        z = z_ref[f, rows, off:off + gd].astype(F32)
        o_ref[rows, g * gd:(g + 1) * gd] = ((u * sv) * z).astype(o_ref.dtype)

    steps = []
    for c in range(ta // chunk):
        steps.append(functools.partial(layernorm, c))
        steps += [functools.partial(group, c, g) for g in range(N_GROUPS)]
    return steps


def _attn_kernel(slopes_ref, lq1_ref, lk1_ref, lq2_ref, lk2_ref, q_ref, k_ref, vt_ref, zt_ref,
                 sg_ref, o_ref, bias_ref, vta_ref, s_ref, smax_ref, acc_ref, *, dh):
    nq, blk, dv = q_ref.shape
    slope = slopes_ref[pl.program_id(0)] * LOG2E

    @pl.when(pl.program_id(1) == 0)
    def _():
        s_pos = lax.broadcasted_iota(jnp.int32, (blk, blk), 0)
        t_pos = lax.broadcasted_iota(jnp.int32, (blk, blk), 1)
        bias = slope * (s_pos - t_pos).astype(F32)
        bias_ref[0] = bias
        bias_ref[1] = jnp.where(s_pos <= t_pos, bias, -jnp.inf)

    for j in range(nq):
        vta_ref[j, :dv, :] = vt_ref[j]
        vta_ref[j, dv:, :] = jnp.ones((vta_ref.shape[1] - dv, blk), vta_ref.dtype)

    lam_init = jnp.float32(LAMBDA_INIT)
    lam = (jnp.exp(jnp.sum(lq1_ref[...] * lk1_ref[...], axis=-1, keepdims=True))
           - jnp.exp(jnp.sum(lq2_ref[...] * lk2_ref[...], axis=-1, keepdims=True))
           + lam_init)
    gain = sg_ref[...] * (1.0 - lam_init)

    def scores(j, n):
        slot = j % 2
        bias = bias_ref[int(j == n)]
        for c in range(2):
            q = q_ref[n, :, c * dh:(c + 1) * dh]
            k = k_ref[j, :, c * dh:(c + 1) * dh]
            s = lax.dot_general(k, q, (((1,), (1,)), ((), ())), preferred_element_type=F32)
            s = s + bias
            s_ref[slot, c] = s
            smax_ref[slot, c] = jnp.max(s, axis=0, keepdims=True)

    def consume(j, n, m):
        slot = j % 2
        vta = vta_ref[j]
        off = slope * float((j - n) * blk)
        for c in range(2):
            m_new = smax_ref[slot, c] + off
            if m[c] is not None:
                m_new = jnp.maximum(m[c], m_new)
            p = jnp.exp2(s_ref[slot, c] - (m_new - off)).astype(vta.dtype)
            pv = jnp.dot(vta, p, preferred_element_type=F32)
            if m[c] is None:
                acc_ref[c] = pv
            else:
                acc_ref[c] = jnp.exp2(m[c] - m_new) * acc_ref[c] + pv
            m[c] = m_new

    for n in range(nq):
        m = [None, None]
        scores(0, n)
        for j in range(n):
            scores(j + 1, n)
            consume(j, n, m)
        consume(n, n, m)

        a0, a1 = acc_ref[0], acc_ref[1]
        r0 = 1.0 / a0[dv:dv + 1]
        r1 = lam * (1.0 / a1[dv:dv + 1])
        o = a0[:dv] * r0 - a1[:dv] * r1
        o = o * lax.rsqrt(jnp.mean(o * o, axis=0, keepdims=True) + SUBLN_EPS)
        o = (o * gain) * zt_ref[n].astype(F32)
        o_ref[:, n * blk:(n + 1) * blk] = o.astype(o_ref.dtype)


def _attn(proj, slopes, lq1, lk1, lq2, lk2, subln_g, bsz, seq):
    _, n_tt, n_ft, _, _ = proj.shape
    m, d = n_tt * ST, n_ft * ST
    dv = d // N_HEADS
    dh = dv // 2
    nq = seq // ST
    hp = ST // dv
    ones_rows = 16
    tok_major = lambda s: pl.BlockSpec((None, nq, None, ST, dv),
                                       lambda h, b, s=s: (s, b, h // hp, 0, h % hp))
    feat_major = lambda s: pl.BlockSpec((None, nq, None, dv, ST),
                                        lambda h, b, s=s: (s, b, h // hp, h % hp, 0))
    vec = pl.BlockSpec((1, dh), lambda h, b: (0, 0))
    return pl.pallas_call(
        functools.partial(_attn_kernel, dh=dh),
        grid=(N_HEADS, bsz),
        in_specs=[
            pl.BlockSpec(memory_space=pltpu.SMEM),
            vec, vec, vec, vec,
            tok_major(SLAB_Q), tok_major(SLAB_K), feat_major(SLAB_BV), feat_major(SLAB_BZ),
            pl.BlockSpec((dv, 1), lambda h, b: (0, 0)),
        ],
        out_specs=pl.BlockSpec((dv, seq), lambda h, b: (h, b)),
        out_shape=jax.ShapeDtypeStruct((d, m), BF16),
        scratch_shapes=[
            pltpu.VMEM((2, ST, ST), F32),
            pltpu.VMEM((nq, dv + ones_rows, ST), BF16),
            pltpu.VMEM((2, 2, ST, ST), F32),
            pltpu.VMEM((2, 2, 1, ST), F32),
            pltpu.VMEM((2, dv + ones_rows, ST), F32),
        ],
        compiler_params=_params("arbitrary", "arbitrary"),
        name="attn",
    )(slopes, lq1, lk1, lq2, lk2, proj, proj, proj, proj, subln_g)


def _tail_kernel(u_ref, v_ref, z_ref, ga_ref, gb_ref, ybt_ref, x_ref, gate_ref, wa_ref, wb_ref,
                 wo_ref, lng_ref, lnb_ref, ws_ref, bs_ref, fg_ref, o_ref, ya_ref, b_ref, m_ref, *,
                 chunk, gd, nc):
    d = wb_ref.shape[1]
    cols = [slice(k, k + nc) for k in range(0, d, nc)]

    gm = _gmlp_pieces(u_ref, v_ref, z_ref, lng_ref, lnb_ref, ws_ref, bs_ref, ya_ref, chunk, gd)
    per = -(-len(gm) // len(cols))
    for k, cs in enumerate(cols):
        b_ref[:, cs] = lax.dot_general(ybt_ref[...], wb_ref[:, cs], (((0,), (0,)), ((), ())),
                                       preferred_element_type=F32)
        for step in gm[k * per:(k + 1) * per]:
            step()

    def merge(cs, a):
        f, off = divmod(cs.start, ST)
        ga = ga_ref[f, :, off:off + nc].astype(F32)
        gb = gb_ref[f, :, off:off + nc].astype(F32)
        m_ref[:, cs] = (ga * a + gb * b_ref[:, cs]).astype(m_ref.dtype)

    pending = None
    for cs in cols:
        a = jnp.dot(ya_ref[...], wa_ref[:, cs], preferred_element_type=F32)
        if pending is not None:
            merge(*pending)
        pending = (cs, a)
    merge(*pending)

    ssq = None
    for cs in cols:
        out = jnp.dot(m_ref[...], wo_ref[:, cs], preferred_element_type=F32)
        r = x_ref[:, cs] + gate_ref[:, cs] * out
        o_ref[:, cs] = r
        part = jnp.sum(r * r, axis=-1, keepdims=True)
        ssq = part if ssq is None else ssq + part
    rstd = lax.rsqrt(ssq * (1.0 / d) + EPS)
    o_ref[...] = (o_ref[...] * rstd) * fg_ref[...]


def _tail(proj, ybt, x2, mod3, wa, wb, wo, ln_g, ln_b, w_s, bias_full, fgain, seq, tm=256):
    _, n_tt, n_ft, _, _ = proj.shape
    m, d = x2.shape
    chunk = w_s.shape[-1]
    per = ST // tm
    tiles_per_seq = seq // tm
    pspec = lambda s: pl.BlockSpec((None, None, n_ft, tm, ST),
                                   lambda i, s=s: (s, i // per, 0, i % per, 0))
    rows = pl.BlockSpec((tm, d), lambda i: (i, 0))
    vec = pl.BlockSpec((1, d), lambda i: (0, 0))
    weight = pl.BlockSpec((d, d), lambda i: (0, 0), pipeline_mode=pl.Buffered(1))
    return pl.pallas_call(
        functools.partial(_tail_kernel, chunk=chunk, gd=d // N_GROUPS, nc=512),
        grid=(m // tm,),
        in_specs=[
            pspec(SLAB_U), pspec(SLAB_V), pspec(SLAB_Z), pspec(SLAB_GA), pspec(SLAB_GB),
            pl.BlockSpec((d, tm), lambda i: (0, i)), rows,
            pl.BlockSpec((None, 1, d), lambda i: (i // tiles_per_seq, 0, 2)),
            weight, weight, weight,
            vec, vec,
            pl.BlockSpec((N_GROUPS, chunk, chunk), lambda i: (0, 0, 0)),
            pl.BlockSpec((chunk, d), lambda i: (0, 0)),
            vec,
        ],
        out_specs=rows,
        out_shape=jax.ShapeDtypeStruct((m, d), F32),
        scratch_shapes=[pltpu.VMEM((tm, d), BF16), pltpu.VMEM((tm, d), F32),
                        pltpu.VMEM((tm, d), BF16)],
        compiler_params=_params("parallel"),
        name="tail",
    )(proj, proj, proj, proj, proj, ybt, x2, mod3, wa, wb, wo, ln_g, ln_b, w_s, bias_full, fgain)


def kernel(x, c, w_ada, b_ada, norm_gain, w_in, ln_v_gain, ln_v_bias, w_spatial, b_spatial,
           lambda_q1, lambda_k1, lambda_q2, lambda_k2, subln_gain, w_branch_a, w_branch_b,
           w_out, final_norm_gain):
    bsz, seq, d = x.shape
    depth = w_ada.shape[0]
    assert depth == 1, "single-layer trunk"
    dh = lambda_q1.shape[-1]
    assert d == N_HEADS * 2 * dh and w_in.shape[-1] == N_SLABS * d
    assert seq % ST == 0 and d % ST == 0 and ST % (2 * dh) == 0
    l = 0

    x2 = x.reshape(bsz * seq, d)
    mod = _ada(c, w_ada[l], b_ada[l])
    mod3 = mod.reshape(bsz, 1, 3 * d)

    proj = _inproj(x2, mod3, norm_gain[l].reshape(1, d), w_in[l],
                   LOG2E / math.sqrt(dh), seq)

    bias_full = jnp.repeat(jnp.transpose(b_spatial[l]), d // N_GROUPS, axis=1)

    slopes = jnp.asarray([2.0 ** (-8.0 * (i + 1) / N_HEADS) for i in range(N_HEADS)], F32)
    ybt = _attn(proj, slopes, lambda_q1[l].reshape(1, dh), lambda_k1[l].reshape(1, dh),
                lambda_q2[l].reshape(1, dh), lambda_k2[l].reshape(1, dh),
                subln_gain[l].reshape(2 * dh, 1), bsz, seq)

    out = _tail(proj, ybt, x2, mod3, w_branch_a[l].astype(BF16), w_branch_b[l].astype(BF16),
                w_out[l].astype(BF16), ln_v_gain[l].reshape(1, d), ln_v_bias[l].reshape(1, d),
                w_spatial[l], bias_full, final_norm_gain.reshape(1, d), seq)
    return out.reshape(bsz, seq, d)
```

```python
import functools
import math

import jax
import jax.numpy as jnp
from jax import lax
from jax.experimental import pallas as pl
from jax.experimental.pallas import tpu as pltpu

F32 = jnp.float32
BF16 = jnp.bfloat16

N_HEADS = 8
N_GROUPS = 8
N_SLABS = 9
SLAB_U, SLAB_V, SLAB_Z, SLAB_Q, SLAB_K, SLAB_BV, SLAB_BZ, SLAB_GA, SLAB_GB = range(N_SLABS)
ST = 512
EPS = 1e-6
SUBLN_EPS = 1e-5
LAMBDA_INIT = 0.8 - 0.6 * math.exp(-0.3 * 0)
LOG2E = 1.0 / math.log(2.0)

VMEM_LIMIT = 56 * 1024 * 1024


def _sigmoid(x):
    return 0.5 * jnp.tanh(0.5 * x) + 0.5


def _silu(x):
    hx = 0.5 * x
    return hx * jnp.tanh(hx) + hx


def _gelu_tanh(x):
    c = math.sqrt(2.0 / math.pi)
    hx = 0.5 * x
    return hx * jnp.tanh(x * ((0.044715 * c) * (x * x) + c)) + hx


def _params(*sem):
    return pltpu.CompilerParams(dimension_semantics=sem, vmem_limit_bytes=VMEM_LIMIT)


def _ada_kernel(c_ref, w_ref, b_ref, o_ref):
    c = c_ref[...]
    o_ref[...] = jnp.dot(_silu(c), w_ref[...], preferred_element_type=F32) + b_ref[...]


def _ada(c, w_ada, b_ada, tn=1024):
    bsz, d = c.shape
    n = w_ada.shape[1]
    return pl.pallas_call(
        _ada_kernel,
        grid=(n // tn,),
        in_specs=[
            pl.BlockSpec((bsz, d), lambda j: (0, 0)),
            pl.BlockSpec((d, tn), lambda j: (0, j)),
            pl.BlockSpec((1, tn), lambda j: (0, j)),
        ],
        out_specs=pl.BlockSpec((bsz, tn), lambda j: (0, j)),
        out_shape=jax.ShapeDtypeStruct((bsz, n), F32),
        compiler_params=_params("parallel"),
        name="ada",
    )(c, w_ada, b_ada.reshape(1, n))


def _inproj_kernel(x_ref, g_ref, shift_ref, scale_ref, w_ref, o_ref, h_ref, *,
                   rows, per, q_scale):
    j = pl.program_id(1)

    @pl.when(j == 0)
    def _():
        gsc = g_ref[...] * (1.0 + scale_ref[...])
        sh = shift_ref[...]

        def body(r, carry):
            sl = pl.ds(pl.multiple_of(r * rows, rows), rows)
            x = x_ref[sl, :]
            rstd = lax.rsqrt(jnp.mean(x * x, axis=-1, keepdims=True) + EPS)
            h_ref[sl, :] = ((x * rstd) * gsc + sh).astype(BF16)
            return carry

        lax.fori_loop(0, x_ref.shape[0] // rows, body, 0, unroll=4)

    n_tok, n_feat = o_ref.shape[0], o_ref.shape[1]
    slab = j // per
    tm = h_ref.shape[0]
    row_pieces = ((0, tm // 2), (tm // 2, 3 * tm // 4), (3 * tm // 4, 7 * tm // 8), (7 * tm // 8, tm))
    epilogues = (
        ((SLAB_U, SLAB_V), _gelu_tanh),
        ((SLAB_Z,), _silu),
        ((SLAB_GA, SLAB_GB), _sigmoid),
        ((SLAB_Q,), lambda acc: acc * q_scale),
        ((SLAB_K,), lambda acc: acc),
    )
    for slabs, act in epilogues:
        @pl.when(functools.reduce(jnp.logical_or, [slab == s for s in slabs]))
        def _(act=act):
            w = w_ref[...].astype(BF16)
            for r0, r1 in row_pieces:
                acc = act(jnp.dot(h_ref[r0:r1, :], w, preferred_element_type=F32))
                step = math.gcd(ST, r1 - r0)
                for b in range(n_feat):
                    for r in range(r0, r1, step):
                        a, ra = divmod(r, ST)
                        o_ref[a, b, ra:ra + step, :] = (
                            acc[r - r0:r - r0 + step, b * ST:(b + 1) * ST].astype(o_ref.dtype))

    for s_t, act in ((SLAB_BV, lambda acc: acc), (SLAB_BZ, _silu)):
        @pl.when(slab == s_t)
        def _(act=act):
            acc_t = act(lax.dot_general(w_ref[...].astype(BF16), h_ref[...], (((0,), (1,)), ((), ())),
                                        preferred_element_type=F32))
            for a in range(n_tok):
                for b in range(n_feat):
                    o_ref[a, b] = acc_t[b * ST:(b + 1) * ST, a * ST:(a + 1) * ST].astype(o_ref.dtype)


def _inproj(x2, mod3, gain, w_bf, q_scale, seq, tm=1024, tn=1024):
    m, d = x2.shape
    n = w_bf.shape[1]
    slab = n // N_SLABS
    per = slab // tn
    tiles_per_seq = seq // tm
    return pl.pallas_call(
        functools.partial(_inproj_kernel, rows=32, per=per, q_scale=q_scale),
        grid=(m // tm, n // tn),
        in_specs=[
            pl.BlockSpec((tm, d), lambda i, j: (i, 0)),
            pl.BlockSpec((1, d), lambda i, j: (0, 0)),
            pl.BlockSpec((None, 1, d), lambda i, j: (i // tiles_per_seq, 0, 0)),
            pl.BlockSpec((None, 1, d), lambda i, j: (i // tiles_per_seq, 0, 1)),
            pl.BlockSpec((d, tn), lambda i, j: (0, j)),
        ],
        out_specs=pl.BlockSpec((None, tm // ST, tn // ST, ST, ST),
                               lambda i, j: (j // per, i, j % per, 0, 0)),
        out_shape=jax.ShapeDtypeStruct((N_SLABS, m // ST, slab // ST, ST, ST), BF16),
        scratch_shapes=[pltpu.VMEM((tm, d), BF16)],
        compiler_params=_params("parallel", "arbitrary"),
        name="inproj",
    )(x2, gain, mod3, mod3, w_bf)


def _gmlp_pieces(u_ref, v_ref, z_ref, lng_ref, lnb_ref, ws_ref, bs_ref, o_ref, chunk, gd):
    n_ft, ta, _ = u_ref.shape
    d = n_ft * ST
    state = {}

    def layernorm(c):
        rows = pl.ds(c * chunk, chunk)
        gv = [v_ref[f, rows, :].astype(F32) for f in range(n_ft)]
        mu = sum(jnp.sum(t, axis=-1, keepdims=True) for t in gv) * (1.0 / d)
        xc = [t - mu for t in gv]
        var = sum(jnp.sum(t * t, axis=-1, keepdims=True) for t in xc) * (1.0 / d)
        rstd = lax.rsqrt(var + EPS)
        state[c] = [((xc[f] * rstd) * lng_ref[:, f * ST:(f + 1) * ST]
                     + lnb_ref[:, f * ST:(f + 1) * ST]).astype(BF16) for f in range(n_ft)]

    def group(c, g):
        rows = pl.ds(c * chunk, chunk)
        row = lax.broadcasted_iota(jnp.int32, (chunk, chunk), 0)
        col = lax.broadcasted_iota(jnp.int32, (chunk, chunk), 1)
        f, off = divmod(g * gd, ST)
        ws = (ws_ref[g] * (row >= col).astype(F32)).astype(BF16)
        sv = jnp.dot(ws, state[c][f][:, off:off + gd], preferred_element_type=F32)
        sv = sv + bs_ref[:, g * gd:(g + 1) * gd]
        u = u_ref[f, rows, off:off + gd].astype(F32)
        z = z_ref[f, rows, off:off + gd].astype(F32)
        o_ref[rows, g * gd:(g + 1) * gd] = ((u * sv) * z).astype(o_ref.dtype)

    steps = []
    for c in range(ta // chunk):
        steps.append(functools.partial(layernorm, c))
        steps += [functools.partial(group, c, g) for g in range(N_GROUPS)]
    return steps


def _attn_kernel(slopes_ref, lq1_ref, lk1_ref, lq2_ref, lk2_ref, q_ref, k_ref, vt_ref, zt_ref,
                 sg_ref, o_ref, bias_ref, vta_ref, s_ref, smax_ref, acc_ref, *, dh):
    nq, blk, dv = q_ref.shape
    slope = slopes_ref[pl.program_id(0)] * LOG2E

    @pl.when(pl.program_id(1) == 0)
    def _():
        s_pos = lax.broadcasted_iota(jnp.int32, (blk, blk), 0)
        t_pos = lax.broadcasted_iota(jnp.int32, (blk, blk), 1)
        bias = slope * (s_pos - t_pos).astype(F32)
        bias_ref[0] = bias
        bias_ref[1] = jnp.where(s_pos <= t_pos, bias, -jnp.inf)

    for j in range(nq):
        vta_ref[j, :dv, :] = vt_ref[j]
        vta_ref[j, dv:, :] = jnp.ones((vta_ref.shape[1] - dv, blk), vta_ref.dtype)

    lam_init = jnp.float32(LAMBDA_INIT)
    lam = (jnp.exp(jnp.sum(lq1_ref[...] * lk1_ref[...], axis=-1, keepdims=True))
           - jnp.exp(jnp.sum(lq2_ref[...] * lk2_ref[...], axis=-1, keepdims=True))
           + lam_init)
    gain = sg_ref[...] * (1.0 - lam_init)

    def scores(j, n):
        slot = j % 2
        bias = bias_ref[int(j == n)]
        for c in range(2):
            q = q_ref[n, :, c * dh:(c + 1) * dh]
            k = k_ref[j, :, c * dh:(c + 1) * dh]
            s = lax.dot_general(k, q, (((1,), (1,)), ((), ())), preferred_element_type=F32)
            s = s + bias
            s_ref[slot, c] = s
            smax_ref[slot, c] = jnp.max(s, axis=0, keepdims=True)

    def consume(j, n, m):
        slot = j % 2
        vta = vta_ref[j]
        off = slope * float((j - n) * blk)
        for c in range(2):
            m_new = smax_ref[slot, c] + off
            if m[c] is not None:
                m_new = jnp.maximum(m[c], m_new)
            p = jnp.exp2(s_ref[slot, c] - (m_new - off)).astype(vta.dtype)
            pv = jnp.dot(vta, p, preferred_element_type=F32)
            if m[c] is None:
                acc_ref[c] = pv
            else:
                acc_ref[c] = jnp.exp2(m[c] - m_new) * acc_ref[c] + pv
            m[c] = m_new

    for n in range(nq):
        m = [None, None]
        scores(0, n)
        for j in range(n):
            scores(j + 1, n)
            consume(j, n, m)
        consume(n, n, m)

        a0, a1 = acc_ref[0], acc_ref[1]
        r0 = 1.0 / a0[dv:dv + 1]
        r1 = lam * (1.0 / a1[dv:dv + 1])
        o = a0[:dv] * r0 - a1[:dv] * r1
        o = o * lax.rsqrt(jnp.mean(o * o, axis=0, keepdims=True) + SUBLN_EPS)
        o = (o * gain) * zt_ref[n].astype(F32)
        o_ref[:, n * blk:(n + 1) * blk] = o.astype(o_ref.dtype)


def _attn(proj, slopes, lq1, lk1, lq2, lk2, subln_g, bsz, seq):
    _, n_tt, n_ft, _, _ = proj.shape
    m, d = n_tt * ST, n_ft * ST
    dv = d // N_HEADS
    dh = dv // 2
    nq = seq // ST
    hp = ST // dv
    ones_rows = 16
    tok_major = lambda s: pl.BlockSpec((None, nq, None, ST, dv),
                                       lambda h, b, s=s: (s, b, h // hp, 0, h % hp))
    feat_major = lambda s: pl.BlockSpec((None, nq, None, dv, ST),
                                        lambda h, b, s=s: (s, b, h // hp, h % hp, 0))
    vec = pl.BlockSpec((1, dh), lambda h, b: (0, 0))
    return pl.pallas_call(
        functools.partial(_attn_kernel, dh=dh),
        grid=(N_HEADS, bsz),
        in_specs=[
            pl.BlockSpec(memory_space=pltpu.SMEM),
            vec, vec, vec, vec,
            tok_major(SLAB_Q), tok_major(SLAB_K), feat_major(SLAB_BV), feat_major(SLAB_BZ),
            pl.BlockSpec((dv, 1), lambda h, b: (0, 0)),
        ],
        out_specs=pl.BlockSpec((dv, seq), lambda h, b: (h, b)),
        out_shape=jax.ShapeDtypeStruct((d, m), BF16),
        scratch_shapes=[
            pltpu.VMEM((2, ST, ST), F32),
            pltpu.VMEM((nq, dv + ones_rows, ST), BF16),
            pltpu.VMEM((2, 2, ST, ST), F32),
            pltpu.VMEM((2, 2, 1, ST), F32),
            pltpu.VMEM((2, dv + ones_rows, ST), F32),
        ],
        compiler_params=_params("arbitrary", "arbitrary"),
        name="attn",
    )(slopes, lq1, lk1, lq2, lk2, proj, proj, proj, proj, subln_g)


def _tail_kernel(u_ref, v_ref, z_ref, ga_ref, gb_ref, ybt_ref, x_ref, gate_ref, wa_ref, wb_ref,
                 wo_ref, lng_ref, lnb_ref, ws_ref, bs_ref, fg_ref, o_ref, ya_ref, b_ref, m_ref, *,
                 chunk, gd, nc):
    d = wb_ref.shape[1]
    cols = [slice(k, k + nc) for k in range(0, d, nc)]

    gm = _gmlp_pieces(u_ref, v_ref, z_ref, lng_ref, lnb_ref, ws_ref, bs_ref, ya_ref, chunk, gd)
    per = -(-len(gm) // len(cols))
    for k, cs in enumerate(cols):
        b_ref[:, cs] = lax.dot_general(ybt_ref[...], wb_ref[:, cs], (((0,), (0,)), ((), ())),
                                       preferred_element_type=F32)
        for step in gm[k * per:(k + 1) * per]:
            step()

    def merge(cs, a):
        f, off = divmod(cs.start, ST)
        ga = ga_ref[f, :, off:off + nc].astype(F32)
        gb = gb_ref[f, :, off:off + nc].astype(F32)
        m_ref[:, cs] = (ga * a + gb * b_ref[:, cs]).astype(m_ref.dtype)

    pending = None
    for cs in cols:
        a = jnp.dot(ya_ref[...], wa_ref[:, cs], preferred_element_type=F32)
        if pending is not None:
            merge(*pending)
        pending = (cs, a)
    merge(*pending)

    ssq = None
    for cs in cols:
        out = jnp.dot(m_ref[...], wo_ref[:, cs], preferred_element_type=F32)
        r = x_ref[:, cs] + gate_ref[:, cs] * out
        o_ref[:, cs] = r
        part = jnp.sum(r * r, axis=-1, keepdims=True)
        ssq = part if ssq is None else ssq + part
    rstd = lax.rsqrt(ssq * (1.0 / d) + EPS)
    o_ref[...] = (o_ref[...] * rstd) * fg_ref[...]


def _tail(proj, ybt, x2, mod3, wa, wb, wo, ln_g, ln_b, w_s, bias_full, fgain, seq, tm=256):
    _, n_tt, n_ft, _, _ = proj.shape
    m, d = x2.shape
    chunk = w_s.shape[-1]
    per = ST // tm
    tiles_per_seq = seq // tm
    pspec = lambda s: pl.BlockSpec((None, None, n_ft, tm, ST),
                                   lambda i, s=s: (s, i // per, 0, i % per, 0))
    rows = pl.BlockSpec((tm, d), lambda i: (i, 0))
    vec = pl.BlockSpec((1, d), lambda i: (0, 0))
    weight = pl.BlockSpec((d, d), lambda i: (0, 0), pipeline_mode=pl.Buffered(1))
    return pl.pallas_call(
        functools.partial(_tail_kernel, chunk=chunk, gd=d // N_GROUPS, nc=512),
        grid=(m // tm,),
        in_specs=[
            pspec(SLAB_U), pspec(SLAB_V), pspec(SLAB_Z), pspec(SLAB_GA), pspec(SLAB_GB),
            pl.BlockSpec((d, tm), lambda i: (0, i)), rows,
            pl.BlockSpec((None, 1, d), lambda i: (i // tiles_per_seq, 0, 2)),
            weight, weight, weight,
            vec, vec,
            pl.BlockSpec((N_GROUPS, chunk, chunk), lambda i: (0, 0, 0)),
            pl.BlockSpec((chunk, d), lambda i: (0, 0)),
            vec,
        ],
        out_specs=rows,
        out_shape=jax.ShapeDtypeStruct((m, d), F32),
        scratch_shapes=[pltpu.VMEM((tm, d), BF16), pltpu.VMEM((tm, d), F32),
                        pltpu.VMEM((tm, d), BF16)],
        compiler_params=_params("parallel"),
        name="tail",
    )(proj, proj, proj, proj, proj, ybt, x2, mod3, wa, wb, wo, ln_g, ln_b, w_s, bias_full, fgain)


def kernel(x, c, w_ada, b_ada, norm_gain, w_in, ln_v_gain, ln_v_bias, w_spatial, b_spatial,
           lambda_q1, lambda_k1, lambda_q2, lambda_k2, subln_gain, w_branch_a, w_branch_b,
           w_out, final_norm_gain):
    bsz, seq, d = x.shape
    depth = w_ada.shape[0]
    assert depth == 1, "single-layer trunk"
    dh = lambda_q1.shape[-1]
    assert d == N_HEADS * 2 * dh and w_in.shape[-1] == N_SLABS * d
    assert seq % ST == 0 and d % ST == 0 and ST % (2 * dh) == 0
    l = 0

    x2 = x.reshape(bsz * seq, d)
    mod = _ada(c, w_ada[l], b_ada[l])
    mod3 = mod.reshape(bsz, 1, 3 * d)

    proj = _inproj(x2, mod3, norm_gain[l].reshape(1, d), w_in[l],
                   LOG2E / math.sqrt(dh), seq)

    bias_full = jnp.repeat(jnp.transpose(b_spatial[l]), d // N_GROUPS, axis=1)

    slopes = jnp.asarray([2.0 ** (-8.0 * (i + 1) / N_HEADS) for i in range(N_HEADS)], F32)
    ybt = _attn(proj, slopes, lambda_q1[l].reshape(1, dh), lambda_k1[l].reshape(1, dh),
                lambda_q2[l].reshape(1, dh), lambda_k2[l].reshape(1, dh),
                subln_gain[l].reshape(2 * dh, 1), bsz, seq)

    out = _tail(proj, ybt, x2, mod3, w_branch_a[l].astype(BF16), w_branch_b[l].astype(BF16),
                w_out[l].astype(BF16), ln_v_gain[l].reshape(1, d), ln_v_bias[l].reshape(1, d),
                w_spatial[l], bias_full, final_norm_gain.reshape(1, d), seq)
    return out.reshape(bsz, seq, d)
```

```python
import functools
import math

import jax
import jax.numpy as jnp
from jax import lax
from jax.experimental import pallas as pl
from jax.experimental.pallas import tpu as pltpu

F32 = jnp.float32
BF16 = jnp.bfloat16

N_HEADS = 8
N_GROUPS = 8
N_SLABS = 9
SLAB_U, SLAB_V, SLAB_Z, SLAB_Q, SLAB_K, SLAB_BV, SLAB_BZ, SLAB_GA, SLAB_GB = range(N_SLABS)
ST = 512
EPS = 1e-6
SUBLN_EPS = 1e-5
LAMBDA_INIT = 0.8 - 0.6 * math.exp(-0.3 * 0)
LOG2E = 1.0 / math.log(2.0)

VMEM_LIMIT = 56 * 1024 * 1024


def _sigmoid(x):
    return 0.5 * jnp.tanh(0.5 * x) + 0.5


def _silu(x):
    hx = 0.5 * x
    return hx * jnp.tanh(hx) + hx


def _gelu_tanh(x):
    c = math.sqrt(2.0 / math.pi)
    hx = 0.5 * x
    return hx * jnp.tanh(x * ((0.044715 * c) * (x * x) + c)) + hx


def _params(*sem):
    return pltpu.CompilerParams(dimension_semantics=sem, vmem_limit_bytes=VMEM_LIMIT)


def _ada_kernel(c_ref, w_ref, b_ref, o_ref):
    c = c_ref[...]
    o_ref[...] = jnp.dot(_silu(c), w_ref[...], preferred_element_type=F32) + b_ref[...]


def _ada(c, w_ada, b_ada, tn=1024):
    bsz, d = c.shape
    n = w_ada.shape[1]
    return pl.pallas_call(
        _ada_kernel,
        grid=(n // tn,),
        in_specs=[
            pl.BlockSpec((bsz, d), lambda j: (0, 0)),
            pl.BlockSpec((d, tn), lambda j: (0, j)),
            pl.BlockSpec((1, tn), lambda j: (0, j)),
        ],
        out_specs=pl.BlockSpec((bsz, tn), lambda j: (0, j)),
        out_shape=jax.ShapeDtypeStruct((bsz, n), F32),
        compiler_params=_params("parallel"),
        name="ada",
    )(c, w_ada, b_ada.reshape(1, n))


def _norm_kernel(x_ref, g_ref, shift_ref, scale_ref, h_ref, *, rows):
    gsc = g_ref[...] * (1.0 + scale_ref[...])
    sh = shift_ref[...]

    def body(r, carry):
        sl = pl.ds(pl.multiple_of(r * rows, rows), rows)
        x = x_ref[sl, :]
        rstd = lax.rsqrt(jnp.mean(x * x, axis=-1, keepdims=True) + EPS)
        h_ref[sl, :] = ((x * rstd) * gsc + sh).astype(h_ref.dtype)
        return carry

    lax.fori_loop(0, x_ref.shape[0] // rows, body, 0, unroll=4)


def _norm(x2, mod3, gain, seq, tm=512):
    m, d = x2.shape
    tiles_per_seq = seq // tm
    return pl.pallas_call(
        functools.partial(_norm_kernel, rows=32),
        grid=(m // tm,),
        in_specs=[
            pl.BlockSpec((tm, d), lambda i: (i, 0)),
            pl.BlockSpec((1, d), lambda i: (0, 0)),
            pl.BlockSpec((None, 1, d), lambda i: (i // tiles_per_seq, 0, 0)),
            pl.BlockSpec((None, 1, d), lambda i: (i // tiles_per_seq, 0, 1)),
        ],
        out_specs=pl.BlockSpec((tm, d), lambda i: (i, 0)),
        out_shape=jax.ShapeDtypeStruct((m, d), BF16),
        compiler_params=_params("parallel"),
        name="norm",
    )(x2, gain, mod3, mod3)


def _inproj_kernel(h_ref, w_ref, o_ref, *, per, q_scale):
    n_tok, n_feat = o_ref.shape[0], o_ref.shape[1]
    slab = pl.program_id(1) // per
    epilogues = (
        ((SLAB_U, SLAB_V), _gelu_tanh),
        ((SLAB_Z,), _silu),
        ((SLAB_GA, SLAB_GB), _sigmoid),
        ((SLAB_Q,), lambda acc: acc * q_scale),
        ((SLAB_K,), lambda acc: acc),
    )
    for slabs, act in epilogues:
        @pl.when(functools.reduce(jnp.logical_or, [slab == s for s in slabs]))
        def _(act=act):
            acc = act(jnp.dot(h_ref[...], w_ref[...].astype(BF16), preferred_element_type=F32))
            for a in range(n_tok):
                for b in range(n_feat):
                    o_ref[a, b] = acc[a * ST:(a + 1) * ST, b * ST:(b + 1) * ST].astype(o_ref.dtype)

    for s_t, act in ((SLAB_BV, lambda acc: acc), (SLAB_BZ, _silu)):
        @pl.when(slab == s_t)
        def _(act=act):
            acc_t = act(lax.dot_general(w_ref[...].astype(BF16), h_ref[...], (((0,), (1,)), ((), ())),
                                        preferred_element_type=F32))
            for a in range(n_tok):
                for b in range(n_feat):
                    o_ref[a, b] = acc_t[b * ST:(b + 1) * ST, a * ST:(a + 1) * ST].astype(o_ref.dtype)


def _inproj(h, w_in, q_scale, tm=2048, tn=1024):
    m, d = h.shape
    n = w_in.shape[1]
    slab = n // N_SLABS
    per = slab // tn
    return pl.pallas_call(
        functools.partial(_inproj_kernel, per=per, q_scale=q_scale),
        grid=(m // tm, n // tn),
        in_specs=[
            pl.BlockSpec((tm, d), lambda i, j: (i, 0)),
            pl.BlockSpec((d, tn), lambda i, j: (0, j)),
        ],
        out_specs=pl.BlockSpec((None, tm // ST, tn // ST, ST, ST),
                               lambda i, j: (j // per, i, j % per, 0, 0)),
        out_shape=jax.ShapeDtypeStruct((N_SLABS, m // ST, slab // ST, ST, ST), BF16),
        compiler_params=_params("parallel", "arbitrary"),
        name="inproj",
    )(h, w_in)


def _gmlp_pieces(u_ref, v_ref, z_ref, lng_ref, lnb_ref, ws_ref, bs_ref, o_ref, chunk, gd):
    n_ft, ta, _ = u_ref.shape
    d = n_ft * ST
    state = {}

    def layernorm(c):
        rows = pl.ds(c * chunk, chunk)
        gv = [v_ref[f, rows, :].astype(F32) for f in range(n_ft)]
        mu = sum(jnp.sum(t, axis=-1, keepdims=True) for t in gv) * (1.0 / d)
        xc = [t - mu for t in gv]
        var = sum(jnp.sum(t * t, axis=-1, keepdims=True) for t in xc) * (1.0 / d)
        rstd = lax.rsqrt(var + EPS)
        state[c] = [((xc[f] * rstd) * lng_ref[:, f * ST:(f + 1) * ST]
                     + lnb_ref[:, f * ST:(f + 1) * ST]).astype(BF16) for f in range(n_ft)]

    def group(c, g):
        rows = pl.ds(c * chunk, chunk)
        row = lax.broadcasted_iota(jnp.int32, (chunk, chunk), 0)
        col = lax.broadcasted_iota(jnp.int32, (chunk, chunk), 1)
        f, off = divmod(g * gd, ST)
        ws = (ws_ref[g] * (row >= col).astype(F32)).astype(BF16)
        sv = jnp.dot(ws, state[c][f][:, off:off + gd], preferred_element_type=F32)
        sv = sv + bs_ref[:, g * gd:(g + 1) * gd]
        u = u_ref[f, rows, off:off + gd].astype(F32)
        z = z_ref[f, rows, off:off + gd].astype(F32)
        o_ref[rows, g * gd:(g + 1) * gd] = ((u * sv) * z).astype(o_ref.dtype)

    steps = []
    for c in range(ta // chunk):
        steps.append(functools.partial(layernorm, c))
        steps += [functools.partial(group, c, g) for g in range(N_GROUPS)]
    return steps


def _attn_kernel(slopes_ref, lq1_ref, lk1_ref, lq2_ref, lk2_ref, q_ref, k_ref, vt_ref, zt_ref,
                 sg_ref, o_ref, bias_ref, vta_ref, s_ref, smax_ref, acc_ref, *, dh):
    nq, blk, dv = q_ref.shape
    slope = slopes_ref[pl.program_id(0)] * LOG2E

    @pl.when(pl.program_id(1) == 0)
    def _():
        s_pos = lax.broadcasted_iota(jnp.int32, (blk, blk), 0)
        t_pos = lax.broadcasted_iota(jnp.int32, (blk, blk), 1)
        bias = slope * (s_pos - t_pos).astype(F32)
        bias_ref[0] = bias
        bias_ref[1] = jnp.where(s_pos <= t_pos, bias, -jnp.inf)

    for j in range(nq):
        vta_ref[j, :dv, :] = vt_ref[j]
        vta_ref[j, dv:, :] = jnp.ones((vta_ref.shape[1] - dv, blk), vta_ref.dtype)

    lam_init = jnp.float32(LAMBDA_INIT)
    lam = (jnp.exp(jnp.sum(lq1_ref[...] * lk1_ref[...], axis=-1, keepdims=True))
           - jnp.exp(jnp.sum(lq2_ref[...] * lk2_ref[...], axis=-1, keepdims=True))
           + lam_init)
    gain = sg_ref[...] * (1.0 - lam_init)

    def scores(j, n):
        slot = j % 2
        bias = bias_ref[int(j == n)]
        for c in range(2):
            q = q_ref[n, :, c * dh:(c + 1) * dh]
            k = k_ref[j, :, c * dh:(c + 1) * dh]
            s = lax.dot_general(k, q, (((1,), (1,)), ((), ())), preferred_element_type=F32)
            s = s + bias
            s_ref[slot, c] = s
            smax_ref[slot, c] = jnp.max(s, axis=0, keepdims=True)

    def consume(j, n, m):
        slot = j % 2
        vta = vta_ref[j]
        off = slope * float((j - n) * blk)
        for c in range(2):
            m_new = smax_ref[slot, c] + off
            if m[c] is not None:
                m_new = jnp.maximum(m[c], m_new)
            p = jnp.exp2(s_ref[slot, c] - (m_new - off)).astype(vta.dtype)
            pv = jnp.dot(vta, p, preferred_element_type=F32)
            if m[c] is None:
                acc_ref[c] = pv
            else:
                acc_ref[c] = jnp.exp2(m[c] - m_new) * acc_ref[c] + pv
            m[c] = m_new

    for n in range(nq):
        m = [None, None]
        scores(0, n)
        for j in range(n):
            scores(j + 1, n)
            consume(j, n, m)
        consume(n, n, m)

        a0, a1 = acc_ref[0], acc_ref[1]
        r0 = 1.0 / a0[dv:dv + 1]
        r1 = lam * (1.0 / a1[dv:dv + 1])
        o = a0[:dv] * r0 - a1[:dv] * r1
        o = o * lax.rsqrt(jnp.mean(o * o, axis=0, keepdims=True) + SUBLN_EPS)
        o = (o * gain) * zt_ref[n].astype(F32)
        o_ref[:, n * blk:(n + 1) * blk] = o.astype(o_ref.dtype)


def _attn(proj, slopes, lq1, lk1, lq2, lk2, subln_g, bsz, seq):
    _, n_tt, n_ft, _, _ = proj.shape
    m, d = n_tt * ST, n_ft * ST
    dv = d // N_HEADS
    dh = dv // 2
    nq = seq // ST
    hp = ST // dv
    ones_rows = 16
    tok_major = lambda s: pl.BlockSpec((None, nq, None, ST, dv),
                                       lambda h, b, s=s: (s, b, h // hp, 0, h % hp))
    feat_major = lambda s: pl.BlockSpec((None, nq, None, dv, ST),
                                        lambda h, b, s=s: (s, b, h // hp, h % hp, 0))
    vec = pl.BlockSpec((1, dh), lambda h, b: (0, 0))
    return pl.pallas_call(
        functools.partial(_attn_kernel, dh=dh),
        grid=(N_HEADS, bsz),
        in_specs=[
            pl.BlockSpec(memory_space=pltpu.SMEM),
            vec, vec, vec, vec,
            tok_major(SLAB_Q), tok_major(SLAB_K), feat_major(SLAB_BV), feat_major(SLAB_BZ),
            pl.BlockSpec((dv, 1), lambda h, b: (0, 0)),
        ],
        out_specs=pl.BlockSpec((dv, seq), lambda h, b: (h, b)),
        out_shape=jax.ShapeDtypeStruct((d, m), BF16),
        scratch_shapes=[
            pltpu.VMEM((2, ST, ST), F32),
            pltpu.VMEM((nq, dv + ones_rows, ST), BF16),
            pltpu.VMEM((2, 2, ST, ST), F32),
            pltpu.VMEM((2, 2, 1, ST), F32),
            pltpu.VMEM((2, dv + ones_rows, ST), F32),
        ],
        compiler_params=_params("arbitrary", "arbitrary"),
        name="attn",
    )(slopes, lq1, lk1, lq2, lk2, proj, proj, proj, proj, subln_g)


def _tail_kernel(u_ref, v_ref, z_ref, ga_ref, gb_ref, ybt_ref, x_ref, gate_ref, wa_ref, wb_ref,
                 wo_ref, lng_ref, lnb_ref, ws_ref, bs_ref, fg_ref, o_ref, ya_ref, b_ref, m_ref, *,
                 chunk, gd, nc):
    d = wb_ref.shape[1]
    cols = [slice(k, k + nc) for k in range(0, d, nc)]

    gm = _gmlp_pieces(u_ref, v_ref, z_ref, lng_ref, lnb_ref, ws_ref, bs_ref, ya_ref, chunk, gd)
    per = -(-len(gm) // len(cols))
    for k, cs in enumerate(cols):
        b_ref[:, cs] = lax.dot_general(ybt_ref[...], wb_ref[:, cs], (((0,), (0,)), ((), ())),
                                       preferred_element_type=F32)
        for step in gm[k * per:(k + 1) * per]:
            step()

    def merge(cs, a):
        f, off = divmod(cs.start, ST)
        ga = ga_ref[f, :, off:off + nc].astype(F32)
        gb = gb_ref[f, :, off:off + nc].astype(F32)
        m_ref[:, cs] = (ga * a + gb * b_ref[:, cs]).astype(m_ref.dtype)

    pending = None
    for cs in cols:
        a = jnp.dot(ya_ref[...], wa_ref[:, cs], preferred_element_type=F32)
        if pending is not None:
            merge(*pending)
        pending = (cs, a)
    merge(*pending)

    ssq = None
    for cs in cols:
        out = jnp.dot(m_ref[...], wo_ref[:, cs], preferred_element_type=F32)
        r = x_ref[:, cs] + gate_ref[:, cs] * out
        o_ref[:, cs] = r
        part = jnp.sum(r * r, axis=-1, keepdims=True)
        ssq = part if ssq is None else ssq + part
    rstd = lax.rsqrt(ssq * (1.0 / d) + EPS)
    o_ref[...] = (o_ref[...] * rstd) * fg_ref[...]


def _tail(proj, ybt, x2, mod3, wa, wb, wo, ln_g, ln_b, w_s, bias_full, fgain, seq, tm=256):
    _, n_tt, n_ft, _, _ = proj.shape
    m, d = x2.shape
    chunk = w_s.shape[-1]
    per = ST // tm
    tiles_per_seq = seq // tm
    pspec = lambda s: pl.BlockSpec((None, None, n_ft, tm, ST),
                                   lambda i, s=s: (s, i // per, 0, i % per, 0))
    rows = pl.BlockSpec((tm, d), lambda i: (i, 0))
    vec = pl.BlockSpec((1, d), lambda i: (0, 0))
    weight = pl.BlockSpec((d, d), lambda i: (0, 0), pipeline_mode=pl.Buffered(1))
    return pl.pallas_call(
        functools.partial(_tail_kernel, chunk=chunk, gd=d // N_GROUPS, nc=512),
        grid=(m // tm,),
        in_specs=[
            pspec(SLAB_U), pspec(SLAB_V), pspec(SLAB_Z), pspec(SLAB_GA), pspec(SLAB_GB),
            pl.BlockSpec((d, tm), lambda i: (0, i)), rows,
            pl.BlockSpec((None, 1, d), lambda i: (i // tiles_per_seq, 0, 2)),
            weight, weight, weight,
            vec, vec,
            pl.BlockSpec((N_GROUPS, chunk, chunk), lambda i: (0, 0, 0)),
            pl.BlockSpec((chunk, d), lambda i: (0, 0)),
            vec,
        ],
        out_specs=rows,
        out_shape=jax.ShapeDtypeStruct((m, d), F32),
        scratch_shapes=[pltpu.VMEM((tm, d), BF16), pltpu.VMEM((tm, d), F32),
                        pltpu.VMEM((tm, d), BF16)],
        compiler_params=_params("parallel"),
        name="tail",
    )(proj, proj, proj, proj, proj, ybt, x2, mod3, wa, wb, wo, ln_g, ln_b, w_s, bias_full, fgain)


def kernel(x, c, w_ada, b_ada, norm_gain, w_in, ln_v_gain, ln_v_bias, w_spatial, b_spatial,
           lambda_q1, lambda_k1, lambda_q2, lambda_k2, subln_gain, w_branch_a, w_branch_b,
           w_out, final_norm_gain):
    bsz, seq, d = x.shape
    depth = w_ada.shape[0]
    assert depth == 1, "single-layer trunk"
    dh = lambda_q1.shape[-1]
    assert d == N_HEADS * 2 * dh and w_in.shape[-1] == N_SLABS * d
    assert seq % ST == 0 and d % ST == 0 and ST % (2 * dh) == 0
    l = 0

    x2 = x.reshape(bsz * seq, d)
    mod = _ada(c, w_ada[l], b_ada[l])
    mod3 = mod.reshape(bsz, 1, 3 * d)

    h = _norm(x2, mod3, norm_gain[l].reshape(1, d), seq)
    proj = _inproj(h, w_in[l], LOG2E / math.sqrt(dh))

    bias_full = jnp.repeat(jnp.transpose(b_spatial[l]), d // N_GROUPS, axis=1)

    slopes = jnp.asarray([2.0 ** (-8.0 * (i + 1) / N_HEADS) for i in range(N_HEADS)], F32)
    ybt = _attn(proj, slopes, lambda_q1[l].reshape(1, dh), lambda_k1[l].reshape(1, dh),
                lambda_q2[l].reshape(1, dh), lambda_k2[l].reshape(1, dh),
                subln_gain[l].reshape(2 * dh, 1), bsz, seq)

    out = _tail(proj, ybt, x2, mod3, w_branch_a[l].astype(BF16), w_branch_b[l].astype(BF16),
                w_out[l].astype(BF16), ln_v_gain[l].reshape(1, d), ln_v_bias[l].reshape(1, d),
                w_spatial[l], bias_full, final_norm_gain.reshape(1, d), seq)
    return out.reshape(bsz, seq, d)
```

```python
import functools
import math

import jax
import jax.numpy as jnp
from jax import lax
from jax.experimental import pallas as pl
from jax.experimental.pallas import tpu as pltpu

F32 = jnp.float32
BF16 = jnp.bfloat16

N_HEADS = 8
N_GROUPS = 8
N_SLABS = 9
SLAB_U, SLAB_V, SLAB_Z, SLAB_Q, SLAB_K, SLAB_BV, SLAB_BZ, SLAB_GA, SLAB_GB = range(N_SLABS)
ST = 512
EPS = 1e-6
SUBLN_EPS = 1e-5
LAMBDA_INIT = 0.8 - 0.6 * math.exp(-0.3 * 0)
LOG2E = 1.0 / math.log(2.0)

VMEM_LIMIT = 56 * 1024 * 1024


def _sigmoid(x):
    return 0.5 * jnp.tanh(0.5 * x) + 0.5


def _silu(x):
    hx = 0.5 * x
    return hx * jnp.tanh(hx) + hx


def _gelu_tanh(x):
    c = math.sqrt(2.0 / math.pi)
    hx = 0.5 * x
    return hx * jnp.tanh(x * ((0.044715 * c) * (x * x) + c)) + hx


def _params(*sem):
    return pltpu.CompilerParams(dimension_semantics=sem, vmem_limit_bytes=VMEM_LIMIT)


def _ada_kernel(c_ref, w_ref, b_ref, o_ref):
    c = c_ref[...]
    o_ref[...] = jnp.dot(_silu(c), w_ref[...], preferred_element_type=F32) + b_ref[...]


def _ada(c, w_ada, b_ada, tn=1024):
    bsz, d = c.shape
    n = w_ada.shape[1]
    return pl.pallas_call(
        _ada_kernel,
        grid=(n // tn,),
        in_specs=[
            pl.BlockSpec((bsz, d), lambda j: (0, 0)),
            pl.BlockSpec((d, tn), lambda j: (0, j)),
            pl.BlockSpec((1, tn), lambda j: (0, j)),
        ],
        out_specs=pl.BlockSpec((bsz, tn), lambda j: (0, j)),
        out_shape=jax.ShapeDtypeStruct((bsz, n), F32),
        compiler_params=_params("parallel"),
        name="ada",
    )(c, w_ada, b_ada.reshape(1, n))


def _inproj_kernel(x_ref, g_ref, shift_ref, scale_ref, w_ref, o_ref, h_ref, *,
                   rows, per, q_scale):
    j = pl.program_id(1)

    @pl.when(j == 0)
    def _():
        gsc = g_ref[...] * (1.0 + scale_ref[...])
        sh = shift_ref[...]

        def body(r, carry):
            sl = pl.ds(pl.multiple_of(r * rows, rows), rows)
            x = x_ref[sl, :]
            rstd = lax.rsqrt(jnp.mean(x * x, axis=-1, keepdims=True) + EPS)
            h_ref[sl, :] = ((x * rstd) * gsc + sh).astype(BF16)
            return carry

        lax.fori_loop(0, x_ref.shape[0] // rows, body, 0, unroll=4)

    n_tok, n_feat = o_ref.shape[0], o_ref.shape[1]
    slab = j // per
    epilogues = (
        ((SLAB_U, SLAB_V), _gelu_tanh),
        ((SLAB_Z,), _silu),
        ((SLAB_GA, SLAB_GB), _sigmoid),
        ((SLAB_Q,), lambda acc: acc * q_scale),
        ((SLAB_K,), lambda acc: acc),
    )
    for slabs, act in epilogues:
        @pl.when(functools.reduce(jnp.logical_or, [slab == s for s in slabs]))
        def _(act=act):
            acc = act(jnp.dot(h_ref[...], w_ref[...].astype(BF16), preferred_element_type=F32))
            for a in range(n_tok):
                for b in range(n_feat):
                    o_ref[a, b] = acc[a * ST:(a + 1) * ST, b * ST:(b + 1) * ST].astype(o_ref.dtype)

    for s_t, act in ((SLAB_BV, lambda acc: acc), (SLAB_BZ, _silu)):
        @pl.when(slab == s_t)
        def _(act=act):
            acc_t = act(lax.dot_general(w_ref[...].astype(BF16), h_ref[...], (((0,), (1,)), ((), ())),
                                        preferred_element_type=F32))
            for a in range(n_tok):
                for b in range(n_feat):
                    o_ref[a, b] = acc_t[b * ST:(b + 1) * ST, a * ST:(a + 1) * ST].astype(o_ref.dtype)


def _inproj(x2, mod3, gain, w_bf, q_scale, seq, tm=1024, tn=1024):
    m, d = x2.shape
    n = w_bf.shape[1]
    slab = n // N_SLABS
    per = slab // tn
    tiles_per_seq = seq // tm
    return pl.pallas_call(
        functools.partial(_inproj_kernel, rows=32, per=per, q_scale=q_scale),
        grid=(m // tm, n // tn),
        in_specs=[
            pl.BlockSpec((tm, d), lambda i, j: (i, 0)),
            pl.BlockSpec((1, d), lambda i, j: (0, 0)),
            pl.BlockSpec((None, 1, d), lambda i, j: (i // tiles_per_seq, 0, 0)),
            pl.BlockSpec((None, 1, d), lambda i, j: (i // tiles_per_seq, 0, 1)),
            pl.BlockSpec((d, tn), lambda i, j: (0, j)),
        ],
        out_specs=pl.BlockSpec((None, tm // ST, tn // ST, ST, ST),
                               lambda i, j: (j // per, i, j % per, 0, 0)),
        out_shape=jax.ShapeDtypeStruct((N_SLABS, m // ST, slab // ST, ST, ST), BF16),
        scratch_shapes=[pltpu.VMEM((tm, d), BF16)],
        compiler_params=_params("parallel", "arbitrary"),
        name="inproj",
    )(x2, gain, mod3, mod3, w_bf)


def _gmlp_pieces(u_ref, v_ref, z_ref, lng_ref, lnb_ref, ws_ref, bs_ref, o_ref, chunk, gd):
    n_ft, ta, _ = u_ref.shape
    d = n_ft * ST
    state = {}

    def layernorm(c):
        rows = pl.ds(c * chunk, chunk)
        gv = [v_ref[f, rows, :].astype(F32) for f in range(n_ft)]
        mu = sum(jnp.sum(t, axis=-1, keepdims=True) for t in gv) * (1.0 / d)
        xc = [t - mu for t in gv]
        var = sum(jnp.sum(t * t, axis=-1, keepdims=True) for t in xc) * (1.0 / d)
        rstd = lax.rsqrt(var + EPS)
        state[c] = [((xc[f] * rstd) * lng_ref[:, f * ST:(f + 1) * ST]
                     + lnb_ref[:, f * ST:(f + 1) * ST]).astype(BF16) for f in range(n_ft)]

    def group(c, g):
        rows = pl.ds(c * chunk, chunk)
        row = lax.broadcasted_iota(jnp.int32, (chunk, chunk), 0)
        col = lax.broadcasted_iota(jnp.int32, (chunk, chunk), 1)
        f, off = divmod(g * gd, ST)
        ws = (ws_ref[g] * (row >= col).astype(F32)).astype(BF16)
        sv = jnp.dot(ws, state[c][f][:, off:off + gd], preferred_element_type=F32)
        sv = sv + bs_ref[:, g * gd:(g + 1) * gd]
        u = u_ref[f, rows, off:off + gd].astype(F32)
        z = z_ref[f, rows, off:off + gd].astype(F32)
        o_ref[rows, g * gd:(g + 1) * gd] = ((u * sv) * z).astype(o_ref.dtype)

    steps = []
    for c in range(ta // chunk):
        steps.append(functools.partial(layernorm, c))
        steps += [functools.partial(group, c, g) for g in range(N_GROUPS)]
    return steps


def _attn_kernel(slopes_ref, lq1_ref, lk1_ref, lq2_ref, lk2_ref, q_ref, k_ref, vt_ref, zt_ref,
                 sg_ref, wa_ref, wb_ref, wo_ref, o_ref, wa_bf_ref, wb_bf_ref, wo_bf_ref,
                 bias_ref, vta_ref, s_ref, smax_ref, acc_ref, *, dh):
    nq, blk, dv = q_ref.shape
    for src, dst in ((wa_ref, wa_bf_ref), (wb_ref, wb_bf_ref), (wo_ref, wo_bf_ref)):
        dst[...] = src[...].astype(dst.dtype)
    slope = slopes_ref[pl.program_id(0)] * LOG2E

    @pl.when(pl.program_id(1) == 0)
    def _():
        s_pos = lax.broadcasted_iota(jnp.int32, (blk, blk), 0)
        t_pos = lax.broadcasted_iota(jnp.int32, (blk, blk), 1)
        bias = slope * (s_pos - t_pos).astype(F32)
        bias_ref[0] = bias
        bias_ref[1] = jnp.where(s_pos <= t_pos, bias, -jnp.inf)

    for j in range(nq):
        vta_ref[j, :dv, :] = vt_ref[j]
        vta_ref[j, dv:, :] = jnp.ones((vta_ref.shape[1] - dv, blk), vta_ref.dtype)

    lam_init = jnp.float32(LAMBDA_INIT)
    lam = (jnp.exp(jnp.sum(lq1_ref[...] * lk1_ref[...], axis=-1, keepdims=True))
           - jnp.exp(jnp.sum(lq2_ref[...] * lk2_ref[...], axis=-1, keepdims=True))
           + lam_init)
    gain = sg_ref[...] * (1.0 - lam_init)

    def scores(j, n):
        slot = j % 2
        bias = bias_ref[int(j == n)]
        for c in range(2):
            q = q_ref[n, :, c * dh:(c + 1) * dh]
            k = k_ref[j, :, c * dh:(c + 1) * dh]
            s = lax.dot_general(k, q, (((1,), (1,)), ((), ())), preferred_element_type=F32)
            s = s + bias
            s_ref[slot, c] = s
            smax_ref[slot, c] = jnp.max(s, axis=0, keepdims=True)

    def consume(j, n, m):
        slot = j % 2
        vta = vta_ref[j]
        off = slope * float((j - n) * blk)
        for c in range(2):
            m_new = smax_ref[slot, c] + off
            if m[c] is not None:
                m_new = jnp.maximum(m[c], m_new)
            p = jnp.exp2(s_ref[slot, c] - (m_new - off)).astype(vta.dtype)
            pv = jnp.dot(vta, p, preferred_element_type=F32)
            if m[c] is None:
                acc_ref[c] = pv
            else:
                acc_ref[c] = jnp.exp2(m[c] - m_new) * acc_ref[c] + pv
            m[c] = m_new

    for n in range(nq):
        m = [None, None]
        scores(0, n)
        for j in range(n):
            scores(j + 1, n)
            consume(j, n, m)
        consume(n, n, m)

        a0, a1 = acc_ref[0], acc_ref[1]
        r0 = 1.0 / a0[dv:dv + 1]
        r1 = lam * (1.0 / a1[dv:dv + 1])
        o = a0[:dv] * r0 - a1[:dv] * r1
        o = o * lax.rsqrt(jnp.mean(o * o, axis=0, keepdims=True) + SUBLN_EPS)
        o = (o * gain) * zt_ref[n].astype(F32)
        o_ref[:, n * blk:(n + 1) * blk] = o.astype(o_ref.dtype)


def _attn(proj, slopes, lq1, lk1, lq2, lk2, subln_g, weights, bsz, seq):
    _, n_tt, n_ft, _, _ = proj.shape
    m, d = n_tt * ST, n_ft * ST
    dv = d // N_HEADS
    dh = dv // 2
    nq = seq // ST
    hp = ST // dv
    ones_rows = 16
    tok_major = lambda s: pl.BlockSpec((None, nq, None, ST, dv),
                                       lambda h, b, s=s: (s, b, h // hp, 0, h % hp))
    feat_major = lambda s: pl.BlockSpec((None, nq, None, dv, ST),
                                        lambda h, b, s=s: (s, b, h // hp, h % hp, 0))
    vec = pl.BlockSpec((1, dh), lambda h, b: (0, 0))
    w_rows = d // (N_HEADS * bsz)
    assert w_rows * N_HEADS * bsz == d and w_rows % 16 == 0
    w_slice = pl.BlockSpec((w_rows, d), lambda h, b: (h * bsz + b, 0))
    return pl.pallas_call(
        functools.partial(_attn_kernel, dh=dh),
        grid=(N_HEADS, bsz),
        in_specs=[
            pl.BlockSpec(memory_space=pltpu.SMEM),
            vec, vec, vec, vec,
            tok_major(SLAB_Q), tok_major(SLAB_K), feat_major(SLAB_BV), feat_major(SLAB_BZ),
            pl.BlockSpec((dv, 1), lambda h, b: (0, 0)),
            w_slice, w_slice, w_slice,
        ],
        out_specs=[pl.BlockSpec((dv, seq), lambda h, b: (h, b)), w_slice, w_slice, w_slice],
        out_shape=[jax.ShapeDtypeStruct((d, m), BF16)] + [jax.ShapeDtypeStruct((d, d), BF16)] * 3,
        scratch_shapes=[
            pltpu.VMEM((2, ST, ST), F32),
            pltpu.VMEM((nq, dv + ones_rows, ST), BF16),
            pltpu.VMEM((2, 2, ST, ST), F32),
            pltpu.VMEM((2, 2, 1, ST), F32),
            pltpu.VMEM((2, dv + ones_rows, ST), F32),
        ],
        compiler_params=_params("arbitrary", "arbitrary"),
        name="attn",
    )(slopes, lq1, lk1, lq2, lk2, proj, proj, proj, proj, subln_g, *weights)


def _tail_kernel(u_ref, v_ref, z_ref, ga_ref, gb_ref, ybt_ref, x_ref, gate_ref, wa_ref, wb_ref,
                 wo_ref, lng_ref, lnb_ref, ws_ref, bs_ref, fg_ref, o_ref, ya_ref, b_ref, m_ref, *,
                 chunk, gd, nc):
    d = wb_ref.shape[1]
    cols = [slice(k, k + nc) for k in range(0, d, nc)]

    gm = _gmlp_pieces(u_ref, v_ref, z_ref, lng_ref, lnb_ref, ws_ref, bs_ref, ya_ref, chunk, gd)
    per = -(-len(gm) // len(cols))
    for k, cs in enumerate(cols):
        b_ref[:, cs] = lax.dot_general(ybt_ref[...], wb_ref[:, cs], (((0,), (0,)), ((), ())),
                                       preferred_element_type=F32)
        for step in gm[k * per:(k + 1) * per]:
            step()

    def merge(cs, a):
        f, off = divmod(cs.start, ST)
        ga = ga_ref[f, :, off:off + nc].astype(F32)
        gb = gb_ref[f, :, off:off + nc].astype(F32)
        m_ref[:, cs] = (ga * a + gb * b_ref[:, cs]).astype(m_ref.dtype)

    pending = None
    for cs in cols:
        a = jnp.dot(ya_ref[...], wa_ref[:, cs], preferred_element_type=F32)
        if pending is not None:
            merge(*pending)
        pending = (cs, a)
    merge(*pending)

    ssq = None
    for cs in cols:
        out = jnp.dot(m_ref[...], wo_ref[:, cs], preferred_element_type=F32)
        r = x_ref[:, cs] + gate_ref[:, cs] * out
        o_ref[:, cs] = r
        part = jnp.sum(r * r, axis=-1, keepdims=True)
        ssq = part if ssq is None else ssq + part
    rstd = lax.rsqrt(ssq * (1.0 / d) + EPS)
    o_ref[...] = (o_ref[...] * rstd) * fg_ref[...]


def _tail(proj, ybt, x2, mod3, wa, wb, wo, ln_g, ln_b, w_s, bias_full, fgain, seq, tm=256):
    _, n_tt, n_ft, _, _ = proj.shape
    m, d = x2.shape
    chunk = w_s.shape[-1]
    per = ST // tm
    tiles_per_seq = seq // tm
    pspec = lambda s: pl.BlockSpec((None, None, n_ft, tm, ST),
                                   lambda i, s=s: (s, i // per, 0, i % per, 0))
    rows = pl.BlockSpec((tm, d), lambda i: (i, 0))
    vec = pl.BlockSpec((1, d), lambda i: (0, 0))
    weight = pl.BlockSpec((d, d), lambda i: (0, 0), pipeline_mode=pl.Buffered(1))
    return pl.pallas_call(
        functools.partial(_tail_kernel, chunk=chunk, gd=d // N_GROUPS, nc=512),
        grid=(m // tm,),
        in_specs=[
            pspec(SLAB_U), pspec(SLAB_V), pspec(SLAB_Z), pspec(SLAB_GA), pspec(SLAB_GB),
            pl.BlockSpec((d, tm), lambda i: (0, i)), rows,
            pl.BlockSpec((None, 1, d), lambda i: (i // tiles_per_seq, 0, 2)),
            weight, weight, weight,
            vec, vec,
            pl.BlockSpec((N_GROUPS, chunk, chunk), lambda i: (0, 0, 0)),
            pl.BlockSpec((chunk, d), lambda i: (0, 0)),
            vec,
        ],
        out_specs=rows,
        out_shape=jax.ShapeDtypeStruct((m, d), F32),
        scratch_shapes=[pltpu.VMEM((tm, d), BF16), pltpu.VMEM((tm, d), F32),
                        pltpu.VMEM((tm, d), BF16)],
        compiler_params=_params("parallel"),
        name="tail",
    )(proj, proj, proj, proj, proj, ybt, x2, mod3, wa, wb, wo, ln_g, ln_b, w_s, bias_full, fgain)


def kernel(x, c, w_ada, b_ada, norm_gain, w_in, ln_v_gain, ln_v_bias, w_spatial, b_spatial,
           lambda_q1, lambda_k1, lambda_q2, lambda_k2, subln_gain, w_branch_a, w_branch_b,
           w_out, final_norm_gain):
    bsz, seq, d = x.shape
    depth = w_ada.shape[0]
    assert depth == 1, "single-layer trunk"
    dh = lambda_q1.shape[-1]
    assert d == N_HEADS * 2 * dh and w_in.shape[-1] == N_SLABS * d
    assert seq % ST == 0 and d % ST == 0 and ST % (2 * dh) == 0
    l = 0

    x2 = x.reshape(bsz * seq, d)
    mod = _ada(c, w_ada[l], b_ada[l])
    mod3 = mod.reshape(bsz, 1, 3 * d)

    proj = _inproj(x2, mod3, norm_gain[l].reshape(1, d), w_in[l],
                   LOG2E / math.sqrt(dh), seq)

    bias_full = jnp.repeat(jnp.transpose(b_spatial[l]), d // N_GROUPS, axis=1)

    slopes = jnp.asarray([2.0 ** (-8.0 * (i + 1) / N_HEADS) for i in range(N_HEADS)], F32)
    ybt, wa, wb, wo = _attn(proj, slopes, lambda_q1[l].reshape(1, dh), lambda_k1[l].reshape(1, dh),
                            lambda_q2[l].reshape(1, dh), lambda_k2[l].reshape(1, dh),
                            subln_gain[l].reshape(2 * dh, 1),
                            (w_branch_a[l], w_branch_b[l], w_out[l]), bsz, seq)

    out = _tail(proj, ybt, x2, mod3, wa, wb, wo, ln_v_gain[l].reshape(1, d),
                ln_v_bias[l].reshape(1, d),
                w_spatial[l], bias_full, final_norm_gain.reshape(1, d), seq)
    return out.reshape(bsz, seq, d)
```

```python
import functools
import math

import jax
import jax.numpy as jnp
from jax import lax
from jax.experimental import pallas as pl
from jax.experimental.pallas import tpu as pltpu

F32 = jnp.float32
BF16 = jnp.bfloat16

N_HEADS = 8
N_GROUPS = 8
N_SLABS = 9
SLAB_U, SLAB_V, SLAB_Z, SLAB_Q, SLAB_K, SLAB_BV, SLAB_BZ, SLAB_GA, SLAB_GB = range(N_SLABS)
ST = 512
EPS = 1e-6
SUBLN_EPS = 1e-5
LAMBDA_INIT = 0.8 - 0.6 * math.exp(-0.3 * 0)
LOG2E = 1.0 / math.log(2.0)

V7X_VMEM_BYTES = 64 * 1024 * 1024
VMEM_LIMIT = V7X_VMEM_BYTES * 7 // 8


def _sigmoid(x):
    return 0.5 * jnp.tanh(0.5 * x) + 0.5


def _silu(x):
    hx = 0.5 * x
    return hx * jnp.tanh(hx) + hx


def _gelu_tanh(x):
    c = math.sqrt(2.0 / math.pi)
    hx = 0.5 * x
    return hx * jnp.tanh(x * ((0.044715 * c) * (x * x) + c)) + hx


def _params(*sem):
    return pltpu.CompilerParams(dimension_semantics=sem, vmem_limit_bytes=VMEM_LIMIT)


def _ada_kernel(c_ref, w_ref, b_ref, o_ref):
    c = c_ref[...]
    o_ref[...] = jnp.dot(_silu(c), w_ref[...], preferred_element_type=F32) + b_ref[...]


def _ada(c, w_ada, b_ada, tn=1024):
    bsz, d = c.shape
    n = w_ada.shape[1]
    return pl.pallas_call(
        _ada_kernel,
        grid=(n // tn,),
        in_specs=[
            pl.BlockSpec((bsz, d), lambda j: (0, 0)),
            pl.BlockSpec((d, tn), lambda j: (0, j)),
            pl.BlockSpec((1, tn), lambda j: (0, j)),
        ],
        out_specs=pl.BlockSpec((bsz, tn), lambda j: (0, j)),
        out_shape=jax.ShapeDtypeStruct((bsz, n), F32),
        compiler_params=_params("parallel"),
        name="ada",
    )(c, w_ada, b_ada.reshape(1, n))


def _inproj_kernel(x_ref, g_ref, shift_ref, scale_ref, w_ref, o_ref, h_ref, *,
                   rows, per, q_scale):
    j = pl.program_id(1)

    @pl.when(j == 0)
    def _():
        gsc = g_ref[...] * (1.0 + scale_ref[...])
        sh = shift_ref[...]

        def body(r, carry):
            sl = pl.ds(pl.multiple_of(r * rows, rows), rows)
            x = x_ref[sl, :]
            rstd = lax.rsqrt(jnp.mean(x * x, axis=-1, keepdims=True) + EPS)
            h_ref[sl, :] = ((x * rstd) * gsc + sh).astype(BF16)
            return carry

        lax.fori_loop(0, x_ref.shape[0] // rows, body, 0, unroll=4)

    n_tok, n_feat = o_ref.shape[0], o_ref.shape[1]
    slab = j // per
    epilogues = (
        ((SLAB_U, SLAB_V), _gelu_tanh),
        ((SLAB_Z,), _silu),
        ((SLAB_GA, SLAB_GB), _sigmoid),
        ((SLAB_Q,), lambda acc: acc * q_scale),
        ((SLAB_K,), lambda acc: acc),
    )
    for slabs, act in epilogues:
        @pl.when(functools.reduce(jnp.logical_or, [slab == s for s in slabs]))
        def _(act=act):
            acc = act(jnp.dot(h_ref[...], w_ref[...].astype(BF16), preferred_element_type=F32))
            for a in range(n_tok):
                for b in range(n_feat):
                    o_ref[a, b] = acc[a * ST:(a + 1) * ST, b * ST:(b + 1) * ST].astype(o_ref.dtype)

    for s_t, act in ((SLAB_BV, lambda acc: acc), (SLAB_BZ, _silu)):
        @pl.when(slab == s_t)
        def _(act=act):
            acc_t = act(lax.dot_general(w_ref[...].astype(BF16), h_ref[...], (((0,), (1,)), ((), ())),
                                        preferred_element_type=F32))
            for a in range(n_tok):
                for b in range(n_feat):
                    o_ref[a, b] = acc_t[b * ST:(b + 1) * ST, a * ST:(a + 1) * ST].astype(o_ref.dtype)


def _inproj(x2, mod3, gain, w_in, q_scale, seq, tm=1024, tn=1024):
    m, d = x2.shape
    n = w_in.shape[1]
    slab = n // N_SLABS
    per = slab // tn
    tiles_per_seq = seq // tm
    return pl.pallas_call(
        functools.partial(_inproj_kernel, rows=32, per=per, q_scale=q_scale),
        grid=(m // tm, n // tn),
        in_specs=[
            pl.BlockSpec((tm, d), lambda i, j: (i, 0)),
            pl.BlockSpec((1, d), lambda i, j: (0, 0)),
            pl.BlockSpec((None, 1, d), lambda i, j: (i // tiles_per_seq, 0, 0)),
            pl.BlockSpec((None, 1, d), lambda i, j: (i // tiles_per_seq, 0, 1)),
            pl.BlockSpec((d, tn), lambda i, j: (0, j)),
        ],
        out_specs=pl.BlockSpec((None, tm // ST, tn // ST, ST, ST),
                               lambda i, j: (j // per, i, j % per, 0, 0)),
        out_shape=jax.ShapeDtypeStruct((N_SLABS, m // ST, slab // ST, ST, ST), BF16),
        scratch_shapes=[pltpu.VMEM((tm, d), BF16)],
        compiler_params=_params("parallel", "arbitrary"),
        name="inproj",
    )(x2, gain, mod3, mod3, w_in)


def _gmlp_pieces(u_ref, v_ref, z_ref, lng_ref, lnb_ref, ws_ref, bs_ref, o_ref, chunk, gd):
    n_ft, ta, _ = u_ref.shape
    d = n_ft * ST
    state = {}

    def layernorm(c):
        rows = pl.ds(c * chunk, chunk)
        gv = [v_ref[f, rows, :].astype(F32) for f in range(n_ft)]
        mu = sum(jnp.sum(t, axis=-1, keepdims=True) for t in gv) * (1.0 / d)
        xc = [t - mu for t in gv]
        var = sum(jnp.sum(t * t, axis=-1, keepdims=True) for t in xc) * (1.0 / d)
        rstd = lax.rsqrt(var + EPS)
        state[c] = [((xc[f] * rstd) * lng_ref[:, f * ST:(f + 1) * ST]
                     + lnb_ref[:, f * ST:(f + 1) * ST]).astype(BF16) for f in range(n_ft)]

    def group(c, g):
        rows = pl.ds(c * chunk, chunk)
        row = lax.broadcasted_iota(jnp.int32, (chunk, chunk), 0)
        col = lax.broadcasted_iota(jnp.int32, (chunk, chunk), 1)
        f, off = divmod(g * gd, ST)
        ws = (ws_ref[g] * (row >= col).astype(F32)).astype(BF16)
        sv = jnp.dot(ws, state[c][f][:, off:off + gd], preferred_element_type=F32)
        sv = sv + bs_ref[:, g * gd:(g + 1) * gd]
        u = u_ref[f, rows, off:off + gd].astype(F32)
        z = z_ref[f, rows, off:off + gd].astype(F32)
        o_ref[rows, g * gd:(g + 1) * gd] = ((u * sv) * z).astype(o_ref.dtype)

    steps = []
    for c in range(ta // chunk):
        steps.append(functools.partial(layernorm, c))
        steps += [functools.partial(group, c, g) for g in range(N_GROUPS)]
    return steps


def _attn_kernel(slopes_ref, lq1_ref, lk1_ref, lq2_ref, lk2_ref, q_ref, k_ref, vt_ref, zt_ref,
                 sg_ref, wa_ref, wb_ref, wo_ref, o_ref, wa_bf_ref, wb_bf_ref, wo_bf_ref,
                 bias_ref, vta_ref, s_ref, smax_ref, acc_ref, *, dh):
    nq, blk, dv = q_ref.shape
    for src, dst in ((wa_ref, wa_bf_ref), (wb_ref, wb_bf_ref), (wo_ref, wo_bf_ref)):
        dst[...] = src[...].astype(dst.dtype)
    slope = slopes_ref[pl.program_id(0)] * LOG2E

    @pl.when(pl.program_id(1) == 0)
    def _():
        s_pos = lax.broadcasted_iota(jnp.int32, (blk, blk), 0)
        t_pos = lax.broadcasted_iota(jnp.int32, (blk, blk), 1)
        bias = slope * (s_pos - t_pos).astype(F32)
        bias_ref[0] = bias
        bias_ref[1] = jnp.where(s_pos <= t_pos, bias, -jnp.inf)

    for j in range(nq):
        vta_ref[j, :dv, :] = vt_ref[j]
        vta_ref[j, dv:, :] = jnp.ones((vta_ref.shape[1] - dv, blk), vta_ref.dtype)

    lam_init = jnp.float32(LAMBDA_INIT)
    lam = (jnp.exp(jnp.sum(lq1_ref[...] * lk1_ref[...], axis=-1, keepdims=True))
           - jnp.exp(jnp.sum(lq2_ref[...] * lk2_ref[...], axis=-1, keepdims=True))
           + lam_init)
    gain = sg_ref[...] * (1.0 - lam_init)

    def scores(j, n):
        slot = j % 2
        bias = bias_ref[int(j == n)]
        for c in range(2):
            q = q_ref[n, :, c * dh:(c + 1) * dh]
            k = k_ref[j, :, c * dh:(c + 1) * dh]
            s = lax.dot_general(k, q, (((1,), (1,)), ((), ())), preferred_element_type=F32)
            s = s + bias
            s_ref[slot, c] = s
            smax_ref[slot, c] = jnp.max(s, axis=0, keepdims=True)

    def consume(j, n, m):
        slot = j % 2
        vta = vta_ref[j]
        off = slope * float((j - n) * blk)
        for c in range(2):
            m_new = smax_ref[slot, c] + off
            if m[c] is not None:
                m_new = jnp.maximum(m[c], m_new)
            p = jnp.exp2(s_ref[slot, c] - (m_new - off)).astype(vta.dtype)
            pv = jnp.dot(vta, p, preferred_element_type=F32)
            if m[c] is None:
                acc_ref[c] = pv
            else:
                acc_ref[c] = jnp.exp2(m[c] - m_new) * acc_ref[c] + pv
            m[c] = m_new

    for n in range(nq):
        m = [None, None]
        scores(0, n)
        for j in range(n):
            scores(j + 1, n)
            consume(j, n, m)
        consume(n, n, m)

        a0, a1 = acc_ref[0], acc_ref[1]
        r0 = 1.0 / a0[dv:dv + 1]
        r1 = lam * (1.0 / a1[dv:dv + 1])
        o = a0[:dv] * r0 - a1[:dv] * r1
        o = o * lax.rsqrt(jnp.mean(o * o, axis=0, keepdims=True) + SUBLN_EPS)
        o = (o * gain) * zt_ref[n].astype(F32)
        o_ref[:, n * blk:(n + 1) * blk] = o.astype(o_ref.dtype)


def _attn(proj, slopes, lq1, lk1, lq2, lk2, subln_g, weights, bsz, seq):
    _, n_tt, n_ft, _, _ = proj.shape
    m, d = n_tt * ST, n_ft * ST
    dv = d // N_HEADS
    dh = dv // 2
    nq = seq // ST
    hp = ST // dv
    ones_rows = 16
    tok_major = lambda s: pl.BlockSpec((None, nq, None, ST, dv),
                                       lambda h, b, s=s: (s, b, h // hp, 0, h % hp))
    feat_major = lambda s: pl.BlockSpec((None, nq, None, dv, ST),
                                        lambda h, b, s=s: (s, b, h // hp, h % hp, 0))
    vec = pl.BlockSpec((1, dh), lambda h, b: (0, 0))
    w_rows = d // (N_HEADS * bsz)
    assert w_rows * N_HEADS * bsz == d and w_rows % 16 == 0
    w_slice = pl.BlockSpec((w_rows, d), lambda h, b: (h * bsz + b, 0))
    return pl.pallas_call(
        functools.partial(_attn_kernel, dh=dh),
        grid=(N_HEADS, bsz),
        in_specs=[
            pl.BlockSpec(memory_space=pltpu.SMEM),
            vec, vec, vec, vec,
            tok_major(SLAB_Q), tok_major(SLAB_K), feat_major(SLAB_BV), feat_major(SLAB_BZ),
            pl.BlockSpec((dv, 1), lambda h, b: (0, 0)),
            w_slice, w_slice, w_slice,
        ],
        out_specs=[pl.BlockSpec((dv, seq), lambda h, b: (h, b)), w_slice, w_slice, w_slice],
        out_shape=[jax.ShapeDtypeStruct((d, m), BF16)] + [jax.ShapeDtypeStruct((d, d), BF16)] * 3,
        scratch_shapes=[
            pltpu.VMEM((2, ST, ST), F32),
            pltpu.VMEM((nq, dv + ones_rows, ST), BF16),
            pltpu.VMEM((2, 2, ST, ST), F32),
            pltpu.VMEM((2, 2, 1, ST), F32),
            pltpu.VMEM((2, dv + ones_rows, ST), F32),
        ],
        compiler_params=_params("arbitrary", "arbitrary"),
        name="attn",
    )(slopes, lq1, lk1, lq2, lk2, proj, proj, proj, proj, subln_g, *weights)


def _tail_kernel(u_ref, v_ref, z_ref, ga_ref, gb_ref, ybt_ref, x_ref, gate_ref, wa_ref, wb_ref,
                 wo_ref, lng_ref, lnb_ref, ws_ref, bs_ref, fg_ref, o_ref, ya_ref, b_ref, m_ref, *,
                 chunk, gd, nc):
    d = wb_ref.shape[1]
    cols = [slice(k, k + nc) for k in range(0, d, nc)]

    gm = _gmlp_pieces(u_ref, v_ref, z_ref, lng_ref, lnb_ref, ws_ref, bs_ref, ya_ref, chunk, gd)
    per = -(-len(gm) // len(cols))
    for k, cs in enumerate(cols):
        b_ref[:, cs] = lax.dot_general(ybt_ref[...], wb_ref[:, cs], (((0,), (0,)), ((), ())),
                                       preferred_element_type=F32)
        for step in gm[k * per:(k + 1) * per]:
            step()

    def merge(cs, a):
        f, off = divmod(cs.start, ST)
        ga = ga_ref[f, :, off:off + nc].astype(F32)
        gb = gb_ref[f, :, off:off + nc].astype(F32)
        m_ref[:, cs] = (ga * a + gb * b_ref[:, cs]).astype(m_ref.dtype)

    pending = None
    for cs in cols:
        a = jnp.dot(ya_ref[...], wa_ref[:, cs], preferred_element_type=F32)
        if pending is not None:
            merge(*pending)
        pending = (cs, a)
    merge(*pending)

    ssq = None
    for cs in cols:
        out = jnp.dot(m_ref[...], wo_ref[:, cs], preferred_element_type=F32)
        r = x_ref[:, cs] + gate_ref[:, cs] * out
        o_ref[:, cs] = r
        part = jnp.sum(r * r, axis=-1, keepdims=True)
        ssq = part if ssq is None else ssq + part
    rstd = lax.rsqrt(ssq * (1.0 / d) + EPS)
    o_ref[...] = (o_ref[...] * rstd) * fg_ref[...]


def _tail(proj, ybt, x2, mod3, wa, wb, wo, ln_g, ln_b, w_s, bias_full, fgain, seq, tm=256):
    _, n_tt, n_ft, _, _ = proj.shape
    m, d = x2.shape
    chunk = w_s.shape[-1]
    assert ST % tm == 0 and tm % chunk == 0 and seq % tm == 0
    per = ST // tm
    tiles_per_seq = seq // tm
    pspec = lambda s: pl.BlockSpec((None, None, n_ft, tm, ST),
                                   lambda i, s=s: (s, i // per, 0, i % per, 0))
    rows = pl.BlockSpec((tm, d), lambda i: (i, 0))
    vec = pl.BlockSpec((1, d), lambda i: (0, 0))
    weight = pl.BlockSpec((d, d), lambda i: (0, 0), pipeline_mode=pl.Buffered(1))
    return pl.pallas_call(
        functools.partial(_tail_kernel, chunk=chunk, gd=d // N_GROUPS, nc=512),
        grid=(m // tm,),
        in_specs=[
            pspec(SLAB_U), pspec(SLAB_V), pspec(SLAB_Z), pspec(SLAB_GA), pspec(SLAB_GB),
            pl.BlockSpec((d, tm), lambda i: (0, i)), rows,
            pl.BlockSpec((None, 1, d), lambda i: (i // tiles_per_seq, 0, 2)),
            weight, weight, weight,
            vec, vec,
            pl.BlockSpec((N_GROUPS, chunk, chunk), lambda i: (0, 0, 0)),
            pl.BlockSpec((chunk, d), lambda i: (0, 0)),
            vec,
        ],
        out_specs=rows,
        out_shape=jax.ShapeDtypeStruct((m, d), F32),
        scratch_shapes=[pltpu.VMEM((tm, d), BF16), pltpu.VMEM((tm, d), F32),
                        pltpu.VMEM((tm, d), BF16)],
        compiler_params=_params("parallel"),
        name="tail",
    )(proj, proj, proj, proj, proj, ybt, x2, mod3, wa, wb, wo, ln_g, ln_b, w_s, bias_full, fgain)


def kernel(x, c, w_ada, b_ada, norm_gain, w_in, ln_v_gain, ln_v_bias, w_spatial, b_spatial,
           lambda_q1, lambda_k1, lambda_q2, lambda_k2, subln_gain, w_branch_a, w_branch_b,
           w_out, final_norm_gain):
    bsz, seq, d = x.shape
    depth = w_ada.shape[0]
    assert depth == 1, "single-layer trunk"
    dh = lambda_q1.shape[-1]
    assert d == N_HEADS * 2 * dh and w_in.shape[-1] == N_SLABS * d
    assert seq % ST == 0 and d % ST == 0 and ST % (2 * dh) == 0
    l = 0

    x2 = x.reshape(bsz * seq, d)
    mod = _ada(c, w_ada[l], b_ada[l])
    mod3 = mod.reshape(bsz, 1, 3 * d)

    proj = _inproj(x2, mod3, norm_gain[l].reshape(1, d), w_in[l],
                   LOG2E / math.sqrt(dh), seq)

    bias_full = jnp.repeat(jnp.transpose(b_spatial[l]), d // N_GROUPS, axis=1)

    slopes = jnp.asarray([2.0 ** (-8.0 * (i + 1) / N_HEADS) for i in range(N_HEADS)], F32)
    ybt, wa, wb, wo = _attn(proj, slopes, lambda_q1[l].reshape(1, dh), lambda_k1[l].reshape(1, dh),
                            lambda_q2[l].reshape(1, dh), lambda_k2[l].reshape(1, dh),
                            subln_gain[l].reshape(2 * dh, 1),
                            (w_branch_a[l], w_branch_b[l], w_out[l]), bsz, seq)

    out = _tail(proj, ybt, x2, mod3, wa, wb, wo, ln_v_gain[l].reshape(1, d),
                ln_v_bias[l].reshape(1, d),
                w_spatial[l], bias_full, final_norm_gain.reshape(1, d), seq)
    return out.reshape(bsz, seq, d)
```

```python
import functools
import math

import jax
import jax.numpy as jnp
from jax import lax
from jax.experimental import pallas as pl
from jax.experimental.pallas import tpu as pltpu

F32 = jnp.float32
BF16 = jnp.bfloat16

N_HEADS = 8
N_GROUPS = 8
N_SLABS = 9
SLAB_U, SLAB_V, SLAB_Z, SLAB_Q, SLAB_K, SLAB_BV, SLAB_BZ, SLAB_GA, SLAB_GB = range(N_SLABS)
ST = 512
EPS = 1e-6
SUBLN_EPS = 1e-5
LAMBDA_INIT = 0.8 - 0.6 * math.exp(-0.3 * 0)
LOG2E = 1.0 / math.log(2.0)

V7X_VMEM_BYTES = 64 * 1024 * 1024
VMEM_LIMIT = V7X_VMEM_BYTES * 7 // 8


def _sigmoid(x):
    return 0.5 * jnp.tanh(0.5 * x) + 0.5


def _silu(x):
    hx = 0.5 * x
    return hx * jnp.tanh(hx) + hx


def _gelu_tanh(x):
    c = math.sqrt(2.0 / math.pi)
    hx = 0.5 * x
    return hx * jnp.tanh(x * ((0.044715 * c) * (x * x) + c)) + hx


def _params(*sem):
    return pltpu.CompilerParams(dimension_semantics=sem, vmem_limit_bytes=VMEM_LIMIT)


def _ada_kernel(c_ref, w_ref, b_ref, o_ref):
    c = c_ref[...]
    o_ref[...] = jnp.dot(_silu(c), w_ref[...], preferred_element_type=F32) + b_ref[...]


def _ada(c, w_ada, b_ada, tn=1024):
    bsz, d = c.shape
    n = w_ada.shape[1]
    return pl.pallas_call(
        _ada_kernel,
        grid=(n // tn,),
        in_specs=[
            pl.BlockSpec((bsz, d), lambda j: (0, 0)),
            pl.BlockSpec((d, tn), lambda j: (0, j)),
            pl.BlockSpec((1, tn), lambda j: (0, j)),
        ],
        out_specs=pl.BlockSpec((bsz, tn), lambda j: (0, j)),
        out_shape=jax.ShapeDtypeStruct((bsz, n), F32),
        compiler_params=_params("parallel"),
        name="ada",
    )(c, w_ada, b_ada.reshape(1, n))


def _inproj_kernel(*refs, nx, rows, per, q_scale):
    x_refs = refs[:nx]
    g_ref, shift_ref, scale_ref, w_ref, o_ref, h_ref = refs[nx:]
    j = pl.program_id(1)

    @pl.when(j == 0)
    def _():
        gsc = g_ref[...] * (1.0 + scale_ref[...])
        sh = shift_ref[...]
        for k, x_ref in enumerate(x_refs):
            base = k * x_ref.shape[0]

            def body(r, carry, x_ref=x_ref, base=base):
                sl = pl.ds(pl.multiple_of(r * rows, rows), rows)
                x = x_ref[sl, :]
                rstd = lax.rsqrt(jnp.mean(x * x, axis=-1, keepdims=True) + EPS)
                h_ref[pl.ds(pl.multiple_of(base + r * rows, rows), rows), :] = (
                    ((x * rstd) * gsc + sh).astype(BF16))
                return carry

            lax.fori_loop(0, x_ref.shape[0] // rows, body, 0, unroll=4)

    n_tok, n_feat = o_ref.shape[0], o_ref.shape[1]
    slab = j // per
    epilogues = (
        ((SLAB_U, SLAB_V), _gelu_tanh),
        ((SLAB_Z,), _silu),
        ((SLAB_GA, SLAB_GB), _sigmoid),
        ((SLAB_Q,), lambda acc: acc * q_scale),
        ((SLAB_K,), lambda acc: acc),
    )
    for slabs, act in epilogues:
        @pl.when(functools.reduce(jnp.logical_or, [slab == s for s in slabs]))
        def _(act=act):
            acc = act(jnp.dot(h_ref[...], w_ref[...].astype(BF16), preferred_element_type=F32))
            for a in range(n_tok):
                for b in range(n_feat):
                    o_ref[a, b] = acc[a * ST:(a + 1) * ST, b * ST:(b + 1) * ST].astype(o_ref.dtype)

    for s_t, act in ((SLAB_BV, lambda acc: acc), (SLAB_BZ, _silu)):
        @pl.when(slab == s_t)
        def _(act=act):
            acc_t = act(lax.dot_general(w_ref[...].astype(BF16), h_ref[...], (((0,), (1,)), ((), ())),
                                        preferred_element_type=F32))
            for a in range(n_tok):
                for b in range(n_feat):
                    o_ref[a, b] = acc_t[b * ST:(b + 1) * ST, a * ST:(a + 1) * ST].astype(o_ref.dtype)


def _inproj(x2, mod3, gain, w_in, q_scale, seq, tm=1024, tn=1024):
    m, d = x2.shape
    n = w_in.shape[1]
    slab = n // N_SLABS
    per = slab // tn
    tiles_per_seq = seq // tm
    ni, nj = m // tm, n // tn
    nx = 4
    switch = [nj // 2 + 2 * k for k in range(nx)]
    assert tm % (nx * 32) == 0 and switch[-1] < nj

    def x_slice(k):
        return pl.BlockSpec(
            (tm // nx, d),
            lambda i, j: (jnp.minimum(i + (j >= switch[k]).astype(jnp.int32), ni - 1) * nx + k, 0))

    return pl.pallas_call(
        functools.partial(_inproj_kernel, nx=nx, rows=32, per=per, q_scale=q_scale),
        grid=(ni, nj),
        in_specs=[x_slice(k) for k in range(nx)] + [
            pl.BlockSpec((1, d), lambda i, j: (0, 0)),
            pl.BlockSpec((None, 1, d), lambda i, j: (i // tiles_per_seq, 0, 0)),
            pl.BlockSpec((None, 1, d), lambda i, j: (i // tiles_per_seq, 0, 1)),
            pl.BlockSpec((d, tn), lambda i, j: (0, j)),
        ],
        out_specs=pl.BlockSpec((None, tm // ST, tn // ST, ST, ST),
                               lambda i, j: (j // per, i, j % per, 0, 0)),
        out_shape=jax.ShapeDtypeStruct((N_SLABS, m // ST, slab // ST, ST, ST), BF16),
        scratch_shapes=[pltpu.VMEM((tm, d), BF16)],
        compiler_params=_params("parallel", "arbitrary"),
        name="inproj",
    )(*([x2] * nx), gain, mod3, mod3, w_in)


def _gmlp_pieces(u_ref, v_ref, z_ref, lng_ref, lnb_ref, ws_ref, bs_ref, o_ref, chunk, gd):
    n_ft, ta, _ = u_ref.shape
    d = n_ft * ST
    state = {}

    def layernorm(c):
        rows = pl.ds(c * chunk, chunk)
        gv = [v_ref[f, rows, :].astype(F32) for f in range(n_ft)]
        mu = sum(jnp.sum(t, axis=-1, keepdims=True) for t in gv) * (1.0 / d)
        xc = [t - mu for t in gv]
        var = sum(jnp.sum(t * t, axis=-1, keepdims=True) for t in xc) * (1.0 / d)
        rstd = lax.rsqrt(var + EPS)
        state[c] = [((xc[f] * rstd) * lng_ref[:, f * ST:(f + 1) * ST]
                     + lnb_ref[:, f * ST:(f + 1) * ST]).astype(BF16) for f in range(n_ft)]

    def group(c, g):
        rows = pl.ds(c * chunk, chunk)
        row = lax.broadcasted_iota(jnp.int32, (chunk, chunk), 0)
        col = lax.broadcasted_iota(jnp.int32, (chunk, chunk), 1)
        f, off = divmod(g * gd, ST)
        ws = (ws_ref[g] * (row >= col).astype(F32)).astype(BF16)
        sv = jnp.dot(ws, state[c][f][:, off:off + gd], preferred_element_type=F32)
        sv = sv + bs_ref[:, g * gd:(g + 1) * gd]
        u = u_ref[f, rows, off:off + gd].astype(F32)
        z = z_ref[f, rows, off:off + gd].astype(F32)
        o_ref[rows, g * gd:(g + 1) * gd] = ((u * sv) * z).astype(o_ref.dtype)

    steps = []
    for c in range(ta // chunk):
        steps.append(functools.partial(layernorm, c))
        steps += [functools.partial(group, c, g) for g in range(N_GROUPS)]
    return steps


def _attn_kernel(slopes_ref, lq1_ref, lk1_ref, lq2_ref, lk2_ref, q_ref, k_ref, vt_ref, zt_ref,
                 sg_ref, wa_ref, wb_ref, wo_ref, o_ref, wa_bf_ref, wb_bf_ref, wo_bf_ref,
                 bias_ref, vta_ref, s_ref, smax_ref, acc_ref, *, dh):
    nq, blk, dv = q_ref.shape
    for src, dst in ((wa_ref, wa_bf_ref), (wb_ref, wb_bf_ref), (wo_ref, wo_bf_ref)):
        dst[...] = src[...].astype(dst.dtype)
    slope = slopes_ref[pl.program_id(0)] * LOG2E

    @pl.when(pl.program_id(1) == 0)
    def _():
        s_pos = lax.broadcasted_iota(jnp.int32, (blk, blk), 0)
        t_pos = lax.broadcasted_iota(jnp.int32, (blk, blk), 1)
        bias = slope * (s_pos - t_pos).astype(F32)
        bias_ref[0] = bias
        bias_ref[1] = jnp.where(s_pos <= t_pos, bias, -jnp.inf)

    for j in range(nq):
        vta_ref[j, :dv, :] = vt_ref[j]
        vta_ref[j, dv:, :] = jnp.ones((vta_ref.shape[1] - dv, blk), vta_ref.dtype)

    lam_init = jnp.float32(LAMBDA_INIT)
    lam = (jnp.exp(jnp.sum(lq1_ref[...] * lk1_ref[...], axis=-1, keepdims=True))
           - jnp.exp(jnp.sum(lq2_ref[...] * lk2_ref[...], axis=-1, keepdims=True))
           + lam_init)
    gain = sg_ref[...] * (1.0 - lam_init)

    def scores(j, n):
        slot = j % 2
        bias = bias_ref[int(j == n)]
        for c in range(2):
            q = q_ref[n, :, c * dh:(c + 1) * dh]
            k = k_ref[j, :, c * dh:(c + 1) * dh]
            s = lax.dot_general(k, q, (((1,), (1,)), ((), ())), preferred_element_type=F32)
            s = s + bias
            s_ref[slot, c] = s
            smax_ref[slot, c] = jnp.max(s, axis=0, keepdims=True)

    def consume(j, n, m):
        slot = j % 2
        vta = vta_ref[j]
        off = slope * float((j - n) * blk)
        for c in range(2):
            m_new = smax_ref[slot, c] + off
            if m[c] is not None:
                m_new = jnp.maximum(m[c], m_new)
            p = jnp.exp2(s_ref[slot, c] - (m_new - off)).astype(vta.dtype)
            pv = jnp.dot(vta, p, preferred_element_type=F32)
            if m[c] is None:
                acc_ref[c] = pv
            else:
                acc_ref[c] = jnp.exp2(m[c] - m_new) * acc_ref[c] + pv
            m[c] = m_new

    for n in range(nq):
        m = [None, None]
        scores(0, n)
        for j in range(n):
            scores(j + 1, n)
            consume(j, n, m)
        consume(n, n, m)

        a0, a1 = acc_ref[0], acc_ref[1]
        r0 = 1.0 / a0[dv:dv + 1]
        r1 = lam * (1.0 / a1[dv:dv + 1])
        o = a0[:dv] * r0 - a1[:dv] * r1
        o = o * lax.rsqrt(jnp.mean(o * o, axis=0, keepdims=True) + SUBLN_EPS)
        o = (o * gain) * zt_ref[n].astype(F32)
        o_ref[:, n * blk:(n + 1) * blk] = o.astype(o_ref.dtype)


def _attn(proj, slopes, lq1, lk1, lq2, lk2, subln_g, weights, bsz, seq):
    _, n_tt, n_ft, _, _ = proj.shape
    m, d = n_tt * ST, n_ft * ST
    dv = d // N_HEADS
    dh = dv // 2
    nq = seq // ST
    hp = ST // dv
    ones_rows = 16
    tok_major = lambda s: pl.BlockSpec((None, nq, None, ST, dv),
                                       lambda h, b, s=s: (s, b, h // hp, 0, h % hp))
    feat_major = lambda s: pl.BlockSpec((None, nq, None, dv, ST),
                                        lambda h, b, s=s: (s, b, h // hp, h % hp, 0))
    vec = pl.BlockSpec((1, dh), lambda h, b: (0, 0))
    w_rows = d // (N_HEADS * bsz)
    assert w_rows * N_HEADS * bsz == d and w_rows % 16 == 0
    w_slice = pl.BlockSpec((w_rows, d), lambda h, b: (h * bsz + b, 0))
    return pl.pallas_call(
        functools.partial(_attn_kernel, dh=dh),
        grid=(N_HEADS, bsz),
        in_specs=[
            pl.BlockSpec(memory_space=pltpu.SMEM),
            vec, vec, vec, vec,
            tok_major(SLAB_Q), tok_major(SLAB_K), feat_major(SLAB_BV), feat_major(SLAB_BZ),
            pl.BlockSpec((dv, 1), lambda h, b: (0, 0)),
            w_slice, w_slice, w_slice,
        ],
        out_specs=[pl.BlockSpec((dv, seq), lambda h, b: (h, b)), w_slice, w_slice, w_slice],
        out_shape=[jax.ShapeDtypeStruct((d, m), BF16)] + [jax.ShapeDtypeStruct((d, d), BF16)] * 3,
        scratch_shapes=[
            pltpu.VMEM((2, ST, ST), F32),
            pltpu.VMEM((nq, dv + ones_rows, ST), BF16),
            pltpu.VMEM((2, 2, ST, ST), F32),
            pltpu.VMEM((2, 2, 1, ST), F32),
            pltpu.VMEM((2, dv + ones_rows, ST), F32),
        ],
        compiler_params=_params("arbitrary", "arbitrary"),
        name="attn",
    )(slopes, lq1, lk1, lq2, lk2, proj, proj, proj, proj, subln_g, *weights)


def _tail_kernel(u_ref, v_ref, z_ref, ga_ref, gb_ref, ybt_ref, x_ref, gate_ref, wa_ref, wb_ref,
                 wo_ref, lng_ref, lnb_ref, ws_ref, bs_ref, fg_ref, o_ref, ya_ref, b_ref, m_ref, *,
                 chunk, gd, nc):
    d = wb_ref.shape[1]
    cols = [slice(k, k + nc) for k in range(0, d, nc)]

    gm = _gmlp_pieces(u_ref, v_ref, z_ref, lng_ref, lnb_ref, ws_ref, bs_ref, ya_ref, chunk, gd)
    per = -(-len(gm) // len(cols))
    for k, cs in enumerate(cols):
        b_ref[:, cs] = lax.dot_general(ybt_ref[...], wb_ref[:, cs], (((0,), (0,)), ((), ())),
                                       preferred_element_type=F32)
        for step in gm[k * per:(k + 1) * per]:
            step()

    def merge(cs, a):
        f, off = divmod(cs.start, ST)
        ga = ga_ref[f, :, off:off + nc].astype(F32)
        gb = gb_ref[f, :, off:off + nc].astype(F32)
        m_ref[:, cs] = (ga * a + gb * b_ref[:, cs]).astype(m_ref.dtype)

    pending = None
    for cs in cols:
        a = jnp.dot(ya_ref[...], wa_ref[:, cs], preferred_element_type=F32)
        if pending is not None:
            merge(*pending)
        pending = (cs, a)
    merge(*pending)

    ssq = None
    for cs in cols:
        out = jnp.dot(m_ref[...], wo_ref[:, cs], preferred_element_type=F32)
        r = x_ref[:, cs] + gate_ref[:, cs] * out
        o_ref[:, cs] = r
        part = jnp.sum(r * r, axis=-1, keepdims=True)
        ssq = part if ssq is None else ssq + part
    rstd = lax.rsqrt(ssq * (1.0 / d) + EPS)
    o_ref[...] = (o_ref[...] * rstd) * fg_ref[...]


def _tail(proj, ybt, x2, mod3, wa, wb, wo, ln_g, ln_b, w_s, bias_full, fgain, seq, tm=256):
    _, n_tt, n_ft, _, _ = proj.shape
    m, d = x2.shape
    chunk = w_s.shape[-1]
    assert ST % tm == 0 and tm % chunk == 0 and seq % tm == 0
    per = ST // tm
    tiles_per_seq = seq // tm
    pspec = lambda s: pl.BlockSpec((None, None, n_ft, tm, ST),
                                   lambda i, s=s: (s, i // per, 0, i % per, 0))
    rows = pl.BlockSpec((tm, d), lambda i: (i, 0))
    vec = pl.BlockSpec((1, d), lambda i: (0, 0))
    weight = pl.BlockSpec((d, d), lambda i: (0, 0), pipeline_mode=pl.Buffered(1))
    return pl.pallas_call(
        functools.partial(_tail_kernel, chunk=chunk, gd=d // N_GROUPS, nc=512),
        grid=(m // tm,),
        in_specs=[
            pspec(SLAB_U), pspec(SLAB_V), pspec(SLAB_Z), pspec(SLAB_GA), pspec(SLAB_GB),
            pl.BlockSpec((d, tm), lambda i: (0, i)), rows,
            pl.BlockSpec((None, 1, d), lambda i: (i // tiles_per_seq, 0, 2)),
            weight, weight, weight,
            vec, vec,
            pl.BlockSpec((N_GROUPS, chunk, chunk), lambda i: (0, 0, 0)),
            pl.BlockSpec((chunk, d), lambda i: (0, 0)),
            vec,
        ],
        out_specs=rows,
        out_shape=jax.ShapeDtypeStruct((m, d), F32),
        scratch_shapes=[pltpu.VMEM((tm, d), BF16), pltpu.VMEM((tm, d), F32),
                        pltpu.VMEM((tm, d), BF16)],
        compiler_params=_params("parallel"),
        name="tail",
    )(proj, proj, proj, proj, proj, ybt, x2, mod3, wa, wb, wo, ln_g, ln_b, w_s, bias_full, fgain)


def kernel(x, c, w_ada, b_ada, norm_gain, w_in, ln_v_gain, ln_v_bias, w_spatial, b_spatial,
           lambda_q1, lambda_k1, lambda_q2, lambda_k2, subln_gain, w_branch_a, w_branch_b,
           w_out, final_norm_gain):
    bsz, seq, d = x.shape
    depth = w_ada.shape[0]
    assert depth == 1, "single-layer trunk"
    dh = lambda_q1.shape[-1]
    assert d == N_HEADS * 2 * dh and w_in.shape[-1] == N_SLABS * d
    assert seq % ST == 0 and d % ST == 0 and ST % (2 * dh) == 0
    l = 0

    x2 = x.reshape(bsz * seq, d)
    mod = _ada(c, w_ada[l], b_ada[l])
    mod3 = mod.reshape(bsz, 1, 3 * d)

    proj = _inproj(x2, mod3, norm_gain[l].reshape(1, d), w_in[l],
                   LOG2E / math.sqrt(dh), seq)

    bias_full = jnp.repeat(jnp.transpose(b_spatial[l]), d // N_GROUPS, axis=1)

    slopes = jnp.asarray([2.0 ** (-8.0 * (i + 1) / N_HEADS) for i in range(N_HEADS)], F32)
    ybt, wa, wb, wo = _attn(proj, slopes, lambda_q1[l].reshape(1, dh), lambda_k1[l].reshape(1, dh),
                            lambda_q2[l].reshape(1, dh), lambda_k2[l].reshape(1, dh),
                            subln_gain[l].reshape(2 * dh, 1),
                            (w_branch_a[l], w_branch_b[l], w_out[l]), bsz, seq)

    out = _tail(proj, ybt, x2, mod3, wa, wb, wo, ln_v_gain[l].reshape(1, d),
                ln_v_bias[l].reshape(1, d),
                w_spatial[l], bias_full, final_norm_gain.reshape(1, d), seq)
    return out.reshape(bsz, seq, d)
```

```python
import functools
import math

import jax
import jax.numpy as jnp
from jax import lax
from jax.experimental import pallas as pl
from jax.experimental.pallas import tpu as pltpu

F32 = jnp.float32
BF16 = jnp.bfloat16

N_HEADS = 8
N_GROUPS = 8
N_SLABS = 9
SLAB_U, SLAB_V, SLAB_Z, SLAB_Q, SLAB_K, SLAB_BV, SLAB_BZ, SLAB_GA, SLAB_GB = range(N_SLABS)
ST = 512
EPS = 1e-6
SUBLN_EPS = 1e-5
LAMBDA_INIT = 0.8 - 0.6 * math.exp(-0.3 * 0)
LOG2E = 1.0 / math.log(2.0)

V7X_VMEM_BYTES = 64 * 1024 * 1024
VMEM_LIMIT = V7X_VMEM_BYTES * 7 // 8


def _sigmoid(x):
    return 0.5 * jnp.tanh(0.5 * x) + 0.5


def _silu(x):
    hx = 0.5 * x
    return hx * jnp.tanh(hx) + hx


def _gelu_tanh(x):
    c = math.sqrt(2.0 / math.pi)
    hx = 0.5 * x
    return hx * jnp.tanh(x * ((0.044715 * c) * (x * x) + c)) + hx


def _params(*sem):
    return pltpu.CompilerParams(dimension_semantics=sem, vmem_limit_bytes=VMEM_LIMIT)


def _ada_kernel(c_ref, w_ref, b_ref, o_ref):
    c = c_ref[...]
    o_ref[...] = jnp.dot(_silu(c), w_ref[...], preferred_element_type=F32) + b_ref[...]


def _ada(c, w_ada, b_ada, tn=1024):
    bsz, d = c.shape
    n = w_ada.shape[1]
    return pl.pallas_call(
        _ada_kernel,
        grid=(n // tn,),
        in_specs=[
            pl.BlockSpec((bsz, d), lambda j: (0, 0)),
            pl.BlockSpec((d, tn), lambda j: (0, j)),
            pl.BlockSpec((1, tn), lambda j: (0, j)),
        ],
        out_specs=pl.BlockSpec((bsz, tn), lambda j: (0, j)),
        out_shape=jax.ShapeDtypeStruct((bsz, n), F32),
        compiler_params=_params("parallel"),
        name="ada",
    )(c, w_ada, b_ada.reshape(1, n))


def _inproj_kernel(*refs, nx, rows, per, q_scale):
    x_refs = refs[:nx]
    g_ref, shift_ref, scale_ref, w_ref, o_ref, h_ref = refs[nx:]
    j = pl.program_id(1)

    @pl.when(j == 0)
    def _():
        gsc = g_ref[...] * (1.0 + scale_ref[...])
        sh = shift_ref[...]
        for k, x_ref in enumerate(x_refs):
            base = k * x_ref.shape[0]

            def body(r, carry, x_ref=x_ref, base=base):
                sl = pl.ds(pl.multiple_of(r * rows, rows), rows)
                x = x_ref[sl, :]
                rstd = lax.rsqrt(jnp.mean(x * x, axis=-1, keepdims=True) + EPS)
                h_ref[pl.ds(pl.multiple_of(base + r * rows, rows), rows), :] = (
                    ((x * rstd) * gsc + sh).astype(BF16))
                return carry

            lax.fori_loop(0, x_ref.shape[0] // rows, body, 0, unroll=4)

    n_tok, n_feat = o_ref.shape[0], o_ref.shape[1]
    slab = j // per
    epilogues = (
        ((SLAB_U, SLAB_V), _gelu_tanh),
        ((SLAB_Z,), _silu),
        ((SLAB_GA, SLAB_GB), _sigmoid),
        ((SLAB_Q,), lambda acc: acc * q_scale),
        ((SLAB_K,), lambda acc: acc),
    )
    for slabs, act in epilogues:
        @pl.when(functools.reduce(jnp.logical_or, [slab == s for s in slabs]))
        def _(act=act):
            acc = act(jnp.dot(h_ref[...], w_ref[...].astype(BF16), preferred_element_type=F32))
            for a in range(n_tok):
                for b in range(n_feat):
                    o_ref[a, b] = acc[a * ST:(a + 1) * ST, b * ST:(b + 1) * ST].astype(o_ref.dtype)

    for s_t, act in ((SLAB_BV, lambda acc: acc), (SLAB_BZ, _silu)):
        @pl.when(slab == s_t)
        def _(act=act):
            acc_t = act(lax.dot_general(w_ref[...].astype(BF16), h_ref[...], (((0,), (1,)), ((), ())),
                                        preferred_element_type=F32))
            for a in range(n_tok):
                for b in range(n_feat):
                    o_ref[a, b] = acc_t[b * ST:(b + 1) * ST, a * ST:(a + 1) * ST].astype(o_ref.dtype)


def _inproj(x2, mod3, gain, w_in, q_scale, seq, tm=1024, tn=1024):
    m, d = x2.shape
    n = w_in.shape[1]
    slab = n // N_SLABS
    per = slab // tn
    tiles_per_seq = seq // tm
    ni, nj = m // tm, n // tn
    nx = 4
    switch = [1 + k for k in range(nx)]
    assert tm % (nx * 32) == 0 and switch[-1] < nj

    def x_slice(k):
        return pl.BlockSpec(
            (tm // nx, d),
            lambda i, j: (jnp.minimum(i + (j >= switch[k]).astype(jnp.int32), ni - 1) * nx + k, 0))

    return pl.pallas_call(
        functools.partial(_inproj_kernel, nx=nx, rows=32, per=per, q_scale=q_scale),
        grid=(ni, nj),
        in_specs=[x_slice(k) for k in range(nx)] + [
            pl.BlockSpec((1, d), lambda i, j: (0, 0)),
            pl.BlockSpec((None, 1, d), lambda i, j: (i // tiles_per_seq, 0, 0)),
            pl.BlockSpec((None, 1, d), lambda i, j: (i // tiles_per_seq, 0, 1)),
            pl.BlockSpec((d, tn), lambda i, j: (0, j)),
        ],
        out_specs=pl.BlockSpec((None, tm // ST, tn // ST, ST, ST),
                               lambda i, j: (j // per, i, j % per, 0, 0)),
        out_shape=jax.ShapeDtypeStruct((N_SLABS, m // ST, slab // ST, ST, ST), BF16),
        scratch_shapes=[pltpu.VMEM((tm, d), BF16)],
        compiler_params=_params("parallel", "arbitrary"),
        name="inproj",
    )(*([x2] * nx), gain, mod3, mod3, w_in)


def _gmlp_pieces(u_ref, v_ref, z_ref, lng_ref, lnb_ref, ws_ref, bs_ref, o_ref, chunk, gd):
    n_ft, ta, _ = u_ref.shape
    d = n_ft * ST
    state = {}

    def layernorm(c):
        rows = pl.ds(c * chunk, chunk)
        gv = [v_ref[f, rows, :].astype(F32) for f in range(n_ft)]
        mu = sum(jnp.sum(t, axis=-1, keepdims=True) for t in gv) * (1.0 / d)
        xc = [t - mu for t in gv]
        var = sum(jnp.sum(t * t, axis=-1, keepdims=True) for t in xc) * (1.0 / d)
        rstd = lax.rsqrt(var + EPS)
        state[c] = [((xc[f] * rstd) * lng_ref[:, f * ST:(f + 1) * ST]
                     + lnb_ref[:, f * ST:(f + 1) * ST]).astype(BF16) for f in range(n_ft)]

    def group(c, g):
        rows = pl.ds(c * chunk, chunk)
        row = lax.broadcasted_iota(jnp.int32, (chunk, chunk), 0)
        col = lax.broadcasted_iota(jnp.int32, (chunk, chunk), 1)
        f, off = divmod(g * gd, ST)
        ws = (ws_ref[g] * (row >= col).astype(F32)).astype(BF16)
        sv = jnp.dot(ws, state[c][f][:, off:off + gd], preferred_element_type=F32)
        sv = sv + bs_ref[:, g * gd:(g + 1) * gd]
        u = u_ref[f, rows, off:off + gd].astype(F32)
        z = z_ref[f, rows, off:off + gd].astype(F32)
        o_ref[rows, g * gd:(g + 1) * gd] = ((u * sv) * z).astype(o_ref.dtype)

    steps = []
    for c in range(ta // chunk):
        steps.append(functools.partial(layernorm, c))
        steps += [functools.partial(group, c, g) for g in range(N_GROUPS)]
    return steps


def _attn_kernel(slopes_ref, lq1_ref, lk1_ref, lq2_ref, lk2_ref, q_ref, k_ref, vt_ref, zt_ref,
                 sg_ref, wa_ref, wb_ref, wo_ref, o_ref, wa_bf_ref, wb_bf_ref, wo_bf_ref,
                 bias_ref, vta_ref, s_ref, smax_ref, acc_ref, *, dh):
    nq, blk, dv = q_ref.shape
    for src, dst in ((wa_ref, wa_bf_ref), (wb_ref, wb_bf_ref), (wo_ref, wo_bf_ref)):
        dst[...] = src[...].astype(dst.dtype)
    slope = slopes_ref[pl.program_id(0)] * LOG2E

    @pl.when(pl.program_id(1) == 0)
    def _():
        s_pos = lax.broadcasted_iota(jnp.int32, (blk, blk), 0)
        t_pos = lax.broadcasted_iota(jnp.int32, (blk, blk), 1)
        bias = slope * (s_pos - t_pos).astype(F32)
        bias_ref[0] = bias
        bias_ref[1] = jnp.where(s_pos <= t_pos, bias, -jnp.inf)

    for j in range(nq):
        vta_ref[j, :dv, :] = vt_ref[j]
        vta_ref[j, dv:, :] = jnp.ones((vta_ref.shape[1] - dv, blk), vta_ref.dtype)

    lam_init = jnp.float32(LAMBDA_INIT)
    lam = (jnp.exp(jnp.sum(lq1_ref[...] * lk1_ref[...], axis=-1, keepdims=True))
           - jnp.exp(jnp.sum(lq2_ref[...] * lk2_ref[...], axis=-1, keepdims=True))
           + lam_init)
    gain = sg_ref[...] * (1.0 - lam_init)

    def scores(j, n):
        slot = j % 2
        bias = bias_ref[int(j == n)]
        for c in range(2):
            q = q_ref[n, :, c * dh:(c + 1) * dh]
            k = k_ref[j, :, c * dh:(c + 1) * dh]
            s = lax.dot_general(k, q, (((1,), (1,)), ((), ())), preferred_element_type=F32)
            s = s + bias
            s_ref[slot, c] = s
            smax_ref[slot, c] = jnp.max(s, axis=0, keepdims=True)

    def consume(j, n, m):
        slot = j % 2
        vta = vta_ref[j]
        off = slope * float((j - n) * blk)
        for c in range(2):
            m_new = smax_ref[slot, c] + off
            if m[c] is not None:
                m_new = jnp.maximum(m[c], m_new)
            p = jnp.exp2(s_ref[slot, c] - (m_new - off)).astype(vta.dtype)
            pv = jnp.dot(vta, p, preferred_element_type=F32)
            if m[c] is None:
                acc_ref[c] = pv
            else:
                acc_ref[c] = jnp.exp2(m[c] - m_new) * acc_ref[c] + pv
            m[c] = m_new

    for n in range(nq):
        m = [None, None]
        scores(0, n)
        for j in range(n):
            scores(j + 1, n)
            consume(j, n, m)
        consume(n, n, m)

        a0, a1 = acc_ref[0], acc_ref[1]
        r0 = 1.0 / a0[dv:dv + 1]
        r1 = lam * (1.0 / a1[dv:dv + 1])
        o = a0[:dv] * r0 - a1[:dv] * r1
        o = o * lax.rsqrt(jnp.mean(o * o, axis=0, keepdims=True) + SUBLN_EPS)
        o = (o * gain) * zt_ref[n].astype(F32)
        o_ref[:, n * blk:(n + 1) * blk] = o.astype(o_ref.dtype)


def _attn(proj, slopes, lq1, lk1, lq2, lk2, subln_g, weights, bsz, seq):
    _, n_tt, n_ft, _, _ = proj.shape
    m, d = n_tt * ST, n_ft * ST
    dv = d // N_HEADS
    dh = dv // 2
    nq = seq // ST
    hp = ST // dv
    ones_rows = 16
    tok_major = lambda s: pl.BlockSpec((None, nq, None, ST, dv),
                                       lambda h, b, s=s: (s, b, h // hp, 0, h % hp))
    feat_major = lambda s: pl.BlockSpec((None, nq, None, dv, ST),
                                        lambda h, b, s=s: (s, b, h // hp, h % hp, 0))
    vec = pl.BlockSpec((1, dh), lambda h, b: (0, 0))
    w_rows = d // (N_HEADS * bsz)
    assert w_rows * N_HEADS * bsz == d and w_rows % 16 == 0
    w_slice = pl.BlockSpec((w_rows, d), lambda h, b: (h * bsz + b, 0))
    return pl.pallas_call(
        functools.partial(_attn_kernel, dh=dh),
        grid=(N_HEADS, bsz),
        in_specs=[
            pl.BlockSpec(memory_space=pltpu.SMEM),
            vec, vec, vec, vec,
            tok_major(SLAB_Q), tok_major(SLAB_K), feat_major(SLAB_BV), feat_major(SLAB_BZ),
            pl.BlockSpec((dv, 1), lambda h, b: (0, 0)),
            w_slice, w_slice, w_slice,
        ],
        out_specs=[pl.BlockSpec((dv, seq), lambda h, b: (h, b)), w_slice, w_slice, w_slice],
        out_shape=[jax.ShapeDtypeStruct((d, m), BF16)] + [jax.ShapeDtypeStruct((d, d), BF16)] * 3,
        scratch_shapes=[
            pltpu.VMEM((2, ST, ST), F32),
            pltpu.VMEM((nq, dv + ones_rows, ST), BF16),
            pltpu.VMEM((2, 2, ST, ST), F32),
            pltpu.VMEM((2, 2, 1, ST), F32),
            pltpu.VMEM((2, dv + ones_rows, ST), F32),
        ],
        compiler_params=_params("arbitrary", "arbitrary"),
        name="attn",
    )(slopes, lq1, lk1, lq2, lk2, proj, proj, proj, proj, subln_g, *weights)


def _tail_kernel(u_ref, v_ref, z_ref, ga_ref, gb_ref, ybt_ref, x_ref, gate_ref, wa_ref, wb_ref,
                 wo_ref, lng_ref, lnb_ref, ws_ref, bs_ref, fg_ref, o_ref, ya_ref, b_ref, m_ref, *,
                 chunk, gd, nc):
    d = wb_ref.shape[1]
    cols = [slice(k, k + nc) for k in range(0, d, nc)]

    gm = _gmlp_pieces(u_ref, v_ref, z_ref, lng_ref, lnb_ref, ws_ref, bs_ref, ya_ref, chunk, gd)
    per = -(-len(gm) // len(cols))
    for k, cs in enumerate(cols):
        b_ref[:, cs] = lax.dot_general(ybt_ref[...], wb_ref[:, cs], (((0,), (0,)), ((), ())),
                                       preferred_element_type=F32)
        for step in gm[k * per:(k + 1) * per]:
            step()

    def merge(cs, a):
        f, off = divmod(cs.start, ST)
        ga = ga_ref[f, :, off:off + nc].astype(F32)
        gb = gb_ref[f, :, off:off + nc].astype(F32)
        m_ref[:, cs] = (ga * a + gb * b_ref[:, cs]).astype(m_ref.dtype)

    pending = None
    for cs in cols:
        a = jnp.dot(ya_ref[...], wa_ref[:, cs], preferred_element_type=F32)
        if pending is not None:
            merge(*pending)
        pending = (cs, a)
    merge(*pending)

    ssq = None
    for cs in cols:
        out = jnp.dot(m_ref[...], wo_ref[:, cs], preferred_element_type=F32)
        r = x_ref[:, cs] + gate_ref[:, cs] * out
        o_ref[:, cs] = r
        part = jnp.sum(r * r, axis=-1, keepdims=True)
        ssq = part if ssq is None else ssq + part
    rstd = lax.rsqrt(ssq * (1.0 / d) + EPS)
    o_ref[...] = (o_ref[...] * rstd) * fg_ref[...]


def _tail(proj, ybt, x2, mod3, wa, wb, wo, ln_g, ln_b, w_s, bias_full, fgain, seq, tm=256):
    _, n_tt, n_ft, _, _ = proj.shape
    m, d = x2.shape
    chunk = w_s.shape[-1]
    assert ST % tm == 0 and tm % chunk == 0 and seq % tm == 0
    per = ST // tm
    tiles_per_seq = seq // tm
    pspec = lambda s: pl.BlockSpec((None, None, n_ft, tm, ST),
                                   lambda i, s=s: (s, i // per, 0, i % per, 0))
    rows = pl.BlockSpec((tm, d), lambda i: (i, 0))
    vec = pl.BlockSpec((1, d), lambda i: (0, 0))
    weight = pl.BlockSpec((d, d), lambda i: (0, 0), pipeline_mode=pl.Buffered(1))
    return pl.pallas_call(
        functools.partial(_tail_kernel, chunk=chunk, gd=d // N_GROUPS, nc=512),
        grid=(m // tm,),
        in_specs=[
            pspec(SLAB_U), pspec(SLAB_V), pspec(SLAB_Z), pspec(SLAB_GA), pspec(SLAB_GB),
            pl.BlockSpec((d, tm), lambda i: (0, i)), rows,
            pl.BlockSpec((None, 1, d), lambda i: (i // tiles_per_seq, 0, 2)),
            weight, weight, weight,
            vec, vec,
            pl.BlockSpec((N_GROUPS, chunk, chunk), lambda i: (0, 0, 0)),
            pl.BlockSpec((chunk, d), lambda i: (0, 0)),
            vec,
        ],
        out_specs=rows,
        out_shape=jax.ShapeDtypeStruct((m, d), F32),
        scratch_shapes=[pltpu.VMEM((tm, d), BF16), pltpu.VMEM((tm, d), F32),
                        pltpu.VMEM((tm, d), BF16)],
        compiler_params=_params("parallel"),
        name="tail",
    )(proj, proj, proj, proj, proj, ybt, x2, mod3, wa, wb, wo, ln_g, ln_b, w_s, bias_full, fgain)


def kernel(x, c, w_ada, b_ada, norm_gain, w_in, ln_v_gain, ln_v_bias, w_spatial, b_spatial,
           lambda_q1, lambda_k1, lambda_q2, lambda_k2, subln_gain, w_branch_a, w_branch_b,
           w_out, final_norm_gain):
    bsz, seq, d = x.shape
    depth = w_ada.shape[0]
    assert depth == 1, "single-layer trunk"
    dh = lambda_q1.shape[-1]
    assert d == N_HEADS * 2 * dh and w_in.shape[-1] == N_SLABS * d
    assert seq % ST == 0 and d % ST == 0 and ST % (2 * dh) == 0
    l = 0

    x2 = x.reshape(bsz * seq, d)
    mod = _ada(c, w_ada[l], b_ada[l])
    mod3 = mod.reshape(bsz, 1, 3 * d)

    proj = _inproj(x2, mod3, norm_gain[l].reshape(1, d), w_in[l],
                   LOG2E / math.sqrt(dh), seq)

    bias_full = jnp.repeat(jnp.transpose(b_spatial[l]), d // N_GROUPS, axis=1)

    slopes = jnp.asarray([2.0 ** (-8.0 * (i + 1) / N_HEADS) for i in range(N_HEADS)], F32)
    ybt, wa, wb, wo = _attn(proj, slopes, lambda_q1[l].reshape(1, dh), lambda_k1[l].reshape(1, dh),
                            lambda_q2[l].reshape(1, dh), lambda_k2[l].reshape(1, dh),
                            subln_gain[l].reshape(2 * dh, 1),
                            (w_branch_a[l], w_branch_b[l], w_out[l]), bsz, seq)

    out = _tail(proj, ybt, x2, mod3, wa, wb, wo, ln_v_gain[l].reshape(1, d),
                ln_v_bias[l].reshape(1, d),
                w_spatial[l], bias_full, final_norm_gain.reshape(1, d), seq)
    return out.reshape(bsz, seq, d)
```

```python
import functools
import math

import jax
import jax.numpy as jnp
from jax import lax
from jax.experimental import pallas as pl
from jax.experimental.pallas import tpu as pltpu

F32 = jnp.float32
BF16 = jnp.bfloat16

N_HEADS = 8
N_GROUPS = 8
N_SLABS = 9
SLAB_U, SLAB_V, SLAB_Z, SLAB_Q, SLAB_K, SLAB_BV, SLAB_BZ, SLAB_GA, SLAB_GB = range(N_SLABS)
ST = 512
EPS = 1e-6
SUBLN_EPS = 1e-5
LAMBDA_INIT = 0.8 - 0.6 * math.exp(-0.3 * 0)
LOG2E = 1.0 / math.log(2.0)

V7X_VMEM_BYTES = 64 * 1024 * 1024
VMEM_LIMIT = V7X_VMEM_BYTES * 7 // 8


def _sigmoid(x):
    return 0.5 * jnp.tanh(0.5 * x) + 0.5


def _silu(x):
    hx = 0.5 * x
    return hx * jnp.tanh(hx) + hx


def _gelu_tanh(x):
    c = math.sqrt(2.0 / math.pi)
    hx = 0.5 * x
    return hx * jnp.tanh(x * ((0.044715 * c) * (x * x) + c)) + hx


def _params(*sem):
    return pltpu.CompilerParams(dimension_semantics=sem, vmem_limit_bytes=VMEM_LIMIT)


def _ada_kernel(c_ref, w_ref, b_ref, o_ref):
    c = c_ref[...]
    o_ref[...] = jnp.dot(_silu(c), w_ref[...], preferred_element_type=F32) + b_ref[...]


def _ada(c, w_ada, b_ada, tn=1024):
    bsz, d = c.shape
    n = w_ada.shape[1]
    return pl.pallas_call(
        _ada_kernel,
        grid=(n // tn,),
        in_specs=[
            pl.BlockSpec((bsz, d), lambda j: (0, 0)),
            pl.BlockSpec((d, tn), lambda j: (0, j)),
            pl.BlockSpec((1, tn), lambda j: (0, j)),
        ],
        out_specs=pl.BlockSpec((bsz, tn), lambda j: (0, j)),
        out_shape=jax.ShapeDtypeStruct((bsz, n), F32),
        compiler_params=_params("parallel"),
        name="ada",
    )(c, w_ada, b_ada.reshape(1, n))


def _inproj_kernel(*refs, nx, rows, per, q_scale):
    x_refs = refs[:nx]
    g_ref, shift_ref, scale_ref, w_ref, o_ref, h_ref = refs[nx:]
    j = pl.program_id(1)

    @pl.when(j == 0)
    def _():
        gsc = g_ref[...] * (1.0 + scale_ref[...])
        sh = shift_ref[...]
        for k, x_ref in enumerate(x_refs):
            base = k * x_ref.shape[0]

            def body(r, carry, x_ref=x_ref, base=base):
                sl = pl.ds(pl.multiple_of(r * rows, rows), rows)
                x = x_ref[sl, :]
                rstd = lax.rsqrt(jnp.mean(x * x, axis=-1, keepdims=True) + EPS)
                h_ref[pl.ds(pl.multiple_of(base + r * rows, rows), rows), :] = (
                    ((x * rstd) * gsc + sh).astype(BF16))
                return carry

            lax.fori_loop(0, x_ref.shape[0] // rows, body, 0, unroll=4)

    n_tok, n_feat = o_ref.shape[0], o_ref.shape[1]
    slab = j // per
    epilogues = (
        ((SLAB_U, SLAB_V), _gelu_tanh),
        ((SLAB_Z,), _silu),
        ((SLAB_GA, SLAB_GB), _sigmoid),
        ((SLAB_Q,), lambda acc: acc * q_scale),
        ((SLAB_K,), lambda acc: acc),
    )
    for slabs, act in epilogues:
        @pl.when(functools.reduce(jnp.logical_or, [slab == s for s in slabs]))
        def _(act=act):
            acc = act(jnp.dot(h_ref[...], w_ref[...].astype(BF16), preferred_element_type=F32))
            for a in range(n_tok):
                for b in range(n_feat):
                    o_ref[a, b] = acc[a * ST:(a + 1) * ST, b * ST:(b + 1) * ST].astype(o_ref.dtype)

    for s_t, act in ((SLAB_BV, lambda acc: acc), (SLAB_BZ, _silu)):
        @pl.when(slab == s_t)
        def _(act=act):
            acc_t = act(lax.dot_general(w_ref[...].astype(BF16), h_ref[...], (((0,), (1,)), ((), ())),
                                        preferred_element_type=F32))
            for a in range(n_tok):
                for b in range(n_feat):
                    o_ref[a, b] = acc_t[b * ST:(b + 1) * ST, a * ST:(a + 1) * ST].astype(o_ref.dtype)


def _inproj(x2, mod3, gain, w_in, q_scale, seq, tm=1024, tn=1024):
    m, d = x2.shape
    n = w_in.shape[1]
    slab = n // N_SLABS
    per = slab // tn
    tiles_per_seq = seq // tm
    ni, nj = m // tm, n // tn
    nx = 4
    switch = [1 + k for k in range(nx)]
    assert tm % (nx * 32) == 0 and switch[-1] < nj

    def x_slice(k):
        return pl.BlockSpec(
            (tm // nx, d),
            lambda i, j: (jnp.minimum(i + (j >= switch[k]).astype(jnp.int32), ni - 1) * nx + k, 0))

    return pl.pallas_call(
        functools.partial(_inproj_kernel, nx=nx, rows=32, per=per, q_scale=q_scale),
        grid=(ni, nj),
        in_specs=[x_slice(k) for k in range(nx)] + [
            pl.BlockSpec((1, d), lambda i, j: (0, 0)),
            pl.BlockSpec((None, 1, d), lambda i, j: (i // tiles_per_seq, 0, 0)),
            pl.BlockSpec((None, 1, d), lambda i, j: (i // tiles_per_seq, 0, 1)),
            pl.BlockSpec((d, tn), lambda i, j: (0, j)),
        ],
        out_specs=pl.BlockSpec((None, tm // ST, tn // ST, ST, ST),
                               lambda i, j: (j // per, i, j % per, 0, 0)),
        out_shape=jax.ShapeDtypeStruct((N_SLABS, m // ST, slab // ST, ST, ST), BF16),
        scratch_shapes=[pltpu.VMEM((tm, d), BF16)],
        compiler_params=_params("parallel", "arbitrary"),
        name="inproj",
    )(*([x2] * nx), gain, mod3, mod3, w_in)


def _gmlp_pieces(u_ref, v_ref, z_ref, lng_ref, lnb_ref, ws_ref, bs_ref, o_ref, chunk, gd):
    n_ft, ta, _ = u_ref.shape
    d = n_ft * ST
    state = {}

    def layernorm(c):
        rows = pl.ds(c * chunk, chunk)
        gv = [v_ref[f, rows, :].astype(F32) for f in range(n_ft)]
        mu = sum(jnp.sum(t, axis=-1, keepdims=True) for t in gv) * (1.0 / d)
        xc = [t - mu for t in gv]
        var = sum(jnp.sum(t * t, axis=-1, keepdims=True) for t in xc) * (1.0 / d)
        rstd = lax.rsqrt(var + EPS)
        state[c] = [((xc[f] * rstd) * lng_ref[:, f * ST:(f + 1) * ST]
                     + lnb_ref[:, f * ST:(f + 1) * ST]).astype(BF16) for f in range(n_ft)]

    def group(c, g):
        rows = pl.ds(c * chunk, chunk)
        row = lax.broadcasted_iota(jnp.int32, (chunk, chunk), 0)
        col = lax.broadcasted_iota(jnp.int32, (chunk, chunk), 1)
        f, off = divmod(g * gd, ST)
        ws = (ws_ref[g] * (row >= col).astype(F32)).astype(BF16)
        sv = jnp.dot(ws, state[c][f][:, off:off + gd], preferred_element_type=F32)
        sv = sv + bs_ref[:, g * gd:(g + 1) * gd]
        u = u_ref[f, rows, off:off + gd].astype(F32)
        z = z_ref[f, rows, off:off + gd].astype(F32)
        o_ref[rows, g * gd:(g + 1) * gd] = ((u * sv) * z).astype(o_ref.dtype)

    steps = []
    for c in range(ta // chunk):
        steps.append(functools.partial(layernorm, c))
        steps += [functools.partial(group, c, g) for g in range(N_GROUPS)]
    return steps


def _attn_kernel(slopes_ref, lq1_ref, lk1_ref, lq2_ref, lk2_ref, q_ref, k_ref, vt_ref, zt_ref,
                 sg_ref, wa_ref, wb_ref, wo_ref, o_ref, wa_bf_ref, wb_bf_ref, wo_bf_ref,
                 bias_ref, vta_ref, s_ref, smax_ref, acc_ref, *, dh):
    nq, blk, dv = q_ref.shape
    for src, dst in ((wa_ref, wa_bf_ref), (wb_ref, wb_bf_ref), (wo_ref, wo_bf_ref)):
        dst[...] = src[...].astype(dst.dtype)
    slope = slopes_ref[pl.program_id(0)] * LOG2E

    @pl.when(pl.program_id(1) == 0)
    def _():
        s_pos = lax.broadcasted_iota(jnp.int32, (blk, blk), 0)
        t_pos = lax.broadcasted_iota(jnp.int32, (blk, blk), 1)
        bias = slope * (s_pos - t_pos).astype(F32)
        bias_ref[0] = bias
        bias_ref[1] = jnp.where(s_pos <= t_pos, bias, -jnp.inf)

    for j in range(nq):
        vta_ref[j, :dv, :] = vt_ref[j]
        vta_ref[j, dv:, :] = jnp.ones((vta_ref.shape[1] - dv, blk), vta_ref.dtype)

    lam_init = jnp.float32(LAMBDA_INIT)
    lam = (jnp.exp(jnp.sum(lq1_ref[...] * lk1_ref[...], axis=-1, keepdims=True))
           - jnp.exp(jnp.sum(lq2_ref[...] * lk2_ref[...], axis=-1, keepdims=True))
           + lam_init)
    gain = sg_ref[...] * (1.0 - lam_init)

    def scores(j, n):
        slot = j % 2
        bias = bias_ref[int(j == n)]
        for c in range(2):
            q = q_ref[n, :, c * dh:(c + 1) * dh]
            k = k_ref[j, :, c * dh:(c + 1) * dh]
            s = lax.dot_general(k, q, (((1,), (1,)), ((), ())), preferred_element_type=F32)
            s = s + bias
            s_ref[slot, c] = s
            smax_ref[slot, c] = jnp.max(s, axis=0, keepdims=True)

    def consume(j, n, m):
        slot = j % 2
        vta = vta_ref[j]
        off = slope * float((j - n) * blk)
        for c in range(2):
            m_new = smax_ref[slot, c] + off
            if m[c] is not None:
                m_new = jnp.maximum(m[c], m_new)
            p = jnp.exp2(s_ref[slot, c] - (m_new - off)).astype(vta.dtype)
            pv = jnp.dot(vta, p, preferred_element_type=F32)
            if m[c] is None:
                acc_ref[c] = pv
            else:
                acc_ref[c] = jnp.exp2(m[c] - m_new) * acc_ref[c] + pv
            m[c] = m_new

    hb = blk // 2
    nt = (((1,), (1,)), ((), ()))

    def scores_diag(n):
        slot = n % 2
        for c in range(2):
            q = q_ref[n, :, c * dh:(c + 1) * dh]
            k = k_ref[n, :, c * dh:(c + 1) * dh]
            sa = lax.dot_general(k[:hb], q, nt, preferred_element_type=F32) + bias_ref[1, :hb, :]
            sb = (lax.dot_general(k[hb:], q[hb:], nt, preferred_element_type=F32)
                  + bias_ref[1, hb:, hb:])
            s_ref[slot, c, :hb, :] = sa
            s_ref[slot, c, hb:, hb:] = sb
            ma = jnp.max(sa, axis=0, keepdims=True)
            smax_ref[slot, c, :, :hb] = ma[:, :hb]
            smax_ref[slot, c, :, hb:] = jnp.maximum(ma[:, hb:], jnp.max(sb, axis=0, keepdims=True))

    def consume_diag(n, m):
        slot = n % 2
        vta = vta_ref[n]
        for c in range(2):
            m_new = smax_ref[slot, c]
            if m[c] is not None:
                m_new = jnp.maximum(m[c], m_new)
            pa = jnp.exp2(s_ref[slot, c, :hb, :] - m_new).astype(vta.dtype)
            pb = jnp.exp2(s_ref[slot, c, hb:, hb:] - m_new[:, hb:]).astype(vta.dtype)
            pva = jnp.dot(vta[:, :hb], pa, preferred_element_type=F32)
            pvb = jnp.dot(vta[:, hb:], pb, preferred_element_type=F32)
            if m[c] is None:
                acc_ref[c, :, :hb] = pva[:, :hb]
                acc_ref[c, :, hb:] = pva[:, hb:] + pvb
            else:
                alpha = jnp.exp2(m[c] - m_new)
                acc_ref[c, :, :hb] = alpha[:, :hb] * acc_ref[c, :, :hb] + pva[:, :hb]
                acc_ref[c, :, hb:] = alpha[:, hb:] * acc_ref[c, :, hb:] + (pva[:, hb:] + pvb)
            m[c] = m_new

    for n in range(nq):
        m = [None, None]
        if n == 0:
            scores_diag(0)
        else:
            scores(0, n)
        for j in range(n):
            if j + 1 == n:
                scores_diag(n)
            else:
                scores(j + 1, n)
            consume(j, n, m)
        consume_diag(n, m)

        a0, a1 = acc_ref[0], acc_ref[1]
        r0 = 1.0 / a0[dv:dv + 1]
        r1 = lam * (1.0 / a1[dv:dv + 1])
        o = a0[:dv] * r0 - a1[:dv] * r1
        o = o * lax.rsqrt(jnp.mean(o * o, axis=0, keepdims=True) + SUBLN_EPS)
        o = (o * gain) * zt_ref[n].astype(F32)
        o_ref[:, n * blk:(n + 1) * blk] = o.astype(o_ref.dtype)


def _attn(proj, slopes, lq1, lk1, lq2, lk2, subln_g, weights, bsz, seq):
    _, n_tt, n_ft, _, _ = proj.shape
    m, d = n_tt * ST, n_ft * ST
    dv = d // N_HEADS
    dh = dv // 2
    nq = seq // ST
    hp = ST // dv
    ones_rows = 16
    tok_major = lambda s: pl.BlockSpec((None, nq, None, ST, dv),
                                       lambda h, b, s=s: (s, b, h // hp, 0, h % hp))
    feat_major = lambda s: pl.BlockSpec((None, nq, None, dv, ST),
                                        lambda h, b, s=s: (s, b, h // hp, h % hp, 0))
    vec = pl.BlockSpec((1, dh), lambda h, b: (0, 0))
    w_rows = d // (N_HEADS * bsz)
    assert w_rows * N_HEADS * bsz == d and w_rows % 16 == 0
    w_slice = pl.BlockSpec((w_rows, d), lambda h, b: (h * bsz + b, 0))
    return pl.pallas_call(
        functools.partial(_attn_kernel, dh=dh),
        grid=(N_HEADS, bsz),
        in_specs=[
            pl.BlockSpec(memory_space=pltpu.SMEM),
            vec, vec, vec, vec,
            tok_major(SLAB_Q), tok_major(SLAB_K), feat_major(SLAB_BV), feat_major(SLAB_BZ),
            pl.BlockSpec((dv, 1), lambda h, b: (0, 0)),
            w_slice, w_slice, w_slice,
        ],
        out_specs=[pl.BlockSpec((dv, seq), lambda h, b: (h, b)), w_slice, w_slice, w_slice],
        out_shape=[jax.ShapeDtypeStruct((d, m), BF16)] + [jax.ShapeDtypeStruct((d, d), BF16)] * 3,
        scratch_shapes=[
            pltpu.VMEM((2, ST, ST), F32),
            pltpu.VMEM((nq, dv + ones_rows, ST), BF16),
            pltpu.VMEM((2, 2, ST, ST), F32),
            pltpu.VMEM((2, 2, 1, ST), F32),
            pltpu.VMEM((2, dv + ones_rows, ST), F32),
        ],
        compiler_params=_params("arbitrary", "arbitrary"),
        name="attn",
    )(slopes, lq1, lk1, lq2, lk2, proj, proj, proj, proj, subln_g, *weights)


def _tail_kernel(u_ref, v_ref, z_ref, ga_ref, gb_ref, ybt_ref, x_ref, gate_ref, wa_ref, wb_ref,
                 wo_ref, lng_ref, lnb_ref, ws_ref, bs_ref, fg_ref, o_ref, ya_ref, b_ref, m_ref, *,
                 chunk, gd, nc):
    d = wb_ref.shape[1]
    cols = [slice(k, k + nc) for k in range(0, d, nc)]

    gm = _gmlp_pieces(u_ref, v_ref, z_ref, lng_ref, lnb_ref, ws_ref, bs_ref, ya_ref, chunk, gd)
    per = -(-len(gm) // len(cols))
    for k, cs in enumerate(cols):
        b_ref[:, cs] = lax.dot_general(ybt_ref[...], wb_ref[:, cs], (((0,), (0,)), ((), ())),
                                       preferred_element_type=F32)
        for step in gm[k * per:(k + 1) * per]:
            step()

    def merge(cs, a):
        f, off = divmod(cs.start, ST)
        ga = ga_ref[f, :, off:off + nc].astype(F32)
        gb = gb_ref[f, :, off:off + nc].astype(F32)
        m_ref[:, cs] = (ga * a + gb * b_ref[:, cs]).astype(m_ref.dtype)

    pending = None
    for cs in cols:
        a = jnp.dot(ya_ref[...], wa_ref[:, cs], preferred_element_type=F32)
        if pending is not None:
            merge(*pending)
        pending = (cs, a)
    merge(*pending)

    ssq = None
    for cs in cols:
        out = jnp.dot(m_ref[...], wo_ref[:, cs], preferred_element_type=F32)
        r = x_ref[:, cs] + gate_ref[:, cs] * out
        o_ref[:, cs] = r
        part = jnp.sum(r * r, axis=-1, keepdims=True)
        ssq = part if ssq is None else ssq + part
    rstd = lax.rsqrt(ssq * (1.0 / d) + EPS)
    o_ref[...] = (o_ref[...] * rstd) * fg_ref[...]


def _tail(proj, ybt, x2, mod3, wa, wb, wo, ln_g, ln_b, w_s, bias_full, fgain, seq, tm=256):
    _, n_tt, n_ft, _, _ = proj.shape
    m, d = x2.shape
    chunk = w_s.shape[-1]
    assert ST % tm == 0 and tm % chunk == 0 and seq % tm == 0
    per = ST // tm
    tiles_per_seq = seq // tm
    pspec = lambda s: pl.BlockSpec((None, None, n_ft, tm, ST),
                                   lambda i, s=s: (s, i // per, 0, i % per, 0))
    rows = pl.BlockSpec((tm, d), lambda i: (i, 0))
    vec = pl.BlockSpec((1, d), lambda i: (0, 0))
    weight = pl.BlockSpec((d, d), lambda i: (0, 0), pipeline_mode=pl.Buffered(1))
    return pl.pallas_call(
        functools.partial(_tail_kernel, chunk=chunk, gd=d // N_GROUPS, nc=512),
        grid=(m // tm,),
        in_specs=[
            pspec(SLAB_U), pspec(SLAB_V), pspec(SLAB_Z), pspec(SLAB_GA), pspec(SLAB_GB),
            pl.BlockSpec((d, tm), lambda i: (0, i)), rows,
            pl.BlockSpec((None, 1, d), lambda i: (i // tiles_per_seq, 0, 2)),
            weight, weight, weight,
            vec, vec,
            pl.BlockSpec((N_GROUPS, chunk, chunk), lambda i: (0, 0, 0)),
            pl.BlockSpec((chunk, d), lambda i: (0, 0)),
            vec,
        ],
        out_specs=rows,
        out_shape=jax.ShapeDtypeStruct((m, d), F32),
        scratch_shapes=[pltpu.VMEM((tm, d), BF16), pltpu.VMEM((tm, d), F32),
                        pltpu.VMEM((tm, d), BF16)],
        compiler_params=_params("parallel"),
        name="tail",
    )(proj, proj, proj, proj, proj, ybt, x2, mod3, wa, wb, wo, ln_g, ln_b, w_s, bias_full, fgain)


def kernel(x, c, w_ada, b_ada, norm_gain, w_in, ln_v_gain, ln_v_bias, w_spatial, b_spatial,
           lambda_q1, lambda_k1, lambda_q2, lambda_k2, subln_gain, w_branch_a, w_branch_b,
           w_out, final_norm_gain):
    bsz, seq, d = x.shape
    depth = w_ada.shape[0]
    assert depth == 1, "single-layer trunk"
    dh = lambda_q1.shape[-1]
    assert d == N_HEADS * 2 * dh and w_in.shape[-1] == N_SLABS * d
    assert seq % ST == 0 and d % ST == 0 and ST % (2 * dh) == 0
    l = 0

    x2 = x.reshape(bsz * seq, d)
    mod = _ada(c, w_ada[l], b_ada[l])
    mod3 = mod.reshape(bsz, 1, 3 * d)

    proj = _inproj(x2, mod3, norm_gain[l].reshape(1, d), w_in[l],
                   LOG2E / math.sqrt(dh), seq)

    bias_full = jnp.repeat(jnp.transpose(b_spatial[l]), d // N_GROUPS, axis=1)

    slopes = jnp.asarray([2.0 ** (-8.0 * (i + 1) / N_HEADS) for i in range(N_HEADS)], F32)
    ybt, wa, wb, wo = _attn(proj, slopes, lambda_q1[l].reshape(1, dh), lambda_k1[l].reshape(1, dh),
                            lambda_q2[l].reshape(1, dh), lambda_k2[l].reshape(1, dh),
                            subln_gain[l].reshape(2 * dh, 1),
                            (w_branch_a[l], w_branch_b[l], w_out[l]), bsz, seq)

    out = _tail(proj, ybt, x2, mod3, wa, wb, wo, ln_v_gain[l].reshape(1, d),
                ln_v_bias[l].reshape(1, d),
                w_spatial[l], bias_full, final_norm_gain.reshape(1, d), seq)
    return out.reshape(bsz, seq, d)
```

```python
import functools
import math

import jax
import jax.numpy as jnp
from jax import lax
from jax.experimental import pallas as pl
from jax.experimental.pallas import tpu as pltpu

F32 = jnp.float32
BF16 = jnp.bfloat16

N_HEADS = 8
N_GROUPS = 8
N_SLABS = 9
SLAB_U, SLAB_V, SLAB_Z, SLAB_Q, SLAB_K, SLAB_BV, SLAB_BZ, SLAB_GA, SLAB_GB = range(N_SLABS)
ST = 512
EPS = 1e-6
SUBLN_EPS = 1e-5
LAMBDA_INIT = 0.8 - 0.6 * math.exp(-0.3 * 0)
LOG2E = 1.0 / math.log(2.0)

V7X_VMEM_BYTES = 64 * 1024 * 1024
VMEM_LIMIT = V7X_VMEM_BYTES * 7 // 8


def _sigmoid(x):
    return 0.5 * jnp.tanh(0.5 * x) + 0.5


def _silu(x):
    hx = 0.5 * x
    return hx * jnp.tanh(hx) + hx


def _gelu_tanh(x):
    c = math.sqrt(2.0 / math.pi)
    hx = 0.5 * x
    return hx * jnp.tanh(x * ((0.044715 * c) * (x * x) + c)) + hx


def _params(*sem):
    return pltpu.CompilerParams(dimension_semantics=sem, vmem_limit_bytes=VMEM_LIMIT)


def _ada_kernel(c_ref, w_ref, b_ref, o_ref):
    c = c_ref[...]
    o_ref[...] = jnp.dot(_silu(c), w_ref[...], preferred_element_type=F32) + b_ref[...]


def _ada(c, w_ada, b_ada, tn=1024):
    bsz, d = c.shape
    n = w_ada.shape[1]
    return pl.pallas_call(
        _ada_kernel,
        grid=(n // tn,),
        in_specs=[
            pl.BlockSpec((bsz, d), lambda j: (0, 0)),
            pl.BlockSpec((d, tn), lambda j: (0, j)),
            pl.BlockSpec((1, tn), lambda j: (0, j)),
        ],
        out_specs=pl.BlockSpec((bsz, tn), lambda j: (0, j)),
        out_shape=jax.ShapeDtypeStruct((bsz, n), F32),
        compiler_params=_params("parallel"),
        name="ada",
    )(c, w_ada, b_ada.reshape(1, n))


def _inproj_kernel(*refs, nx, rows, per, q_scale):
    x_refs = refs[:nx]
    g_ref, shift_ref, scale_ref, w_ref, o_ref, h_ref = refs[nx:]
    j = pl.program_id(1)

    @pl.when(j == 0)
    def _():
        gsc = g_ref[...] * (1.0 + scale_ref[...])
        sh = shift_ref[...]
        for k, x_ref in enumerate(x_refs):
            base = k * x_ref.shape[0]

            def body(r, carry, x_ref=x_ref, base=base):
                sl = pl.ds(pl.multiple_of(r * rows, rows), rows)
                x = x_ref[sl, :]
                rstd = lax.rsqrt(jnp.mean(x * x, axis=-1, keepdims=True) + EPS)
                h_ref[pl.ds(pl.multiple_of(base + r * rows, rows), rows), :] = (
                    ((x * rstd) * gsc + sh).astype(BF16))
                return carry

            lax.fori_loop(0, x_ref.shape[0] // rows, body, 0, unroll=4)

    n_tok, n_feat = o_ref.shape[0], o_ref.shape[1]
    slab = j // per
    epilogues = (
        ((SLAB_U, SLAB_V), _gelu_tanh),
        ((SLAB_Z,), _silu),
        ((SLAB_GA, SLAB_GB), _sigmoid),
        ((SLAB_Q,), lambda acc: acc * q_scale),
        ((SLAB_K,), lambda acc: acc),
    )
    for slabs, act in epilogues:
        @pl.when(functools.reduce(jnp.logical_or, [slab == s for s in slabs]))
        def _(act=act):
            acc = act(jnp.dot(h_ref[...], w_ref[...].astype(BF16), preferred_element_type=F32))
            for a in range(n_tok):
                for b in range(n_feat):
                    o_ref[a, b] = acc[a * ST:(a + 1) * ST, b * ST:(b + 1) * ST].astype(o_ref.dtype)

    for s_t, act in ((SLAB_BV, lambda acc: acc), (SLAB_BZ, _silu)):
        @pl.when(slab == s_t)
        def _(act=act):
            acc_t = act(lax.dot_general(w_ref[...].astype(BF16), h_ref[...], (((0,), (1,)), ((), ())),
                                        preferred_element_type=F32))
            for a in range(n_tok):
                for b in range(n_feat):
                    o_ref[a, b] = acc_t[b * ST:(b + 1) * ST, a * ST:(a + 1) * ST].astype(o_ref.dtype)


def _inproj(x2, mod3, gain, w_in, q_scale, seq, tm=1024, tn=1024):
    m, d = x2.shape
    n = w_in.shape[1]
    slab = n // N_SLABS
    per = slab // tn
    tiles_per_seq = seq // tm
    ni, nj = m // tm, n // tn
    nx = 4
    switch = [1 + k for k in range(nx)]
    assert tm % (nx * 32) == 0 and switch[-1] < nj

    def x_slice(k):
        return pl.BlockSpec(
            (tm // nx, d),
            lambda i, j: (jnp.minimum(i + (j >= switch[k]).astype(jnp.int32), ni - 1) * nx + k, 0))

    return pl.pallas_call(
        functools.partial(_inproj_kernel, nx=nx, rows=32, per=per, q_scale=q_scale),
        grid=(ni, nj),
        in_specs=[x_slice(k) for k in range(nx)] + [
            pl.BlockSpec((1, d), lambda i, j: (0, 0)),
            pl.BlockSpec((None, 1, d), lambda i, j: (i // tiles_per_seq, 0, 0)),
            pl.BlockSpec((None, 1, d), lambda i, j: (i // tiles_per_seq, 0, 1)),
            pl.BlockSpec((d, tn), lambda i, j: (0, j)),
        ],
        out_specs=pl.BlockSpec((None, tm // ST, tn // ST, ST, ST),
                               lambda i, j: (j // per, i, j % per, 0, 0)),
        out_shape=jax.ShapeDtypeStruct((N_SLABS, m // ST, slab // ST, ST, ST), BF16),
        scratch_shapes=[pltpu.VMEM((tm, d), BF16)],
        compiler_params=_params("parallel", "arbitrary"),
        name="inproj",
    )(*([x2] * nx), gain, mod3, mod3, w_in)


def _gmlp_pieces(u_ref, v_ref, z_ref, lng_ref, lnb_ref, ws_ref, bs_ref, o_ref, chunk, gd):
    n_ft, ta, _ = u_ref.shape
    d = n_ft * ST
    state = {}

    def layernorm(c):
        rows = pl.ds(c * chunk, chunk)
        gv = [v_ref[f, rows, :].astype(F32) for f in range(n_ft)]
        mu = sum(jnp.sum(t, axis=-1, keepdims=True) for t in gv) * (1.0 / d)
        xc = [t - mu for t in gv]
        var = sum(jnp.sum(t * t, axis=-1, keepdims=True) for t in xc) * (1.0 / d)
        rstd = lax.rsqrt(var + EPS)
        state[c] = [((xc[f] * rstd) * lng_ref[:, f * ST:(f + 1) * ST]
                     + lnb_ref[:, f * ST:(f + 1) * ST]).astype(BF16) for f in range(n_ft)]

    def group(g):
        row = lax.broadcasted_iota(jnp.int32, (chunk, chunk), 0)
        col = lax.broadcasted_iota(jnp.int32, (chunk, chunk), 1)
        f, off = divmod(g * gd, ST)
        ws = (ws_ref[g] * (row >= col).astype(F32)).astype(BF16)
        vn = jnp.concatenate([state[c][f][:, off:off + gd] for c in range(n_chunks)], axis=1)
        sv = jnp.dot(ws, vn, preferred_element_type=F32)
        for c in range(n_chunks):
            rows = pl.ds(c * chunk, chunk)
            svc = sv[:, c * gd:(c + 1) * gd] + bs_ref[:, g * gd:(g + 1) * gd]
            u = u_ref[f, rows, off:off + gd].astype(F32)
            z = z_ref[f, rows, off:off + gd].astype(F32)
            o_ref[rows, g * gd:(g + 1) * gd] = ((u * svc) * z).astype(o_ref.dtype)

    n_chunks = ta // chunk
    return ([functools.partial(layernorm, c) for c in range(n_chunks)]
            + [functools.partial(group, g) for g in range(N_GROUPS)])


def _attn_kernel(slopes_ref, lq1_ref, lk1_ref, lq2_ref, lk2_ref, q_ref, k_ref, vt_ref, zt_ref,
                 sg_ref, wa_ref, wb_ref, wo_ref, o_ref, wa_bf_ref, wb_bf_ref, wo_bf_ref,
                 bias_ref, vta_ref, s_ref, smax_ref, acc_ref, *, dh):
    nq, blk, dv = q_ref.shape
    for src, dst in ((wa_ref, wa_bf_ref), (wb_ref, wb_bf_ref), (wo_ref, wo_bf_ref)):
        dst[...] = src[...].astype(dst.dtype)
    slope = slopes_ref[pl.program_id(0)] * LOG2E

    @pl.when(pl.program_id(1) == 0)
    def _():
        s_pos = lax.broadcasted_iota(jnp.int32, (blk, blk), 0)
        t_pos = lax.broadcasted_iota(jnp.int32, (blk, blk), 1)
        bias = slope * (s_pos - t_pos).astype(F32)
        bias_ref[0] = bias
        bias_ref[1] = jnp.where(s_pos <= t_pos, bias, -jnp.inf)

    for j in range(nq):
        vta_ref[j, :dv, :] = vt_ref[j]
        vta_ref[j, dv:, :] = jnp.ones((vta_ref.shape[1] - dv, blk), vta_ref.dtype)

    lam_init = jnp.float32(LAMBDA_INIT)
    lam = (jnp.exp(jnp.sum(lq1_ref[...] * lk1_ref[...], axis=-1, keepdims=True))
           - jnp.exp(jnp.sum(lq2_ref[...] * lk2_ref[...], axis=-1, keepdims=True))
           + lam_init)
    gain = sg_ref[...] * (1.0 - lam_init)

    def scores(j, n):
        slot = j % 2
        bias = bias_ref[int(j == n)]
        for c in range(2):
            q = q_ref[n, :, c * dh:(c + 1) * dh]
            k = k_ref[j, :, c * dh:(c + 1) * dh]
            s = lax.dot_general(k, q, (((1,), (1,)), ((), ())), preferred_element_type=F32)
            s = s + bias
            s_ref[slot, c] = s
            smax_ref[slot, c] = jnp.max(s, axis=0, keepdims=True)

    def consume(j, n, m):
        slot = j % 2
        vta = vta_ref[j]
        off = slope * float((j - n) * blk)
        for c in range(2):
            m_new = smax_ref[slot, c] + off
            if m[c] is not None:
                m_new = jnp.maximum(m[c], m_new)
            p = jnp.exp2(s_ref[slot, c] - (m_new - off)).astype(vta.dtype)
            pv = jnp.dot(vta, p, preferred_element_type=F32)
            if m[c] is None:
                acc_ref[c] = pv
            else:
                acc_ref[c] = jnp.exp2(m[c] - m_new) * acc_ref[c] + pv
            m[c] = m_new

    hb = blk // 2
    nt = (((1,), (1,)), ((), ()))

    def scores_diag(n):
        slot = n % 2
        for c in range(2):
            q = q_ref[n, :, c * dh:(c + 1) * dh]
            k = k_ref[n, :, c * dh:(c + 1) * dh]
            sa = lax.dot_general(k[:hb], q, nt, preferred_element_type=F32) + bias_ref[1, :hb, :]
            sb = (lax.dot_general(k[hb:], q[hb:], nt, preferred_element_type=F32)
                  + bias_ref[1, hb:, hb:])
            s_ref[slot, c, :hb, :] = sa
            s_ref[slot, c, hb:, hb:] = sb
            ma = jnp.max(sa, axis=0, keepdims=True)
            smax_ref[slot, c, :, :hb] = ma[:, :hb]
            smax_ref[slot, c, :, hb:] = jnp.maximum(ma[:, hb:], jnp.max(sb, axis=0, keepdims=True))

    def consume_diag(n, m):
        slot = n % 2
        vta = vta_ref[n]
        for c in range(2):
            m_new = smax_ref[slot, c]
            if m[c] is not None:
                m_new = jnp.maximum(m[c], m_new)
            pa = jnp.exp2(s_ref[slot, c, :hb, :] - m_new).astype(vta.dtype)
            pb = jnp.exp2(s_ref[slot, c, hb:, hb:] - m_new[:, hb:]).astype(vta.dtype)
            pva = jnp.dot(vta[:, :hb], pa, preferred_element_type=F32)
            pvb = jnp.dot(vta[:, hb:], pb, preferred_element_type=F32)
            if m[c] is None:
                acc_ref[c, :, :hb] = pva[:, :hb]
                acc_ref[c, :, hb:] = pva[:, hb:] + pvb
            else:
                alpha = jnp.exp2(m[c] - m_new)
                acc_ref[c, :, :hb] = alpha[:, :hb] * acc_ref[c, :, :hb] + pva[:, :hb]
                acc_ref[c, :, hb:] = alpha[:, hb:] * acc_ref[c, :, hb:] + (pva[:, hb:] + pvb)
            m[c] = m_new

    for n in range(nq):
        m = [None, None]
        if n == 0:
            scores_diag(0)
        else:
            scores(0, n)
        for j in range(n):
            if j + 1 == n:
                scores_diag(n)
            else:
                scores(j + 1, n)
            consume(j, n, m)
        consume_diag(n, m)

        a0, a1 = acc_ref[0], acc_ref[1]
        r0 = 1.0 / a0[dv:dv + 1]
        r1 = lam * (1.0 / a1[dv:dv + 1])
        o = a0[:dv] * r0 - a1[:dv] * r1
        o = o * lax.rsqrt(jnp.mean(o * o, axis=0, keepdims=True) + SUBLN_EPS)
        o = (o * gain) * zt_ref[n].astype(F32)
        o_ref[:, n * blk:(n + 1) * blk] = o.astype(o_ref.dtype)


def _attn(proj, slopes, lq1, lk1, lq2, lk2, subln_g, weights, bsz, seq):
    _, n_tt, n_ft, _, _ = proj.shape
    m, d = n_tt * ST, n_ft * ST
    dv = d // N_HEADS
    dh = dv // 2
    nq = seq // ST
    hp = ST // dv
    ones_rows = 16
    tok_major = lambda s: pl.BlockSpec((None, nq, None, ST, dv),
                                       lambda h, b, s=s: (s, b, h // hp, 0, h % hp))
    feat_major = lambda s: pl.BlockSpec((None, nq, None, dv, ST),
                                        lambda h, b, s=s: (s, b, h // hp, h % hp, 0))
    vec = pl.BlockSpec((1, dh), lambda h, b: (0, 0))
    w_rows = d // (N_HEADS * bsz)
    assert w_rows * N_HEADS * bsz == d and w_rows % 16 == 0
    w_slice = pl.BlockSpec((w_rows, d), lambda h, b: (h * bsz + b, 0))
    return pl.pallas_call(
        functools.partial(_attn_kernel, dh=dh),
        grid=(N_HEADS, bsz),
        in_specs=[
            pl.BlockSpec(memory_space=pltpu.SMEM),
            vec, vec, vec, vec,
            tok_major(SLAB_Q), tok_major(SLAB_K), feat_major(SLAB_BV), feat_major(SLAB_BZ),
            pl.BlockSpec((dv, 1), lambda h, b: (0, 0)),
            w_slice, w_slice, w_slice,
        ],
        out_specs=[pl.BlockSpec((dv, seq), lambda h, b: (h, b)), w_slice, w_slice, w_slice],
        out_shape=[jax.ShapeDtypeStruct((d, m), BF16)] + [jax.ShapeDtypeStruct((d, d), BF16)] * 3,
        scratch_shapes=[
            pltpu.VMEM((2, ST, ST), F32),
            pltpu.VMEM((nq, dv + ones_rows, ST), BF16),
            pltpu.VMEM((2, 2, ST, ST), F32),
            pltpu.VMEM((2, 2, 1, ST), F32),
            pltpu.VMEM((2, dv + ones_rows, ST), F32),
        ],
        compiler_params=_params("arbitrary", "arbitrary"),
        name="attn",
    )(slopes, lq1, lk1, lq2, lk2, proj, proj, proj, proj, subln_g, *weights)


def _tail_kernel(u_ref, v_ref, z_ref, ga_ref, gb_ref, ybt_ref, x_ref, gate_ref, wa_ref, wb_ref,
                 wo_ref, lng_ref, lnb_ref, ws_ref, bs_ref, fg_ref, o_ref, ya_ref, b_ref, m_ref, *,
                 chunk, gd, nc):
    d = wb_ref.shape[1]
    cols = [slice(k, k + nc) for k in range(0, d, nc)]

    gm = _gmlp_pieces(u_ref, v_ref, z_ref, lng_ref, lnb_ref, ws_ref, bs_ref, ya_ref, chunk, gd)
    per = -(-len(gm) // len(cols))
    for k, cs in enumerate(cols):
        b_ref[:, cs] = lax.dot_general(ybt_ref[...], wb_ref[:, cs], (((0,), (0,)), ((), ())),
                                       preferred_element_type=F32)
        for step in gm[k * per:(k + 1) * per]:
            step()

    def merge(cs, a):
        f, off = divmod(cs.start, ST)
        ga = ga_ref[f, :, off:off + nc].astype(F32)
        gb = gb_ref[f, :, off:off + nc].astype(F32)
        m_ref[:, cs] = (ga * a + gb * b_ref[:, cs]).astype(m_ref.dtype)

    pending = None
    for cs in cols:
        a = jnp.dot(ya_ref[...], wa_ref[:, cs], preferred_element_type=F32)
        if pending is not None:
            merge(*pending)
        pending = (cs, a)
    merge(*pending)

    ssq = None
    for cs in cols:
        out = jnp.dot(m_ref[...], wo_ref[:, cs], preferred_element_type=F32)
        r = x_ref[:, cs] + gate_ref[:, cs] * out
        o_ref[:, cs] = r
        part = jnp.sum(r * r, axis=-1, keepdims=True)
        ssq = part if ssq is None else ssq + part
    rstd = lax.rsqrt(ssq * (1.0 / d) + EPS)
    o_ref[...] = (o_ref[...] * rstd) * fg_ref[...]


def _tail(proj, ybt, x2, mod3, wa, wb, wo, ln_g, ln_b, w_s, bias_full, fgain, seq, tm=256):
    _, n_tt, n_ft, _, _ = proj.shape
    m, d = x2.shape
    chunk = w_s.shape[-1]
    assert ST % tm == 0 and tm % chunk == 0 and seq % tm == 0
    per = ST // tm
    tiles_per_seq = seq // tm
    pspec = lambda s: pl.BlockSpec((None, None, n_ft, tm, ST),
                                   lambda i, s=s: (s, i // per, 0, i % per, 0))
    rows = pl.BlockSpec((tm, d), lambda i: (i, 0))
    vec = pl.BlockSpec((1, d), lambda i: (0, 0))
    weight = pl.BlockSpec((d, d), lambda i: (0, 0), pipeline_mode=pl.Buffered(1))
    return pl.pallas_call(
        functools.partial(_tail_kernel, chunk=chunk, gd=d // N_GROUPS, nc=512),
        grid=(m // tm,),
        in_specs=[
            pspec(SLAB_U), pspec(SLAB_V), pspec(SLAB_Z), pspec(SLAB_GA), pspec(SLAB_GB),
            pl.BlockSpec((d, tm), lambda i: (0, i)), rows,
            pl.BlockSpec((None, 1, d), lambda i: (i // tiles_per_seq, 0, 2)),
            weight, weight, weight,
            vec, vec,
            pl.BlockSpec((N_GROUPS, chunk, chunk), lambda i: (0, 0, 0)),
            pl.BlockSpec((chunk, d), lambda i: (0, 0)),
            vec,
        ],
        out_specs=rows,
        out_shape=jax.ShapeDtypeStruct((m, d), F32),
        scratch_shapes=[pltpu.VMEM((tm, d), BF16), pltpu.VMEM((tm, d), F32),
                        pltpu.VMEM((tm, d), BF16)],
        compiler_params=_params("parallel"),
        name="tail",
    )(proj, proj, proj, proj, proj, ybt, x2, mod3, wa, wb, wo, ln_g, ln_b, w_s, bias_full, fgain)


def kernel(x, c, w_ada, b_ada, norm_gain, w_in, ln_v_gain, ln_v_bias, w_spatial, b_spatial,
           lambda_q1, lambda_k1, lambda_q2, lambda_k2, subln_gain, w_branch_a, w_branch_b,
           w_out, final_norm_gain):
    bsz, seq, d = x.shape
    depth = w_ada.shape[0]
    assert depth == 1, "single-layer trunk"
    dh = lambda_q1.shape[-1]
    assert d == N_HEADS * 2 * dh and w_in.shape[-1] == N_SLABS * d
    assert seq % ST == 0 and d % ST == 0 and ST % (2 * dh) == 0
    l = 0

    x2 = x.reshape(bsz * seq, d)
    mod = _ada(c, w_ada[l], b_ada[l])
    mod3 = mod.reshape(bsz, 1, 3 * d)

    proj = _inproj(x2, mod3, norm_gain[l].reshape(1, d), w_in[l],
                   LOG2E / math.sqrt(dh), seq)

    bias_full = jnp.repeat(jnp.transpose(b_spatial[l]), d // N_GROUPS, axis=1)

    slopes = jnp.asarray([2.0 ** (-8.0 * (i + 1) / N_HEADS) for i in range(N_HEADS)], F32)
    ybt, wa, wb, wo = _attn(proj, slopes, lambda_q1[l].reshape(1, dh), lambda_k1[l].reshape(1, dh),
                            lambda_q2[l].reshape(1, dh), lambda_k2[l].reshape(1, dh),
                            subln_gain[l].reshape(2 * dh, 1),
                            (w_branch_a[l], w_branch_b[l], w_out[l]), bsz, seq)

    out = _tail(proj, ybt, x2, mod3, wa, wb, wo, ln_v_gain[l].reshape(1, d),
                ln_v_bias[l].reshape(1, d),
                w_spatial[l], bias_full, final_norm_gain.reshape(1, d), seq)
    return out.reshape(bsz, seq, d)
```

```python
import functools
import math

import jax
import jax.numpy as jnp
from jax import lax
from jax.experimental import pallas as pl
from jax.experimental.pallas import tpu as pltpu

F32 = jnp.float32
BF16 = jnp.bfloat16

N_HEADS = 8
N_GROUPS = 8
N_SLABS = 9
SLAB_U, SLAB_V, SLAB_Z, SLAB_Q, SLAB_K, SLAB_BV, SLAB_BZ, SLAB_GA, SLAB_GB = range(N_SLABS)
ST = 512
EPS = 1e-6
SUBLN_EPS = 1e-5
LAMBDA_INIT = 0.8 - 0.6 * math.exp(-0.3 * 0)
LOG2E = 1.0 / math.log(2.0)

V7X_VMEM_BYTES = 64 * 1024 * 1024
VMEM_LIMIT = V7X_VMEM_BYTES * 7 // 8


def _sigmoid(x):
    return 0.5 * jnp.tanh(0.5 * x) + 0.5


def _silu(x):
    hx = 0.5 * x
    return hx * jnp.tanh(hx) + hx


def _gelu_tanh(x):
    c = math.sqrt(2.0 / math.pi)
    hx = 0.5 * x
    return hx * jnp.tanh(x * ((0.044715 * c) * (x * x) + c)) + hx


def _params(*sem):
    return pltpu.CompilerParams(dimension_semantics=sem, vmem_limit_bytes=VMEM_LIMIT)


def _ada_kernel(c_ref, w_ref, b_ref, o_ref):
    c = c_ref[...]
    o_ref[...] = jnp.dot(_silu(c), w_ref[...], preferred_element_type=F32) + b_ref[...]


def _ada(c, w_ada, b_ada, tn=1024):
    bsz, d = c.shape
    n = w_ada.shape[1]
    return pl.pallas_call(
        _ada_kernel,
        grid=(n // tn,),
        in_specs=[
            pl.BlockSpec((bsz, d), lambda j: (0, 0)),
            pl.BlockSpec((d, tn), lambda j: (0, j)),
            pl.BlockSpec((1, tn), lambda j: (0, j)),
        ],
        out_specs=pl.BlockSpec((bsz, tn), lambda j: (0, j)),
        out_shape=jax.ShapeDtypeStruct((bsz, n), F32),
        compiler_params=_params("parallel"),
        name="ada",
    )(c, w_ada, b_ada.reshape(1, n))


def _inproj_kernel(*refs, nx, rows, per, q_scale):
    x_refs = refs[:nx]
    g_ref, shift_ref, scale_ref, w_ref, o_ref, h_ref = refs[nx:]
    j = pl.program_id(1)

    @pl.when(j == 0)
    def _():
        gsc = g_ref[...] * (1.0 + scale_ref[...])
        sh = shift_ref[...]
        for k, x_ref in enumerate(x_refs):
            base = k * x_ref.shape[0]

            def body(r, carry, x_ref=x_ref, base=base):
                sl = pl.ds(pl.multiple_of(r * rows, rows), rows)
                x = x_ref[sl, :]
                rstd = lax.rsqrt(jnp.mean(x * x, axis=-1, keepdims=True) + EPS)
                h_ref[pl.ds(pl.multiple_of(base + r * rows, rows), rows), :] = (
                    ((x * rstd) * gsc + sh).astype(BF16))
                return carry

            lax.fori_loop(0, x_ref.shape[0] // rows, body, 0, unroll=4)

    n_tok, n_feat = o_ref.shape[0], o_ref.shape[1]
    slab = j // per
    epilogues = (
        ((SLAB_U, SLAB_V), _gelu_tanh),
        ((SLAB_Z,), _silu),
        ((SLAB_GA, SLAB_GB), _sigmoid),
        ((SLAB_Q,), lambda acc: acc * q_scale),
        ((SLAB_K,), lambda acc: acc),
    )
    for slabs, act in epilogues:
        @pl.when(functools.reduce(jnp.logical_or, [slab == s for s in slabs]))
        def _(act=act):
            acc = act(jnp.dot(h_ref[...], w_ref[...].astype(BF16), preferred_element_type=F32))
            for a in range(n_tok):
                for b in range(n_feat):
                    o_ref[a, b] = acc[a * ST:(a + 1) * ST, b * ST:(b + 1) * ST].astype(o_ref.dtype)

    for s_t, act in ((SLAB_BV, lambda acc: acc), (SLAB_BZ, _silu)):
        @pl.when(slab == s_t)
        def _(act=act):
            acc_t = act(lax.dot_general(w_ref[...].astype(BF16), h_ref[...], (((0,), (1,)), ((), ())),
                                        preferred_element_type=F32))
            for a in range(n_tok):
                for b in range(n_feat):
                    o_ref[a, b] = acc_t[b * ST:(b + 1) * ST, a * ST:(a + 1) * ST].astype(o_ref.dtype)


def _inproj(x2, mod3, gain, w_in, q_scale, seq, tm=1024, tn=1024):
    m, d = x2.shape
    n = w_in.shape[1]
    slab = n // N_SLABS
    per = slab // tn
    tiles_per_seq = seq // tm
    ni, nj = m // tm, n // tn
    nx = 4
    switch = [1 + k for k in range(nx)]
    assert tm % (nx * 32) == 0 and switch[-1] < nj

    def x_slice(k):
        return pl.BlockSpec(
            (tm // nx, d),
            lambda i, j: (jnp.minimum(i + (j >= switch[k]).astype(jnp.int32), ni - 1) * nx + k, 0))

    return pl.pallas_call(
        functools.partial(_inproj_kernel, nx=nx, rows=32, per=per, q_scale=q_scale),
        grid=(ni, nj),
        in_specs=[x_slice(k) for k in range(nx)] + [
            pl.BlockSpec((1, d), lambda i, j: (0, 0)),
            pl.BlockSpec((None, 1, d), lambda i, j: (i // tiles_per_seq, 0, 0)),
            pl.BlockSpec((None, 1, d), lambda i, j: (i // tiles_per_seq, 0, 1)),
            pl.BlockSpec((d, tn), lambda i, j: (0, j)),
        ],
        out_specs=pl.BlockSpec((None, tm // ST, tn // ST, ST, ST),
                               lambda i, j: (j // per, i, j % per, 0, 0)),
        out_shape=jax.ShapeDtypeStruct((N_SLABS, m // ST, slab // ST, ST, ST), BF16),
        scratch_shapes=[pltpu.VMEM((tm, d), BF16)],
        compiler_params=_params("parallel", "arbitrary"),
        name="inproj",
    )(*([x2] * nx), gain, mod3, mod3, w_in)


def _gmlp_pieces(u_ref, v_ref, z_ref, lng_ref, lnb_ref, ws_ref, bs_ref, o_ref, chunk, gd):
    n_ft, ta, _ = u_ref.shape
    d = n_ft * ST
    state = {}

    def layernorm(c):
        rows = pl.ds(c * chunk, chunk)
        gv = [v_ref[f, rows, :].astype(F32) for f in range(n_ft)]
        mu = sum(jnp.sum(t, axis=-1, keepdims=True) for t in gv) * (1.0 / d)
        xc = [t - mu for t in gv]
        var = sum(jnp.sum(t * t, axis=-1, keepdims=True) for t in xc) * (1.0 / d)
        rstd = lax.rsqrt(var + EPS)
        state[c] = [((xc[f] * rstd) * lng_ref[:, f * ST:(f + 1) * ST]
                     + lnb_ref[:, f * ST:(f + 1) * ST]).astype(BF16) for f in range(n_ft)]

    def group(c, g):
        rows = pl.ds(c * chunk, chunk)
        row = lax.broadcasted_iota(jnp.int32, (chunk, chunk), 0)
        col = lax.broadcasted_iota(jnp.int32, (chunk, chunk), 1)
        f, off = divmod(g * gd, ST)
        ws = (ws_ref[g] * (row >= col).astype(F32)).astype(BF16)
        sv = jnp.dot(ws, state[c][f][:, off:off + gd], preferred_element_type=F32)
        sv = sv + bs_ref[:, g * gd:(g + 1) * gd]
        u = u_ref[f, rows, off:off + gd].astype(F32)
        z = z_ref[f, rows, off:off + gd].astype(F32)
        o_ref[rows, g * gd:(g + 1) * gd] = ((u * sv) * z).astype(o_ref.dtype)

    steps = []
    for c in range(ta // chunk):
        steps.append(functools.partial(layernorm, c))
        steps += [functools.partial(group, c, g) for g in range(N_GROUPS)]
    return steps


def _attn_kernel(slopes_ref, lq1_ref, lk1_ref, lq2_ref, lk2_ref, q_ref, k_ref, vt_ref, zt_ref,
                 sg_ref, wa_ref, wb_ref, wo_ref, o_ref, wa_bf_ref, wb_bf_ref, wo_bf_ref,
                 bias_ref, vta_ref, s_ref, smax_ref, acc_ref, *, dh):
    nq, blk, dv = q_ref.shape
    for src, dst in ((wa_ref, wa_bf_ref), (wb_ref, wb_bf_ref), (wo_ref, wo_bf_ref)):
        dst[...] = src[...].astype(dst.dtype)
    slope = slopes_ref[pl.program_id(0)] * LOG2E

    @pl.when(pl.program_id(1) == 0)
    def _():
        s_pos = lax.broadcasted_iota(jnp.int32, (blk, blk), 0)
        t_pos = lax.broadcasted_iota(jnp.int32, (blk, blk), 1)
        bias = slope * (s_pos - t_pos).astype(F32)
        bias_ref[0] = bias
        bias_ref[1] = jnp.where(s_pos <= t_pos, bias, -jnp.inf)

    for j in range(nq):
        vta_ref[j, :dv, :] = vt_ref[j]
        vta_ref[j, dv:, :] = jnp.ones((vta_ref.shape[1] - dv, blk), vta_ref.dtype)

    lam_init = jnp.float32(LAMBDA_INIT)
    lam = (jnp.exp(jnp.sum(lq1_ref[...] * lk1_ref[...], axis=-1, keepdims=True))
           - jnp.exp(jnp.sum(lq2_ref[...] * lk2_ref[...], axis=-1, keepdims=True))
           + lam_init)
    gain = sg_ref[...] * (1.0 - lam_init)

    def scores(j, n):
        slot = j % 2
        bias = bias_ref[int(j == n)]
        for c in range(2):
            q = q_ref[n, :, c * dh:(c + 1) * dh]
            k = k_ref[j, :, c * dh:(c + 1) * dh]
            s = lax.dot_general(k, q, (((1,), (1,)), ((), ())), preferred_element_type=F32)
            s = s + bias
            s_ref[slot, c] = s
            smax_ref[slot, c] = jnp.max(s, axis=0, keepdims=True)

    def consume(j, n, m):
        slot = j % 2
        vta = vta_ref[j]
        off = slope * float((j - n) * blk)
        for c in range(2):
            m_new = smax_ref[slot, c] + off
            if m[c] is not None:
                m_new = jnp.maximum(m[c], m_new)
            p = jnp.exp2((s_ref[slot, c] - (m_new - off)).astype(vta.dtype))
            pv = jnp.dot(vta, p, preferred_element_type=F32)
            if m[c] is None:
                acc_ref[c] = pv
            else:
                acc_ref[c] = jnp.exp2(m[c] - m_new) * acc_ref[c] + pv
            m[c] = m_new

    hb = blk // 2
    nt = (((1,), (1,)), ((), ()))

    def scores_diag(n):
        slot = n % 2
        for c in range(2):
            q = q_ref[n, :, c * dh:(c + 1) * dh]
            k = k_ref[n, :, c * dh:(c + 1) * dh]
            sa = lax.dot_general(k[:hb], q, nt, preferred_element_type=F32) + bias_ref[1, :hb, :]
            sb = (lax.dot_general(k[hb:], q[hb:], nt, preferred_element_type=F32)
                  + bias_ref[1, hb:, hb:])
            s_ref[slot, c, :hb, :] = sa
            s_ref[slot, c, hb:, hb:] = sb
            ma = jnp.max(sa, axis=0, keepdims=True)
            smax_ref[slot, c, :, :hb] = ma[:, :hb]
            smax_ref[slot, c, :, hb:] = jnp.maximum(ma[:, hb:], jnp.max(sb, axis=0, keepdims=True))

    def consume_diag(n, m):
        slot = n % 2
        vta = vta_ref[n]
        for c in range(2):
            m_new = smax_ref[slot, c]
            if m[c] is not None:
                m_new = jnp.maximum(m[c], m_new)
            pa = jnp.exp2((s_ref[slot, c, :hb, :] - m_new).astype(vta.dtype))
            pb = jnp.exp2((s_ref[slot, c, hb:, hb:] - m_new[:, hb:]).astype(vta.dtype))
            pva = jnp.dot(vta[:, :hb], pa, preferred_element_type=F32)
            pvb = jnp.dot(vta[:, hb:], pb, preferred_element_type=F32)
            if m[c] is None:
                acc_ref[c, :, :hb] = pva[:, :hb]
                acc_ref[c, :, hb:] = pva[:, hb:] + pvb
            else:
                alpha = jnp.exp2(m[c] - m_new)
                acc_ref[c, :, :hb] = alpha[:, :hb] * acc_ref[c, :, :hb] + pva[:, :hb]
                acc_ref[c, :, hb:] = alpha[:, hb:] * acc_ref[c, :, hb:] + (pva[:, hb:] + pvb)
            m[c] = m_new

    for n in range(nq):
        m = [None, None]
        if n == 0:
            scores_diag(0)
        else:
            scores(0, n)
        for j in range(n):
            if j + 1 == n:
                scores_diag(n)
            else:
                scores(j + 1, n)
            consume(j, n, m)
        consume_diag(n, m)

        a0, a1 = acc_ref[0], acc_ref[1]
        r0 = 1.0 / a0[dv:dv + 1]
        r1 = lam * (1.0 / a1[dv:dv + 1])
        o = a0[:dv] * r0 - a1[:dv] * r1
        o = o * lax.rsqrt(jnp.mean(o * o, axis=0, keepdims=True) + SUBLN_EPS)
        o = (o * gain) * zt_ref[n].astype(F32)
        o_ref[:, n * blk:(n + 1) * blk] = o.astype(o_ref.dtype)


def _attn(proj, slopes, lq1, lk1, lq2, lk2, subln_g, weights, bsz, seq):
    _, n_tt, n_ft, _, _ = proj.shape
    m, d = n_tt * ST, n_ft * ST
    dv = d // N_HEADS
    dh = dv // 2
    nq = seq // ST
    hp = ST // dv
    ones_rows = 16
    tok_major = lambda s: pl.BlockSpec((None, nq, None, ST, dv),
                                       lambda h, b, s=s: (s, b, h // hp, 0, h % hp))
    feat_major = lambda s: pl.BlockSpec((None, nq, None, dv, ST),
                                        lambda h, b, s=s: (s, b, h // hp, h % hp, 0))
    vec = pl.BlockSpec((1, dh), lambda h, b: (0, 0))
    w_rows = d // (N_HEADS * bsz)
    assert w_rows * N_HEADS * bsz == d and w_rows % 16 == 0
    w_slice = pl.BlockSpec((w_rows, d), lambda h, b: (h * bsz + b, 0))
    return pl.pallas_call(
        functools.partial(_attn_kernel, dh=dh),
        grid=(N_HEADS, bsz),
        in_specs=[
            pl.BlockSpec(memory_space=pltpu.SMEM),
            vec, vec, vec, vec,
            tok_major(SLAB_Q), tok_major(SLAB_K), feat_major(SLAB_BV), feat_major(SLAB_BZ),
            pl.BlockSpec((dv, 1), lambda h, b: (0, 0)),
            w_slice, w_slice, w_slice,
        ],
        out_specs=[pl.BlockSpec((dv, seq), lambda h, b: (h, b)), w_slice, w_slice, w_slice],
        out_shape=[jax.ShapeDtypeStruct((d, m), BF16)] + [jax.ShapeDtypeStruct((d, d), BF16)] * 3,
        scratch_shapes=[
            pltpu.VMEM((2, ST, ST), F32),
            pltpu.VMEM((nq, dv + ones_rows, ST), BF16),
            pltpu.VMEM((2, 2, ST, ST), F32),
            pltpu.VMEM((2, 2, 1, ST), F32),
            pltpu.VMEM((2, dv + ones_rows, ST), F32),
        ],
        compiler_params=_params("arbitrary", "arbitrary"),
        name="attn",
    )(slopes, lq1, lk1, lq2, lk2, proj, proj, proj, proj, subln_g, *weights)


def _tail_kernel(u_ref, v_ref, z_ref, ga_ref, gb_ref, ybt_ref, x_ref, gate_ref, wa_ref, wb_ref,
                 wo_ref, lng_ref, lnb_ref, ws_ref, bs_ref, fg_ref, o_ref, ya_ref, b_ref, m_ref, *,
                 chunk, gd, nc):
    d = wb_ref.shape[1]
    cols = [slice(k, k + nc) for k in range(0, d, nc)]

    gm = _gmlp_pieces(u_ref, v_ref, z_ref, lng_ref, lnb_ref, ws_ref, bs_ref, ya_ref, chunk, gd)
    per = -(-len(gm) // len(cols))
    for k, cs in enumerate(cols):
        b_ref[:, cs] = lax.dot_general(ybt_ref[...], wb_ref[:, cs], (((0,), (0,)), ((), ())),
                                       preferred_element_type=F32)
        for step in gm[k * per:(k + 1) * per]:
            step()

    def merge(cs, a):
        f, off = divmod(cs.start, ST)
        ga = ga_ref[f, :, off:off + nc].astype(F32)
        gb = gb_ref[f, :, off:off + nc].astype(F32)
        m_ref[:, cs] = (ga * a + gb * b_ref[:, cs]).astype(m_ref.dtype)

    pending = None
    for cs in cols:
        a = jnp.dot(ya_ref[...], wa_ref[:, cs], preferred_element_type=F32)
        if pending is not None:
            merge(*pending)
        pending = (cs, a)
    merge(*pending)

    ssq = None
    for cs in cols:
        out = jnp.dot(m_ref[...], wo_ref[:, cs], preferred_element_type=F32)
        r = x_ref[:, cs] + gate_ref[:, cs] * out
        o_ref[:, cs] = r
        part = jnp.sum(r * r, axis=-1, keepdims=True)
        ssq = part if ssq is None else ssq + part
    rstd = lax.rsqrt(ssq * (1.0 / d) + EPS)
    o_ref[...] = (o_ref[...] * rstd) * fg_ref[...]


def _tail(proj, ybt, x2, mod3, wa, wb, wo, ln_g, ln_b, w_s, bias_full, fgain, seq, tm=256):
    _, n_tt, n_ft, _, _ = proj.shape
    m, d = x2.shape
    chunk = w_s.shape[-1]
    assert ST % tm == 0 and tm % chunk == 0 and seq % tm == 0
    per = ST // tm
    tiles_per_seq = seq // tm
    pspec = lambda s: pl.BlockSpec((None, None, n_ft, tm, ST),
                                   lambda i, s=s: (s, i // per, 0, i % per, 0))
    rows = pl.BlockSpec((tm, d), lambda i: (i, 0))
    vec = pl.BlockSpec((1, d), lambda i: (0, 0))
    weight = pl.BlockSpec((d, d), lambda i: (0, 0), pipeline_mode=pl.Buffered(1))
    return pl.pallas_call(
        functools.partial(_tail_kernel, chunk=chunk, gd=d // N_GROUPS, nc=512),
        grid=(m // tm,),
        in_specs=[
            pspec(SLAB_U), pspec(SLAB_V), pspec(SLAB_Z), pspec(SLAB_GA), pspec(SLAB_GB),
            pl.BlockSpec((d, tm), lambda i: (0, i)), rows,
            pl.BlockSpec((None, 1, d), lambda i: (i // tiles_per_seq, 0, 2)),
            weight, weight, weight,
            vec, vec,
            pl.BlockSpec((N_GROUPS, chunk, chunk), lambda i: (0, 0, 0)),
            pl.BlockSpec((chunk, d), lambda i: (0, 0)),
            vec,
        ],
        out_specs=rows,
        out_shape=jax.ShapeDtypeStruct((m, d), F32),
        scratch_shapes=[pltpu.VMEM((tm, d), BF16), pltpu.VMEM((tm, d), F32),
                        pltpu.VMEM((tm, d), BF16)],
        compiler_params=_params("parallel"),
        name="tail",
    )(proj, proj, proj, proj, proj, ybt, x2, mod3, wa, wb, wo, ln_g, ln_b, w_s, bias_full, fgain)


def kernel(x, c, w_ada, b_ada, norm_gain, w_in, ln_v_gain, ln_v_bias, w_spatial, b_spatial,
           lambda_q1, lambda_k1, lambda_q2, lambda_k2, subln_gain, w_branch_a, w_branch_b,
           w_out, final_norm_gain):
    bsz, seq, d = x.shape
    depth = w_ada.shape[0]
    assert depth == 1, "single-layer trunk"
    dh = lambda_q1.shape[-1]
    assert d == N_HEADS * 2 * dh and w_in.shape[-1] == N_SLABS * d
    assert seq % ST == 0 and d % ST == 0 and ST % (2 * dh) == 0
    l = 0

    x2 = x.reshape(bsz * seq, d)
    mod = _ada(c, w_ada[l], b_ada[l])
    mod3 = mod.reshape(bsz, 1, 3 * d)

    proj = _inproj(x2, mod3, norm_gain[l].reshape(1, d), w_in[l],
                   LOG2E / math.sqrt(dh), seq)

    bias_full = jnp.repeat(jnp.transpose(b_spatial[l]), d // N_GROUPS, axis=1)

    slopes = jnp.asarray([2.0 ** (-8.0 * (i + 1) / N_HEADS) for i in range(N_HEADS)], F32)
    ybt, wa, wb, wo = _attn(proj, slopes, lambda_q1[l].reshape(1, dh), lambda_k1[l].reshape(1, dh),
                            lambda_q2[l].reshape(1, dh), lambda_k2[l].reshape(1, dh),
                            subln_gain[l].reshape(2 * dh, 1),
                            (w_branch_a[l], w_branch_b[l], w_out[l]), bsz, seq)

    out = _tail(proj, ybt, x2, mod3, wa, wb, wo, ln_v_gain[l].reshape(1, d),
                ln_v_bias[l].reshape(1, d),
                w_spatial[l], bias_full, final_norm_gain.reshape(1, d), seq)
    return out.reshape(bsz, seq, d)
```

```python
import functools
import math

import jax
import jax.numpy as jnp
from jax import lax
from jax.experimental import pallas as pl
from jax.experimental.pallas import tpu as pltpu

F32 = jnp.float32
BF16 = jnp.bfloat16

N_HEADS = 8
N_GROUPS = 8
N_SLABS = 9
SLAB_U, SLAB_V, SLAB_Z, SLAB_Q, SLAB_K, SLAB_BV, SLAB_BZ, SLAB_GA, SLAB_GB = range(N_SLABS)
ST = 512
EPS = 1e-6
SUBLN_EPS = 1e-5
LAMBDA_INIT = 0.8 - 0.6 * math.exp(-0.3 * 0)
LOG2E = 1.0 / math.log(2.0)

V7X_VMEM_BYTES = 64 * 1024 * 1024
VMEM_LIMIT = V7X_VMEM_BYTES * 7 // 8


def _sigmoid(x):
    return 0.5 * jnp.tanh(0.5 * x) + 0.5


def _silu(x):
    hx = 0.5 * x
    return hx * jnp.tanh(hx) + hx


def _gelu_tanh(x):
    c = math.sqrt(2.0 / math.pi)
    hx = 0.5 * x
    return hx * jnp.tanh(x * ((0.044715 * c) * (x * x) + c)) + hx


def _params(*sem):
    return pltpu.CompilerParams(dimension_semantics=sem, vmem_limit_bytes=VMEM_LIMIT)


def _ada_kernel(c_ref, w_ref, b_ref, o_ref):
    c = c_ref[...]
    o_ref[...] = jnp.dot(_silu(c), w_ref[...], preferred_element_type=F32) + b_ref[...]


def _ada(c, w_ada, b_ada, tn=1024):
    bsz, d = c.shape
    n = w_ada.shape[1]
    return pl.pallas_call(
        _ada_kernel,
        grid=(n // tn,),
        in_specs=[
            pl.BlockSpec((bsz, d), lambda j: (0, 0)),
            pl.BlockSpec((d, tn), lambda j: (0, j)),
            pl.BlockSpec((1, tn), lambda j: (0, j)),
        ],
        out_specs=pl.BlockSpec((bsz, tn), lambda j: (0, j)),
        out_shape=jax.ShapeDtypeStruct((bsz, n), F32),
        compiler_params=_params("parallel"),
        name="ada",
    )(c, w_ada, b_ada.reshape(1, n))


def _inproj_kernel(*refs, nx, rows, per, q_scale):
    x_refs = refs[:nx]
    g_ref, shift_ref, scale_ref, w_ref, o_ref, h_ref = refs[nx:]
    j = pl.program_id(1)

    @pl.when(j == 0)
    def _():
        gsc = g_ref[...] * (1.0 + scale_ref[...])
        sh = shift_ref[...]
        for k, x_ref in enumerate(x_refs):
            base = k * x_ref.shape[0]

            def body(r, carry, x_ref=x_ref, base=base):
                sl = pl.ds(pl.multiple_of(r * rows, rows), rows)
                x = x_ref[sl, :]
                rstd = lax.rsqrt(jnp.mean(x * x, axis=-1, keepdims=True) + EPS)
                h_ref[pl.ds(pl.multiple_of(base + r * rows, rows), rows), :] = (
                    ((x * rstd) * gsc + sh).astype(BF16))
                return carry

            lax.fori_loop(0, x_ref.shape[0] // rows, body, 0, unroll=4)

    n_tok, n_feat = o_ref.shape[0], o_ref.shape[1]
    slab = j // per
    narrow = lambda f: (lambda acc: f(acc.astype(BF16)))
    epilogues = (
        ((SLAB_U, SLAB_V), narrow(_gelu_tanh)),
        ((SLAB_Z,), narrow(_silu)),
        ((SLAB_GA, SLAB_GB), narrow(_sigmoid)),
        ((SLAB_Q,), lambda acc: acc * q_scale),
        ((SLAB_K,), lambda acc: acc),
    )
    for slabs, act in epilogues:
        @pl.when(functools.reduce(jnp.logical_or, [slab == s for s in slabs]))
        def _(act=act):
            acc = act(jnp.dot(h_ref[...], w_ref[...].astype(BF16), preferred_element_type=F32))
            for a in range(n_tok):
                for b in range(n_feat):
                    o_ref[a, b] = acc[a * ST:(a + 1) * ST, b * ST:(b + 1) * ST].astype(o_ref.dtype)

    for s_t, act in ((SLAB_BV, lambda acc: acc), (SLAB_BZ, narrow(_silu))):
        @pl.when(slab == s_t)
        def _(act=act):
            acc_t = act(lax.dot_general(w_ref[...].astype(BF16), h_ref[...], (((0,), (1,)), ((), ())),
                                        preferred_element_type=F32))
            for a in range(n_tok):
                for b in range(n_feat):
                    o_ref[a, b] = acc_t[b * ST:(b + 1) * ST, a * ST:(a + 1) * ST].astype(o_ref.dtype)


def _inproj(x2, mod3, gain, w_in, q_scale, seq, tm=1024, tn=1024):
    m, d = x2.shape
    n = w_in.shape[1]
    slab = n // N_SLABS
    per = slab // tn
    tiles_per_seq = seq // tm
    ni, nj = m // tm, n // tn
    nx = 4
    switch = [1 + k for k in range(nx)]
    assert tm % (nx * 32) == 0 and switch[-1] < nj

    def x_slice(k):
        return pl.BlockSpec(
            (tm // nx, d),
            lambda i, j: (jnp.minimum(i + (j >= switch[k]).astype(jnp.int32), ni - 1) * nx + k, 0))

    return pl.pallas_call(
        functools.partial(_inproj_kernel, nx=nx, rows=32, per=per, q_scale=q_scale),
        grid=(ni, nj),
        in_specs=[x_slice(k) for k in range(nx)] + [
            pl.BlockSpec((1, d), lambda i, j: (0, 0)),
            pl.BlockSpec((None, 1, d), lambda i, j: (i // tiles_per_seq, 0, 0)),
            pl.BlockSpec((None, 1, d), lambda i, j: (i // tiles_per_seq, 0, 1)),
            pl.BlockSpec((d, tn), lambda i, j: (0, j)),
        ],
        out_specs=pl.BlockSpec((None, tm // ST, tn // ST, ST, ST),
                               lambda i, j: (j // per, i, j % per, 0, 0)),
        out_shape=jax.ShapeDtypeStruct((N_SLABS, m // ST, slab // ST, ST, ST), BF16),
        scratch_shapes=[pltpu.VMEM((tm, d), BF16)],
        compiler_params=_params("parallel", "arbitrary"),
        name="inproj",
    )(*([x2] * nx), gain, mod3, mod3, w_in)


def _gmlp_pieces(u_ref, v_ref, z_ref, lng_ref, lnb_ref, ws_ref, bs_ref, o_ref, chunk, gd):
    n_ft, ta, _ = u_ref.shape
    d = n_ft * ST
    state = {}

    def layernorm(c):
        rows = pl.ds(c * chunk, chunk)
        gv = [v_ref[f, rows, :].astype(F32) for f in range(n_ft)]
        mu = sum(jnp.sum(t, axis=-1, keepdims=True) for t in gv) * (1.0 / d)
        xc = [t - mu for t in gv]
        var = sum(jnp.sum(t * t, axis=-1, keepdims=True) for t in xc) * (1.0 / d)
        rstd = lax.rsqrt(var + EPS)
        state[c] = [((xc[f] * rstd) * lng_ref[:, f * ST:(f + 1) * ST]
                     + lnb_ref[:, f * ST:(f + 1) * ST]).astype(BF16) for f in range(n_ft)]

    def group(c, g):
        rows = pl.ds(c * chunk, chunk)
        row = lax.broadcasted_iota(jnp.int32, (chunk, chunk), 0)
        col = lax.broadcasted_iota(jnp.int32, (chunk, chunk), 1)
        f, off = divmod(g * gd, ST)
        ws = (ws_ref[g] * (row >= col).astype(F32)).astype(BF16)
        sv = jnp.dot(ws, state[c][f][:, off:off + gd], preferred_element_type=F32)
        sv = sv + bs_ref[:, g * gd:(g + 1) * gd]
        u = u_ref[f, rows, off:off + gd].astype(F32)
        z = z_ref[f, rows, off:off + gd].astype(F32)
        o_ref[rows, g * gd:(g + 1) * gd] = ((u * sv) * z).astype(o_ref.dtype)

    steps = []
    for c in range(ta // chunk):
        steps.append(functools.partial(layernorm, c))
        steps += [functools.partial(group, c, g) for g in range(N_GROUPS)]
    return steps


def _attn_kernel(slopes_ref, lq1_ref, lk1_ref, lq2_ref, lk2_ref, q_ref, k_ref, vt_ref, zt_ref,
                 sg_ref, wa_ref, wb_ref, wo_ref, o_ref, wa_bf_ref, wb_bf_ref, wo_bf_ref,
                 bias_ref, vta_ref, s_ref, smax_ref, acc_ref, *, dh):
    nq, blk, dv = q_ref.shape
    for src, dst in ((wa_ref, wa_bf_ref), (wb_ref, wb_bf_ref), (wo_ref, wo_bf_ref)):
        dst[...] = src[...].astype(dst.dtype)
    slope = slopes_ref[pl.program_id(0)] * LOG2E

    @pl.when(pl.program_id(1) == 0)
    def _():
        s_pos = lax.broadcasted_iota(jnp.int32, (blk, blk), 0)
        t_pos = lax.broadcasted_iota(jnp.int32, (blk, blk), 1)
        bias = slope * (s_pos - t_pos).astype(F32)
        bias_ref[0] = bias
        bias_ref[1] = jnp.where(s_pos <= t_pos, bias, -jnp.inf)

    for j in range(nq):
        vta_ref[j, :dv, :] = vt_ref[j]
        vta_ref[j, dv:, :] = jnp.ones((vta_ref.shape[1] - dv, blk), vta_ref.dtype)

    lam_init = jnp.float32(LAMBDA_INIT)
    lam = (jnp.exp(jnp.sum(lq1_ref[...] * lk1_ref[...], axis=-1, keepdims=True))
           - jnp.exp(jnp.sum(lq2_ref[...] * lk2_ref[...], axis=-1, keepdims=True))
           + lam_init)
    gain = sg_ref[...] * (1.0 - lam_init)

    def scores(j, n):
        slot = j % 2
        bias = bias_ref[int(j == n)]
        for c in range(2):
            q = q_ref[n, :, c * dh:(c + 1) * dh]
            k = k_ref[j, :, c * dh:(c + 1) * dh]
            s = lax.dot_general(k, q, (((1,), (1,)), ((), ())), preferred_element_type=F32)
            s = s + bias
            s_ref[slot, c] = s
            smax_ref[slot, c] = jnp.max(s, axis=0, keepdims=True)

    def consume(j, n, m):
        slot = j % 2
        vta = vta_ref[j]
        off = slope * float((j - n) * blk)
        for c in range(2):
            m_new = smax_ref[slot, c] + off
            if m[c] is not None:
                m_new = jnp.maximum(m[c], m_new)
            p = jnp.exp2((s_ref[slot, c] - (m_new - off)).astype(vta.dtype))
            pv = jnp.dot(vta, p, preferred_element_type=F32)
            if m[c] is None:
                acc_ref[c] = pv
            else:
                acc_ref[c] = jnp.exp2(m[c] - m_new) * acc_ref[c] + pv
            m[c] = m_new

    hb = blk // 2
    nt = (((1,), (1,)), ((), ()))

    def scores_diag(n):
        slot = n % 2
        for c in range(2):
            q = q_ref[n, :, c * dh:(c + 1) * dh]
            k = k_ref[n, :, c * dh:(c + 1) * dh]
            sa = lax.dot_general(k[:hb], q, nt, preferred_element_type=F32) + bias_ref[1, :hb, :]
            sb = (lax.dot_general(k[hb:], q[hb:], nt, preferred_element_type=F32)
                  + bias_ref[1, hb:, hb:])
            s_ref[slot, c, :hb, :] = sa
            s_ref[slot, c, hb:, hb:] = sb
            ma = jnp.max(sa, axis=0, keepdims=True)
            smax_ref[slot, c, :, :hb] = ma[:, :hb]
            smax_ref[slot, c, :, hb:] = jnp.maximum(ma[:, hb:], jnp.max(sb, axis=0, keepdims=True))

    def consume_diag(n, m):
        slot = n % 2
        vta = vta_ref[n]
        for c in range(2):
            m_new = smax_ref[slot, c]
            if m[c] is not None:
                m_new = jnp.maximum(m[c], m_new)
            pa = jnp.exp2((s_ref[slot, c, :hb, :] - m_new).astype(vta.dtype))
            pb = jnp.exp2((s_ref[slot, c, hb:, hb:] - m_new[:, hb:]).astype(vta.dtype))
            pva = jnp.dot(vta[:, :hb], pa, preferred_element_type=F32)
            pvb = jnp.dot(vta[:, hb:], pb, preferred_element_type=F32)
            if m[c] is None:
                acc_ref[c, :, :hb] = pva[:, :hb]
                acc_ref[c, :, hb:] = pva[:, hb:] + pvb
            else:
                alpha = jnp.exp2(m[c] - m_new)
                acc_ref[c, :, :hb] = alpha[:, :hb] * acc_ref[c, :, :hb] + pva[:, :hb]
                acc_ref[c, :, hb:] = alpha[:, hb:] * acc_ref[c, :, hb:] + (pva[:, hb:] + pvb)
            m[c] = m_new

    for n in range(nq):
        m = [None, None]
        if n == 0:
            scores_diag(0)
        else:
            scores(0, n)
        for j in range(n):
            if j + 1 == n:
                scores_diag(n)
            else:
                scores(j + 1, n)
            consume(j, n, m)
        consume_diag(n, m)

        a0, a1 = acc_ref[0], acc_ref[1]
        r0 = 1.0 / a0[dv:dv + 1]
        r1 = lam * (1.0 / a1[dv:dv + 1])
        o = a0[:dv] * r0 - a1[:dv] * r1
        o = o * lax.rsqrt(jnp.mean(o * o, axis=0, keepdims=True) + SUBLN_EPS)
        o = (o * gain) * zt_ref[n].astype(F32)
        o_ref[:, n * blk:(n + 1) * blk] = o.astype(o_ref.dtype)


def _attn(proj, slopes, lq1, lk1, lq2, lk2, subln_g, weights, bsz, seq):
    _, n_tt, n_ft, _, _ = proj.shape
    m, d = n_tt * ST, n_ft * ST
    dv = d // N_HEADS
    dh = dv // 2
    nq = seq // ST
    hp = ST // dv
    ones_rows = 16
    tok_major = lambda s: pl.BlockSpec((None, nq, None, ST, dv),
                                       lambda h, b, s=s: (s, b, h // hp, 0, h % hp))
    feat_major = lambda s: pl.BlockSpec((None, nq, None, dv, ST),
                                        lambda h, b, s=s: (s, b, h // hp, h % hp, 0))
    vec = pl.BlockSpec((1, dh), lambda h, b: (0, 0))
    w_rows = d // (N_HEADS * bsz)
    assert w_rows * N_HEADS * bsz == d and w_rows % 16 == 0
    w_slice = pl.BlockSpec((w_rows, d), lambda h, b: (h * bsz + b, 0))
    return pl.pallas_call(
        functools.partial(_attn_kernel, dh=dh),
        grid=(N_HEADS, bsz),
        in_specs=[
            pl.BlockSpec(memory_space=pltpu.SMEM),
            vec, vec, vec, vec,
            tok_major(SLAB_Q), tok_major(SLAB_K), feat_major(SLAB_BV), feat_major(SLAB_BZ),
            pl.BlockSpec((dv, 1), lambda h, b: (0, 0)),
            w_slice, w_slice, w_slice,
        ],
        out_specs=[pl.BlockSpec((dv, seq), lambda h, b: (h, b)), w_slice, w_slice, w_slice],
        out_shape=[jax.ShapeDtypeStruct((d, m), BF16)] + [jax.ShapeDtypeStruct((d, d), BF16)] * 3,
        scratch_shapes=[
            pltpu.VMEM((2, ST, ST), F32),
            pltpu.VMEM((nq, dv + ones_rows, ST), BF16),
            pltpu.VMEM((2, 2, ST, ST), F32),
            pltpu.VMEM((2, 2, 1, ST), F32),
            pltpu.VMEM((2, dv + ones_rows, ST), F32),
        ],
        compiler_params=_params("arbitrary", "arbitrary"),
        name="attn",
    )(slopes, lq1, lk1, lq2, lk2, proj, proj, proj, proj, subln_g, *weights)


def _tail_kernel(u_ref, v_ref, z_ref, ga_ref, gb_ref, ybt_ref, x_ref, gate_ref, wa_ref, wb_ref,
                 wo_ref, lng_ref, lnb_ref, ws_ref, bs_ref, fg_ref, o_ref, ya_ref, b_ref, m_ref, *,
                 chunk, gd, nc):
    d = wb_ref.shape[1]
    cols = [slice(k, k + nc) for k in range(0, d, nc)]

    gm = _gmlp_pieces(u_ref, v_ref, z_ref, lng_ref, lnb_ref, ws_ref, bs_ref, ya_ref, chunk, gd)
    per = -(-len(gm) // len(cols))
    for k, cs in enumerate(cols):
        b_ref[:, cs] = lax.dot_general(ybt_ref[...], wb_ref[:, cs], (((0,), (0,)), ((), ())),
                                       preferred_element_type=F32)
        for step in gm[k * per:(k + 1) * per]:
            step()

    def merge(cs, a):
        f, off = divmod(cs.start, ST)
        ga = ga_ref[f, :, off:off + nc].astype(F32)
        gb = gb_ref[f, :, off:off + nc].astype(F32)
        m_ref[:, cs] = (ga * a + gb * b_ref[:, cs]).astype(m_ref.dtype)

    pending = None
    for cs in cols:
        a = jnp.dot(ya_ref[...], wa_ref[:, cs], preferred_element_type=F32)
        if pending is not None:
            merge(*pending)
        pending = (cs, a)
    merge(*pending)

    ssq = None
    for cs in cols:
        out = jnp.dot(m_ref[...], wo_ref[:, cs], preferred_element_type=F32)
        r = x_ref[:, cs] + gate_ref[:, cs] * out
        o_ref[:, cs] = r
        part = jnp.sum(r * r, axis=-1, keepdims=True)
        ssq = part if ssq is None else ssq + part
    rstd = lax.rsqrt(ssq * (1.0 / d) + EPS)
    o_ref[...] = (o_ref[...] * rstd) * fg_ref[...]


def _tail(proj, ybt, x2, mod3, wa, wb, wo, ln_g, ln_b, w_s, bias_full, fgain, seq, tm=256):
    _, n_tt, n_ft, _, _ = proj.shape
    m, d = x2.shape
    chunk = w_s.shape[-1]
    assert ST % tm == 0 and tm % chunk == 0 and seq % tm == 0
    per = ST // tm
    tiles_per_seq = seq // tm
    pspec = lambda s: pl.BlockSpec((None, None, n_ft, tm, ST),
                                   lambda i, s=s: (s, i // per, 0, i % per, 0))
    rows = pl.BlockSpec((tm, d), lambda i: (i, 0))
    vec = pl.BlockSpec((1, d), lambda i: (0, 0))
    weight = pl.BlockSpec((d, d), lambda i: (0, 0), pipeline_mode=pl.Buffered(1))
    return pl.pallas_call(
        functools.partial(_tail_kernel, chunk=chunk, gd=d // N_GROUPS, nc=512),
        grid=(m // tm,),
        in_specs=[
            pspec(SLAB_U), pspec(SLAB_V), pspec(SLAB_Z), pspec(SLAB_GA), pspec(SLAB_GB),
            pl.BlockSpec((d, tm), lambda i: (0, i)), rows,
            pl.BlockSpec((None, 1, d), lambda i: (i // tiles_per_seq, 0, 2)),
            weight, weight, weight,
            vec, vec,
            pl.BlockSpec((N_GROUPS, chunk, chunk), lambda i: (0, 0, 0)),
            pl.BlockSpec((chunk, d), lambda i: (0, 0)),
            vec,
        ],
        out_specs=rows,
        out_shape=jax.ShapeDtypeStruct((m, d), F32),
        scratch_shapes=[pltpu.VMEM((tm, d), BF16), pltpu.VMEM((tm, d), F32),
                        pltpu.VMEM((tm, d), BF16)],
        compiler_params=_params("parallel"),
        name="tail",
    )(proj, proj, proj, proj, proj, ybt, x2, mod3, wa, wb, wo, ln_g, ln_b, w_s, bias_full, fgain)


def kernel(x, c, w_ada, b_ada, norm_gain, w_in, ln_v_gain, ln_v_bias, w_spatial, b_spatial,
           lambda_q1, lambda_k1, lambda_q2, lambda_k2, subln_gain, w_branch_a, w_branch_b,
           w_out, final_norm_gain):
    bsz, seq, d = x.shape
    depth = w_ada.shape[0]
    assert depth == 1, "single-layer trunk"
    dh = lambda_q1.shape[-1]
    assert d == N_HEADS * 2 * dh and w_in.shape[-1] == N_SLABS * d
    assert seq % ST == 0 and d % ST == 0 and ST % (2 * dh) == 0
    l = 0

    x2 = x.reshape(bsz * seq, d)
    mod = _ada(c, w_ada[l], b_ada[l])
    mod3 = mod.reshape(bsz, 1, 3 * d)

    proj = _inproj(x2, mod3, norm_gain[l].reshape(1, d), w_in[l],
                   LOG2E / math.sqrt(dh), seq)

    bias_full = jnp.repeat(jnp.transpose(b_spatial[l]), d // N_GROUPS, axis=1)

    slopes = jnp.asarray([2.0 ** (-8.0 * (i + 1) / N_HEADS) for i in range(N_HEADS)], F32)
    ybt, wa, wb, wo = _attn(proj, slopes, lambda_q1[l].reshape(1, dh), lambda_k1[l].reshape(1, dh),
                            lambda_q2[l].reshape(1, dh), lambda_k2[l].reshape(1, dh),
                            subln_gain[l].reshape(2 * dh, 1),
                            (w_branch_a[l], w_branch_b[l], w_out[l]), bsz, seq)

    out = _tail(proj, ybt, x2, mod3, wa, wb, wo, ln_v_gain[l].reshape(1, d),
                ln_v_bias[l].reshape(1, d),
                w_spatial[l], bias_full, final_norm_gain.reshape(1, d), seq)
    return out.reshape(bsz, seq, d)
```

```python
import functools
import math

import jax
import jax.numpy as jnp
from jax import lax
from jax.experimental import pallas as pl
from jax.experimental.pallas import tpu as pltpu

F32 = jnp.float32
BF16 = jnp.bfloat16

N_HEADS = 8
N_GROUPS = 8
N_SLABS = 9
SLAB_U, SLAB_V, SLAB_Z, SLAB_Q, SLAB_K, SLAB_BV, SLAB_BZ, SLAB_GA, SLAB_GB = range(N_SLABS)
ST = 512
EPS = 1e-6
SUBLN_EPS = 1e-5
LAMBDA_INIT = 0.8 - 0.6 * math.exp(-0.3 * 0)
LOG2E = 1.0 / math.log(2.0)

V7X_VMEM_BYTES = 64 * 1024 * 1024
VMEM_LIMIT = V7X_VMEM_BYTES * 7 // 8


def _sigmoid(x):
    return 0.5 * jnp.tanh(0.5 * x) + 0.5


def _silu(x):
    hx = 0.5 * x
    return hx * jnp.tanh(hx) + hx


def _gelu_tanh(x):
    c = math.sqrt(2.0 / math.pi)
    hx = 0.5 * x
    return hx * jnp.tanh(x * ((0.044715 * c) * (x * x) + c)) + hx


def _params(*sem):
    return pltpu.CompilerParams(dimension_semantics=sem, vmem_limit_bytes=VMEM_LIMIT)


def _ada_kernel(c_ref, w_ref, b_ref, o_ref):
    c = c_ref[...]
    o_ref[...] = jnp.dot(_silu(c), w_ref[...], preferred_element_type=F32) + b_ref[...]


def _ada(c, w_ada, b_ada, tn=1024):
    bsz, d = c.shape
    n = w_ada.shape[1]
    return pl.pallas_call(
        _ada_kernel,
        grid=(n // tn,),
        in_specs=[
            pl.BlockSpec((bsz, d), lambda j: (0, 0)),
            pl.BlockSpec((d, tn), lambda j: (0, j)),
            pl.BlockSpec((1, tn), lambda j: (0, j)),
        ],
        out_specs=pl.BlockSpec((bsz, tn), lambda j: (0, j)),
        out_shape=jax.ShapeDtypeStruct((bsz, n), F32),
        compiler_params=_params("parallel"),
        name="ada",
    )(c, w_ada, b_ada.reshape(1, n))


def _inproj_kernel(*refs, nx, rows, per, q_scale):
    x_refs = refs[:nx]
    g_ref, shift_ref, scale_ref, w_ref, o_ref, h_ref = refs[nx:]
    j = pl.program_id(1)

    @pl.when(j == 0)
    def _():
        gsc = g_ref[...] * (1.0 + scale_ref[...])
        sh = shift_ref[...]
        for k, x_ref in enumerate(x_refs):
            base = k * x_ref.shape[0]

            def body(r, carry, x_ref=x_ref, base=base):
                sl = pl.ds(pl.multiple_of(r * rows, rows), rows)
                x = x_ref[sl, :]
                rstd = lax.rsqrt(jnp.mean(x * x, axis=-1, keepdims=True) + EPS)
                h_ref[pl.ds(pl.multiple_of(base + r * rows, rows), rows), :] = (
                    ((x * rstd) * gsc + sh).astype(BF16))
                return carry

            lax.fori_loop(0, x_ref.shape[0] // rows, body, 0, unroll=4)

    n_tok, n_feat = o_ref.shape[0], o_ref.shape[1]
    slab = j // per
    narrow = lambda f: (lambda acc: f(acc.astype(BF16)))
    epilogues = (
        ((SLAB_U, SLAB_V), narrow(_gelu_tanh)),
        ((SLAB_Z,), narrow(_silu)),
        ((SLAB_GA, SLAB_GB), narrow(_sigmoid)),
        ((SLAB_Q,), lambda acc: acc * q_scale),
        ((SLAB_K,), lambda acc: acc),
    )
    for slabs, act in epilogues:
        @pl.when(functools.reduce(jnp.logical_or, [slab == s for s in slabs]))
        def _(act=act):
            acc = act(jnp.dot(h_ref[...], w_ref[...].astype(BF16), preferred_element_type=F32))
            for a in range(n_tok):
                for b in range(n_feat):
                    o_ref[a, b] = acc[a * ST:(a + 1) * ST, b * ST:(b + 1) * ST].astype(o_ref.dtype)

    for s_t, act in ((SLAB_BV, lambda acc: acc), (SLAB_BZ, narrow(_silu))):
        @pl.when(slab == s_t)
        def _(act=act):
            acc_t = act(lax.dot_general(w_ref[...].astype(BF16), h_ref[...], (((0,), (1,)), ((), ())),
                                        preferred_element_type=F32))
            for a in range(n_tok):
                for b in range(n_feat):
                    o_ref[a, b] = acc_t[b * ST:(b + 1) * ST, a * ST:(a + 1) * ST].astype(o_ref.dtype)


def _inproj(x2, mod3, gain, w_in, q_scale, seq, tm=1024, tn=1024):
    m, d = x2.shape
    n = w_in.shape[1]
    slab = n // N_SLABS
    per = slab // tn
    tiles_per_seq = seq // tm
    ni, nj = m // tm, n // tn
    nx = 4
    switch = [1 + k for k in range(nx)]
    assert tm % (nx * 32) == 0 and switch[-1] < nj

    def x_slice(k):
        return pl.BlockSpec(
            (tm // nx, d),
            lambda i, j: (jnp.minimum(i + (j >= switch[k]).astype(jnp.int32), ni - 1) * nx + k, 0))

    return pl.pallas_call(
        functools.partial(_inproj_kernel, nx=nx, rows=32, per=per, q_scale=q_scale),
        grid=(ni, nj),
        in_specs=[x_slice(k) for k in range(nx)] + [
            pl.BlockSpec((1, d), lambda i, j: (0, 0)),
            pl.BlockSpec((None, 1, d), lambda i, j: (i // tiles_per_seq, 0, 0)),
            pl.BlockSpec((None, 1, d), lambda i, j: (i // tiles_per_seq, 0, 1)),
            pl.BlockSpec((d, tn), lambda i, j: (0, j)),
        ],
        out_specs=pl.BlockSpec((None, tm // ST, tn // ST, ST, ST),
                               lambda i, j: (j // per, i, j % per, 0, 0)),
        out_shape=jax.ShapeDtypeStruct((N_SLABS, m // ST, slab // ST, ST, ST), BF16),
        scratch_shapes=[pltpu.VMEM((tm, d), BF16)],
        compiler_params=_params("parallel", "arbitrary"),
        name="inproj",
    )(*([x2] * nx), gain, mod3, mod3, w_in)


def _gmlp_pieces(u_ref, v_ref, z_ref, lng_ref, lnb_ref, ws_ref, bs_ref, o_ref, chunk, gd):
    n_ft, ta, _ = u_ref.shape
    d = n_ft * ST
    state = {}

    def layernorm(c):
        rows = pl.ds(c * chunk, chunk)
        gv = [v_ref[f, rows, :].astype(F32) for f in range(n_ft)]
        mu = sum(jnp.sum(t, axis=-1, keepdims=True) for t in gv) * (1.0 / d)
        xc = [t - mu for t in gv]
        var = sum(jnp.sum(t * t, axis=-1, keepdims=True) for t in xc) * (1.0 / d)
        rstd = lax.rsqrt(var + EPS)
        state[c] = [((xc[f] * rstd) * lng_ref[:, f * ST:(f + 1) * ST]
                     + lnb_ref[:, f * ST:(f + 1) * ST]).astype(BF16) for f in range(n_ft)]

    def group(c, g):
        rows = pl.ds(c * chunk, chunk)
        row = lax.broadcasted_iota(jnp.int32, (chunk, chunk), 0)
        col = lax.broadcasted_iota(jnp.int32, (chunk, chunk), 1)
        f, off = divmod(g * gd, ST)
        ws = (ws_ref[g] * (row >= col).astype(F32)).astype(BF16)
        sv = jnp.dot(ws, state[c][f][:, off:off + gd], preferred_element_type=F32)
        sv = sv + bs_ref[:, g * gd:(g + 1) * gd]
        u = u_ref[f, rows, off:off + gd].astype(F32)
        z = z_ref[f, rows, off:off + gd].astype(F32)
        o_ref[rows, g * gd:(g + 1) * gd] = ((u * sv) * z).astype(o_ref.dtype)

    steps = []
    for c in range(ta // chunk):
        steps.append(functools.partial(layernorm, c))
        steps += [functools.partial(group, c, g) for g in range(N_GROUPS)]
    return steps


def _attn_kernel(slopes_ref, lq1_ref, lk1_ref, lq2_ref, lk2_ref, q_ref, k_ref, vt_ref, zt_ref,
                 sg_ref, wa_ref, wb_ref, wo_ref, o_ref, wa_bf_ref, wb_bf_ref, wo_bf_ref,
                 bias_ref, vta_ref, s_ref, smax_ref, acc_ref, *, dh):
    nq, blk, dv = q_ref.shape
    for src, dst in ((wa_ref, wa_bf_ref), (wb_ref, wb_bf_ref), (wo_ref, wo_bf_ref)):
        dst[...] = src[...].astype(dst.dtype)
    slope = slopes_ref[pl.program_id(0)] * LOG2E

    @pl.when(pl.program_id(1) == 0)
    def _():
        s_pos = lax.broadcasted_iota(jnp.int32, (blk, blk), 0)
        t_pos = lax.broadcasted_iota(jnp.int32, (blk, blk), 1)
        bias = slope * (s_pos - t_pos).astype(F32)
        bias_ref[0] = bias
        bias_ref[1] = jnp.where(s_pos <= t_pos, bias, -jnp.inf)

    for j in range(nq):
        vta_ref[j, :dv, :] = vt_ref[j]
        vta_ref[j, dv:, :] = jnp.ones((vta_ref.shape[1] - dv, blk), vta_ref.dtype)

    lam_init = jnp.float32(LAMBDA_INIT)
    lam = (jnp.exp(jnp.sum(lq1_ref[...] * lk1_ref[...], axis=-1, keepdims=True))
           - jnp.exp(jnp.sum(lq2_ref[...] * lk2_ref[...], axis=-1, keepdims=True))
           + lam_init)
    gain = sg_ref[...] * (1.0 - lam_init)

    def scores(j, n):
        slot = j % 2
        bias = bias_ref[int(j == n)]
        for c in range(2):
            q = q_ref[n, :, c * dh:(c + 1) * dh]
            k = k_ref[j, :, c * dh:(c + 1) * dh]
            s = lax.dot_general(k, q, (((1,), (1,)), ((), ())), preferred_element_type=F32)
            s = s + bias
            s_ref[slot, c] = s
            smax_ref[slot, c] = jnp.max(s, axis=0, keepdims=True)

    def consume(j, n, m):
        slot = j % 2
        vta = vta_ref[j]
        off = slope * float((j - n) * blk)
        for c in range(2):
            m_new = smax_ref[slot, c] + off
            if m[c] is not None:
                m_new = jnp.maximum(m[c], m_new)
            p = jnp.exp2((s_ref[slot, c] - (m_new - off)).astype(vta.dtype))
            pv = jnp.dot(vta, p, preferred_element_type=F32)
            if m[c] is None:
                acc_ref[c] = pv
            else:
                acc_ref[c] = jnp.exp2(m[c] - m_new) * acc_ref[c] + pv
            m[c] = m_new

    hb = blk // 2
    nt = (((1,), (1,)), ((), ()))

    def scores_diag(n):
        slot = n % 2
        for c in range(2):
            q = q_ref[n, :, c * dh:(c + 1) * dh]
            k = k_ref[n, :, c * dh:(c + 1) * dh]
            sa = lax.dot_general(k[:hb], q, nt, preferred_element_type=F32) + bias_ref[1, :hb, :]
            sb = (lax.dot_general(k[hb:], q[hb:], nt, preferred_element_type=F32)
                  + bias_ref[1, hb:, hb:])
            s_ref[slot, c, :hb, :] = sa
            s_ref[slot, c, hb:, hb:] = sb
            ma = jnp.max(sa, axis=0, keepdims=True)
            smax_ref[slot, c, :, :hb] = ma[:, :hb]
            smax_ref[slot, c, :, hb:] = jnp.maximum(ma[:, hb:], jnp.max(sb, axis=0, keepdims=True))

    def consume_diag(n, m):
        slot = n % 2
        vta = vta_ref[n]
        for c in range(2):
            m_new = smax_ref[slot, c]
            if m[c] is not None:
                m_new = jnp.maximum(m[c], m_new)
            pa = jnp.exp2((s_ref[slot, c, :hb, :] - m_new).astype(vta.dtype))
            pb = jnp.exp2((s_ref[slot, c, hb:, hb:] - m_new[:, hb:]).astype(vta.dtype))
            pva = jnp.dot(vta[:, :hb], pa, preferred_element_type=F32)
            pvb = jnp.dot(vta[:, hb:], pb, preferred_element_type=F32)
            if m[c] is None:
                acc_ref[c, :, :hb] = pva[:, :hb]
                acc_ref[c, :, hb:] = pva[:, hb:] + pvb
            else:
                alpha = jnp.exp2(m[c] - m_new)
                acc_ref[c, :, :hb] = alpha[:, :hb] * acc_ref[c, :, :hb] + pva[:, :hb]
                acc_ref[c, :, hb:] = alpha[:, hb:] * acc_ref[c, :, hb:] + (pva[:, hb:] + pvb)
            m[c] = m_new

    for n in range(nq):
        m = [None, None]
        if n == 0:
            scores_diag(0)
        else:
            scores(0, n)
        for j in range(n):
            if j + 1 == n:
                scores_diag(n)
            else:
                scores(j + 1, n)
            consume(j, n, m)
        consume_diag(n, m)

        a0, a1 = acc_ref[0], acc_ref[1]
        r0 = 1.0 / a0[dv:dv + 1]
        r1 = lam * (1.0 / a1[dv:dv + 1])
        o = a0[:dv] * r0 - a1[:dv] * r1
        o = o * lax.rsqrt(jnp.mean(o * o, axis=0, keepdims=True) + SUBLN_EPS)
        o_ref[:, n * blk:(n + 1) * blk] = (o * gain).astype(o_ref.dtype) * zt_ref[n]


def _attn(proj, slopes, lq1, lk1, lq2, lk2, subln_g, weights, bsz, seq):
    _, n_tt, n_ft, _, _ = proj.shape
    m, d = n_tt * ST, n_ft * ST
    dv = d // N_HEADS
    dh = dv // 2
    nq = seq // ST
    hp = ST // dv
    ones_rows = 16
    tok_major = lambda s: pl.BlockSpec((None, nq, None, ST, dv),
                                       lambda h, b, s=s: (s, b, h // hp, 0, h % hp))
    feat_major = lambda s: pl.BlockSpec((None, nq, None, dv, ST),
                                        lambda h, b, s=s: (s, b, h // hp, h % hp, 0))
    vec = pl.BlockSpec((1, dh), lambda h, b: (0, 0))
    w_rows = d // (N_HEADS * bsz)
    assert w_rows * N_HEADS * bsz == d and w_rows % 16 == 0
    w_slice = pl.BlockSpec((w_rows, d), lambda h, b: (h * bsz + b, 0))
    return pl.pallas_call(
        functools.partial(_attn_kernel, dh=dh),
        grid=(N_HEADS, bsz),
        in_specs=[
            pl.BlockSpec(memory_space=pltpu.SMEM),
            vec, vec, vec, vec,
            tok_major(SLAB_Q), tok_major(SLAB_K), feat_major(SLAB_BV), feat_major(SLAB_BZ),
            pl.BlockSpec((dv, 1), lambda h, b: (0, 0)),
            w_slice, w_slice, w_slice,
        ],
        out_specs=[pl.BlockSpec((dv, seq), lambda h, b: (h, b)), w_slice, w_slice, w_slice],
        out_shape=[jax.ShapeDtypeStruct((d, m), BF16)] + [jax.ShapeDtypeStruct((d, d), BF16)] * 3,
        scratch_shapes=[
            pltpu.VMEM((2, ST, ST), F32),
            pltpu.VMEM((nq, dv + ones_rows, ST), BF16),
            pltpu.VMEM((2, 2, ST, ST), F32),
            pltpu.VMEM((2, 2, 1, ST), F32),
            pltpu.VMEM((2, dv + ones_rows, ST), F32),
        ],
        compiler_params=_params("arbitrary", "arbitrary"),
        name="attn",
    )(slopes, lq1, lk1, lq2, lk2, proj, proj, proj, proj, subln_g, *weights)


def _tail_kernel(u_ref, v_ref, z_ref, ga_ref, gb_ref, ybt_ref, x_ref, gate_ref, wa_ref, wb_ref,
                 wo_ref, lng_ref, lnb_ref, ws_ref, bs_ref, fg_ref, o_ref, ya_ref, b_ref, m_ref, *,
                 chunk, gd, nc):
    d = wb_ref.shape[1]
    cols = [slice(k, k + nc) for k in range(0, d, nc)]

    gm = _gmlp_pieces(u_ref, v_ref, z_ref, lng_ref, lnb_ref, ws_ref, bs_ref, ya_ref, chunk, gd)
    per = -(-len(gm) // len(cols))
    for k, cs in enumerate(cols):
        b_ref[:, cs] = lax.dot_general(ybt_ref[...], wb_ref[:, cs], (((0,), (0,)), ((), ())),
                                       preferred_element_type=F32)
        for step in gm[k * per:(k + 1) * per]:
            step()

    def merge(cs, a):
        f, off = divmod(cs.start, ST)
        ga = ga_ref[f, :, off:off + nc].astype(F32)
        gb = gb_ref[f, :, off:off + nc].astype(F32)
        m_ref[:, cs] = (ga * a + gb * b_ref[:, cs]).astype(m_ref.dtype)

    pending = None
    for cs in cols:
        a = jnp.dot(ya_ref[...], wa_ref[:, cs], preferred_element_type=F32)
        if pending is not None:
            merge(*pending)
        pending = (cs, a)
    merge(*pending)

    ssq = None
    for cs in cols:
        out = jnp.dot(m_ref[...], wo_ref[:, cs], preferred_element_type=F32)
        r = x_ref[:, cs] + gate_ref[:, cs] * out
        o_ref[:, cs] = r
        part = jnp.sum(r * r, axis=-1, keepdims=True)
        ssq = part if ssq is None else ssq + part
    rstd = lax.rsqrt(ssq * (1.0 / d) + EPS)
    o_ref[...] = (o_ref[...] * rstd) * fg_ref[...]


def _tail(proj, ybt, x2, mod3, wa, wb, wo, ln_g, ln_b, w_s, bias_full, fgain, seq, tm=256):
    _, n_tt, n_ft, _, _ = proj.shape
    m, d = x2.shape
    chunk = w_s.shape[-1]
    assert ST % tm == 0 and tm % chunk == 0 and seq % tm == 0
    per = ST // tm
    tiles_per_seq = seq // tm
    pspec = lambda s: pl.BlockSpec((None, None, n_ft, tm, ST),
                                   lambda i, s=s: (s, i // per, 0, i % per, 0))
    rows = pl.BlockSpec((tm, d), lambda i: (i, 0))
    vec = pl.BlockSpec((1, d), lambda i: (0, 0))
    weight = pl.BlockSpec((d, d), lambda i: (0, 0), pipeline_mode=pl.Buffered(1))
    return pl.pallas_call(
        functools.partial(_tail_kernel, chunk=chunk, gd=d // N_GROUPS, nc=512),
        grid=(m // tm,),
        in_specs=[
            pspec(SLAB_U), pspec(SLAB_V), pspec(SLAB_Z), pspec(SLAB_GA), pspec(SLAB_GB),
            pl.BlockSpec((d, tm), lambda i: (0, i)), rows,
            pl.BlockSpec((None, 1, d), lambda i: (i // tiles_per_seq, 0, 2)),
            weight, weight, weight,
            vec, vec,
            pl.BlockSpec((N_GROUPS, chunk, chunk), lambda i: (0, 0, 0)),
            pl.BlockSpec((chunk, d), lambda i: (0, 0)),
            vec,
        ],
        out_specs=rows,
        out_shape=jax.ShapeDtypeStruct((m, d), F32),
        scratch_shapes=[pltpu.VMEM((tm, d), BF16), pltpu.VMEM((tm, d), F32),
                        pltpu.VMEM((tm, d), BF16)],
        compiler_params=_params("parallel"),
        name="tail",
    )(proj, proj, proj, proj, proj, ybt, x2, mod3, wa, wb, wo, ln_g, ln_b, w_s, bias_full, fgain)


def kernel(x, c, w_ada, b_ada, norm_gain, w_in, ln_v_gain, ln_v_bias, w_spatial, b_spatial,
           lambda_q1, lambda_k1, lambda_q2, lambda_k2, subln_gain, w_branch_a, w_branch_b,
           w_out, final_norm_gain):
    bsz, seq, d = x.shape
    depth = w_ada.shape[0]
    assert depth == 1, "single-layer trunk"
    dh = lambda_q1.shape[-1]
    assert d == N_HEADS * 2 * dh and w_in.shape[-1] == N_SLABS * d
    assert seq % ST == 0 and d % ST == 0 and ST % (2 * dh) == 0
    l = 0

    x2 = x.reshape(bsz * seq, d)
    mod = _ada(c, w_ada[l], b_ada[l])
    mod3 = mod.reshape(bsz, 1, 3 * d)

    proj = _inproj(x2, mod3, norm_gain[l].reshape(1, d), w_in[l],
                   LOG2E / math.sqrt(dh), seq)

    bias_full = jnp.repeat(jnp.transpose(b_spatial[l]), d // N_GROUPS, axis=1)

    slopes = jnp.asarray([2.0 ** (-8.0 * (i + 1) / N_HEADS) for i in range(N_HEADS)], F32)
    ybt, wa, wb, wo = _attn(proj, slopes, lambda_q1[l].reshape(1, dh), lambda_k1[l].reshape(1, dh),
                            lambda_q2[l].reshape(1, dh), lambda_k2[l].reshape(1, dh),
                            subln_gain[l].reshape(2 * dh, 1),
                            (w_branch_a[l], w_branch_b[l], w_out[l]), bsz, seq)

    out = _tail(proj, ybt, x2, mod3, wa, wb, wo, ln_v_gain[l].reshape(1, d),
                ln_v_bias[l].reshape(1, d),
                w_spatial[l], bias_full, final_norm_gain.reshape(1, d), seq)
    return out.reshape(bsz, seq, d)
```
